```python
import functools
import jax, jax.numpy as jnp
from jax import lax
import numpy as np

D_MODEL = 2048
BATCH = 8
SEQ = 2048
DEPTH = 1
DEC_BATCH = 128
DEC_SEQ = 4
PAST_LEN = 2048
PAGE_SIZE = 128

RW_HEADS = 16
RW_HEAD = 64
RW_WIDTH = RW_HEADS * RW_HEAD
D_DECAY_LORA = 96
D_AAA_LORA = 96
D_GATE_LORA = 128
RW_COLS = 3 * RW_WIDTH + D_DECAY_LORA + D_AAA_LORA + D_GATE_LORA
GN_EPS = 64e-5
ATT_HEADS = 8
KV_HEADS = 2
HEAD_DIM = 128
ATT_WIDTH = ATT_HEADS * HEAD_DIM
KV_WIDTH = KV_HEADS * HEAD_DIM
IDX_HEADS = 16
IDX_DIM = 64
IDX_W_SCALE = (IDX_HEADS * IDX_DIM) ** -0.5
TOPK_MAX = 256
Q_BLOCK = 128
ROPE_THETA = 10000.0
ATT_COLS = ATT_WIDTH + 2 * KV_WIDTH + IDX_HEADS * IDX_DIM + IDX_DIM + IDX_HEADS
GATE_COLS = 2 * D_MODEL
IN_COLS = RW_COLS + ATT_COLS + GATE_COLS
D_FF = 5632
CONV_W = 3
RMS_EPS = 1e-6

kernel_name = "rwkv7_dsa_gated_hybrid_step"


def _split(z, sizes):
    return jnp.split(z, [int(o) for o in np.cumsum(sizes)[:-1]], axis=-1)


def rmsnorm(x, g):
    xf = x.astype(jnp.float32)
    y = xf * lax.rsqrt(jnp.mean(xf * xf, axis=-1, keepdims=True) + RMS_EPS)
    return (y * g.astype(jnp.float32)).astype(x.dtype)


def rope(x, pos):
    half = x.shape[-1] // 2
    inv_freq = 1.0 / (ROPE_THETA ** (jnp.arange(half, dtype=jnp.float32) / half))
    ang = pos.astype(jnp.float32)[:, None] * inv_freq[None, :]
    cos = jnp.cos(ang)[:, None, :]
    sin = jnp.sin(ang)[:, None, :]
    xf = x.astype(jnp.float32)
    x1, x2 = xf[..., :half], xf[..., half:]
    return jnp.concatenate([x1 * cos - x2 * sin, x2 * cos + x1 * sin], axis=-1).astype(x.dtype)


def rwkv7_mix(cols, shift_prev, wkv_prev, mu, w0, w2, a0, a2, g2, k_k, k_a, r_k, lnx_w, lnx_b):
    f32 = jnp.float32
    B, T, _ = cols.shape
    prev = jnp.concatenate([shift_prev[:, None, :].astype(cols.dtype), cols[:, :-1]], axis=1)
    m = cols + (prev - cols) * mu
    r, k, v, wl, al, gl = _split(m, (RW_WIDTH, RW_WIDTH, RW_WIDTH, D_DECAY_LORA, D_AAA_LORA, D_GATE_LORA))
    w_log = -jax.nn.softplus(-(w0 + jnp.tanh(wl) @ w2).astype(f32)) - 0.5
    decay = jnp.exp(-jnp.exp(w_log))
    a = jax.nn.sigmoid((a0 + al @ a2).astype(f32))
    g = jax.nn.sigmoid(gl) @ g2

    def heads(z):
        return z.astype(f32).reshape(B, T, RW_HEADS, RW_HEAD)

    kk = heads(k * k_k)
    kk = kk * lax.rsqrt(jnp.maximum(jnp.sum(kk * kk, axis=-1, keepdims=True), 1e-24))
    k = k.astype(f32) * (1.0 + (a - 1.0) * k_a.astype(f32))
    r_h, k_h, v_h, a_h, w_h = heads(r), heads(k), heads(v), heads(a), heads(decay)
    vec_a = -kk
    vec_b = kk * a_h

    def step(S, inp):
        r_t, w_t, k_t, v_t, a_t, b_t = inp
        sa = jnp.einsum('bhij,bhj->bhi', S, a_t)
        S = S * w_t[:, :, None, :] + sa[..., None] * b_t[:, :, None, :] + v_t[..., None] * k_t[:, :, None, :]
        return S, jnp.einsum('bhij,bhj->bhi', S, r_t)

    seq = tuple(jnp.moveaxis(z, 1, 0) for z in (r_h, w_h, k_h, v_h, vec_a, vec_b))
    S_fin, ys = lax.scan(step, wkv_prev.astype(f32), seq)
    y = jnp.moveaxis(ys, 0, 1)
    mean = jnp.mean(y, axis=-1, keepdims=True)
    var = jnp.mean(jnp.square(y - mean), axis=-1, keepdims=True)
    y = ((y - mean) * lax.rsqrt(var + GN_EPS)).reshape(B, T, RW_WIDTH) * lnx_w + lnx_b
    bonus = jnp.sum(r_h * k_h * r_k.astype(f32).reshape(RW_HEADS, RW_HEAD), axis=-1, keepdims=True) * v_h
    out = (y + bonus.reshape(B, T, RW_WIDTH)) * g
    return out.astype(cols.dtype), S_fin.astype(wkv_prev.dtype), cols[:, -1]


def index_select(qi, wi, ki_all, qpos, topk):
    f32 = jnp.float32
    dots = jnp.einsum('bqhd,bld->bqhl', qi.astype(f32), ki_all.astype(f32))
    score = jnp.einsum('bqh,bqhl->bql', wi.astype(f32), jax.nn.relu(dots))
    kpos = jnp.arange(ki_all.shape[1])
    causal = kpos[None, :] <= qpos[:, None]
    score = jnp.where(causal[None], score, -jnp.inf)
    _, idx = lax.top_k(score, topk)
    valid = idx <= qpos[None, :, None]
    return idx, valid


def sparse_attend(q, kg, vg, valid):
    f32 = jnp.float32
    B, Q = q.shape[:2]
    qg = q.reshape(B, Q, KV_HEADS, ATT_HEADS // KV_HEADS, HEAD_DIM).astype(f32)
    s = jnp.einsum('bqgrd,bqkgd->bqgrk', qg, kg.astype(f32)) * (HEAD_DIM ** -0.5)
    s = jnp.where(valid[:, :, None, None, :], s, -jnp.inf)
    p = jax.nn.softmax(s, axis=-1)
    o = jnp.einsum('bqgrk,bqkgd->bqgrd', p, vg.astype(f32))
    return o.reshape(B, Q, ATT_WIDTH).astype(q.dtype)


def dsa_prompt(q, k, v, qi, ki, wi):
    B, S = q.shape[:2]
    topk = min(TOPK_MAX, S // 4)
    nb = S // Q_BLOCK

    def blk(z):
        return jnp.moveaxis(z.reshape((B, nb, Q_BLOCK) + z.shape[2:]), 1, 0)

    def one(args):
        qb, qib, wb, t0 = args
        qpos = t0 + jnp.arange(Q_BLOCK)
        idx, valid = index_select(qib, wb, ki, qpos, topk)
        kg = jax.vmap(lambda f, i: f[i])(k, idx)
        vg = jax.vmap(lambda f, i: f[i])(v, idx)
        return sparse_attend(qb, kg, vg, valid)

    out = lax.map(one, (blk(q), blk(qi), blk(wi), jnp.arange(nb) * Q_BLOCK))
    return jnp.moveaxis(out, 0, 1).reshape(B, S, ATT_WIDTH)


def dsa_sample(q, k, v, qi, ki, wi, pos, cache_k, cache_v, cache_kidx, page_table):
    B, T = q.shape[:2]
    page = cache_k.shape[1]
    past_len = page_table.shape[1] * page
    topk = min(TOPK_MAX, (past_len + T) // 4)
    ki_past = cache_kidx[page_table].reshape(B, past_len, IDX_DIM).astype(ki.dtype)
    ki_all = jnp.concatenate([ki_past, ki], axis=1)
    idx, valid = index_select(qi, wi, ki_all, pos, topk)
    in_past = idx < past_len
    pidx = jnp.minimum(idx, past_len - 1)
    phys = jax.vmap(lambda pt, i: pt[i])(page_table, pidx // page) * page + pidx % page
    nidx = jnp.clip(idx - past_len, 0, T - 1)

    def gather(pool, new):
        g_past = pool.reshape(-1, KV_HEADS, HEAD_DIM)[phys].astype(new.dtype)
        g_new = jax.vmap(lambda n, i: n[i])(new, nidx)
        return jnp.where(in_past[..., None, None], g_past, g_new)

    return sparse_attend(q, gather(cache_k, k), gather(cache_v, v), valid)


def conv_glu(n, conv_prev, w_up, conv_w, conv_b, w_down):
    T = n.shape[1]
    gate, up = _split(n @ w_up, (D_FF, D_FF))
    ext = jnp.concatenate([conv_prev.astype(gate.dtype), gate], axis=1)
    c = conv_b + ext[:, 0:T] * conv_w[0]
    for j in range(1, CONV_W):
        c = c + ext[:, j:j + T] * conv_w[j]
    return (jax.nn.silu(c) * up) @ w_down, ext[:, T:]


def decoder_layer(x, pos, shift_prev, wkv_prev, conv_prev, attend, lw):
    (norm1, w_in, rw_mu, rw_w0, rw_w2, rw_a0, rw_a2, rw_g2, rw_k_k, rw_k_a, rw_r_k,
     rw_lnx_w, rw_lnx_b, p_rw, p_att, w_o, norm2, w_up, conv_w, conv_b, w_down) = lw
    B, T, _ = x.shape
    n = rmsnorm(x, norm1)
    c_rw, c_att, c_gate = _split(n @ w_in, (RW_COLS, ATT_COLS, GATE_COLS))
    o_rw, wkv_new, shift_new = rwkv7_mix(c_rw, shift_prev, wkv_prev, rw_mu, rw_w0, rw_w2, rw_a0, rw_a2,
                                         rw_g2, rw_k_k, rw_k_a, rw_r_k, rw_lnx_w, rw_lnx_b)
    q, k, v, qi, ki, wi = _split(c_att, (ATT_WIDTH, KV_WIDTH, KV_WIDTH, IDX_HEADS * IDX_DIM, IDX_DIM, IDX_HEADS))
    q = rope(q.reshape(B, T, ATT_HEADS, HEAD_DIM), pos)
    k = rope(k.reshape(B, T, KV_HEADS, HEAD_DIM), pos)
    v = v.reshape(B, T, KV_HEADS, HEAD_DIM)
    qi = rope(qi.reshape(B, T, IDX_HEADS, IDX_DIM), pos)
    ki = rope(ki[:, :, None, :], pos)[:, :, 0, :]
    o_att = attend(q, k, v, qi, ki, wi * IDX_W_SCALE)
    g_rw, g_att = _split(c_gate, (D_MODEL, D_MODEL))
    merged = jax.nn.sigmoid(g_rw) * (o_rw @ p_rw) + jax.nn.sigmoid(g_att) * (o_att @ p_att)
    h = x + merged @ w_o
    f, conv_new = conv_glu(rmsnorm(h, norm2), conv_prev, w_up, conv_w, conv_b, w_down)
    return h + f, (k, v, ki), wkv_new, shift_new, conv_new


def setup_inputs(seed: int = 0) -> dict:
    key = jax.random.key(seed)
    keys = jax.random.split(key, 64)
    counter = iter(range(64))
    f32 = jnp.float32

    def nk():
        return keys[next(counter)]

    def nrm(shape, scale):
        return scale * jax.random.normal(nk(), shape, f32)

    def unif(shape, lo, hi):
        return jax.random.uniform(nk(), shape, f32, lo, hi)

    n_pages = PAST_LEN // PAGE_SIZE
    n_used = DEC_BATCH * n_pages
    n_pool = n_used + max(1, n_used // 4)
    L = DEPTH
    inputs = {}
    inputs["x_prompt"] = nrm((BATCH, SEQ, D_MODEL), 1.0)
    inputs["x_sample"] = nrm((DEC_BATCH, DEC_SEQ, D_MODEL), 1.0)
    inputs["cache_k"] = nrm((L, n_pool, PAGE_SIZE, KV_HEADS, HEAD_DIM), 1.0)
    inputs["cache_v"] = nrm((L, n_pool, PAGE_SIZE, KV_HEADS, HEAD_DIM), 1.0)
    inputs["cache_kidx"] = nrm((L, n_pool, PAGE_SIZE, IDX_DIM), 1.0)
    inputs["state_wkv"] = nrm((L, DEC_BATCH, RW_HEADS, RW_HEAD, RW_HEAD), 0.5)
    inputs["state_shift"] = nrm((L, DEC_BATCH, RW_COLS), 1.0)
    inputs["state_conv"] = nrm((L, DEC_BATCH, CONV_W - 1, D_FF), 1.0)
    perm = jax.random.permutation(nk(), n_pool)
    inputs["page_table"] = perm[:n_used].reshape(DEC_BATCH, n_pages).astype(jnp.int32)
    inputs["norm1"] = 1.0 + nrm((L, D_MODEL), 0.05)
    inputs["w_in"] = nrm((L, D_MODEL, IN_COLS), D_MODEL ** -0.5)
    inputs["rw_mu"] = unif((L, RW_COLS), 0.0, 1.0)
    inputs["rw_w0"] = unif((L, RW_WIDTH), -3.0, 1.0)
    inputs["rw_w2"] = nrm((L, D_DECAY_LORA, RW_WIDTH), 0.1 * D_DECAY_LORA ** -0.5)
    inputs["rw_a0"] = nrm((L, RW_WIDTH), 0.1)
    inputs["rw_a2"] = nrm((L, D_AAA_LORA, RW_WIDTH), 0.5 * D_AAA_LORA ** -0.5)
    inputs["rw_g2"] = nrm((L, D_GATE_LORA, RW_WIDTH), D_GATE_LORA ** -0.5)
    inputs["rw_k_k"] = 0.85 + nrm((L, RW_WIDTH), 0.05)
    inputs["rw_k_a"] = 1.0 + nrm((L, RW_WIDTH), 0.05)
    inputs["rw_r_k"] = nrm((L, RW_WIDTH), 0.1)
    inputs["rw_lnx_w"] = 1.0 + nrm((L, RW_WIDTH), 0.05)
    inputs["rw_lnx_b"] = nrm((L, RW_WIDTH), 0.01)
    inputs["p_rw"] = nrm((L, RW_WIDTH, D_MODEL), RW_WIDTH ** -0.5)
    inputs["p_att"] = nrm((L, ATT_WIDTH, D_MODEL), ATT_WIDTH ** -0.5)
    inputs["w_o"] = nrm((L, D_MODEL, D_MODEL), D_MODEL ** -0.5)
    inputs["norm2"] = 1.0 + nrm((L, D_MODEL), 0.05)
    inputs["w_up"] = nrm((L, D_MODEL, 2 * D_FF), D_MODEL ** -0.5)
    inputs["conv_w"] = nrm((L, CONV_W, D_FF), CONV_W ** -0.5)
    inputs["conv_b"] = nrm((L, D_FF), 0.01)
    inputs["w_down"] = nrm((L, D_FF, D_MODEL), D_FF ** -0.5)
    inputs["norm_f"] = 1.0 + nrm((D_MODEL,), 0.05)
    return inputs


def reference(x_prompt, x_sample, cache_k, cache_v, cache_kidx, state_wkv, state_shift, state_conv,
              page_table, norm1, w_in, rw_mu, rw_w0, rw_w2, rw_a0, rw_a2, rw_g2, rw_k_k, rw_k_a,
              rw_r_k, rw_lnx_w, rw_lnx_b, p_rw, p_att, w_o, norm2, w_up, conv_w, conv_b, w_down, norm_f):
    B, S, _ = x_prompt.shape
    DB, DS, _ = x_sample.shape
    page = cache_k.shape[2]
    past_len = page_table.shape[1] * page
    pos_p = jnp.arange(S)
    pos_s = past_len + jnp.arange(DS)
    dt = x_prompt.dtype
    hp, hs = x_prompt, x_sample
    kp_l, vp_l, kip_l, ks_l, vs_l, kis_l = [], [], [], [], [], []
    wkvp_l, wkvs_l, shp_l, shs_l, cvp_l, cvs_l = [], [], [], [], [], []
    for l in range(DEPTH):
        lw = (norm1[l], w_in[l], rw_mu[l], rw_w0[l], rw_w2[l], rw_a0[l], rw_a2[l], rw_g2[l], rw_k_k[l],
              rw_k_a[l], rw_r_k[l], rw_lnx_w[l], rw_lnx_b[l], p_rw[l], p_att[l], w_o[l], norm2[l],
              w_up[l], conv_w[l], conv_b[l], w_down[l])
        hp, (kp, vp, kip), wkvp, shp, cvp = decoder_layer(
            hp, pos_p, jnp.zeros((B, RW_COLS), dt), jnp.zeros((B, RW_HEADS, RW_HEAD, RW_HEAD), dt),
            jnp.zeros((B, CONV_W - 1, D_FF), dt), dsa_prompt, lw)
        attend_s = functools.partial(dsa_sample, pos=pos_s, cache_k=cache_k[l], cache_v=cache_v[l],
                                     cache_kidx=cache_kidx[l], page_table=page_table)
        hs, (ksn, vsn, kisn), wkvs, shs, cvs = decoder_layer(
            hs, pos_s, state_shift[l], state_wkv[l], state_conv[l], attend_s, lw)
        kp_l.append(kp.reshape(B, S // page, page, KV_HEADS, HEAD_DIM))
        vp_l.append(vp.reshape(B, S // page, page, KV_HEADS, HEAD_DIM))
        kip_l.append(kip.reshape(B, S // page, page, IDX_DIM))
        ks_l.append(ksn)
        vs_l.append(vsn)
        kis_l.append(kisn)
        wkvp_l.append(wkvp)
        wkvs_l.append(wkvs)
        shp_l.append(shp)
        shs_l.append(shs)
        cvp_l.append(cvp)
        cvs_l.append(cvs)
    y_prompt = rmsnorm(hp, norm_f)
    y_sample = rmsnorm(hs, norm_f)
    return (y_prompt, y_sample, jnp.stack(kp_l), jnp.stack(vp_l), jnp.stack(kip_l), jnp.stack(ks_l),
            jnp.stack(vs_l), jnp.stack(kis_l), jnp.stack(wkvp_l), jnp.stack(wkvs_l), jnp.stack(shp_l),
            jnp.stack(shs_l), jnp.stack(cvp_l), jnp.stack(cvs_l))
```

```python
import functools
import math

import numpy as np
import jax
import jax.numpy as jnp
from jax import lax
from jax.experimental import pallas as pl
from jax.experimental.pallas import tpu as pltpu

f32 = jnp.float32
bf16 = jnp.bfloat16
i32 = jnp.int32

RW_HEADS = 16
RW_HEAD = 64
RW_WIDTH = RW_HEADS * RW_HEAD
D_DECAY_LORA = 96
D_AAA_LORA = 96
D_GATE_LORA = 128
GN_EPS = 64e-5
ATT_HEADS = 8
KV_HEADS = 2
HEAD_DIM = 128
ATT_WIDTH = ATT_HEADS * HEAD_DIM
KV_WIDTH = KV_HEADS * HEAD_DIM
IDX_HEADS = 16
IDX_DIM = 64
IDX_W_SCALE = (IDX_HEADS * IDX_DIM) ** -0.5
TOPK_MAX = 256
Q_BLOCK = 128
ROPE_THETA = 10000.0
CONV_W = 3
RMS_EPS = 1e-6

LANES = 128
NEG_BIG = -1e30
INT_MIN = -(2 ** 31)

C_R, C_K, C_V, C_Q, C_GRW, C_GATT, C_QI = 0, 1024, 2048, 3072, 4096, 6144, 8192
C_LORA = 9216
C_KI, C_AK, C_AV, C_WI = 9600, 9728, 9984, 10240
NP_COLS = 10368
LORA_W = 384
RKV_W = 3 * RW_WIDTH

VMEM_LIMIT = 56 * 1024 * 1024


def _cparams(sem):
    return pltpu.CompilerParams(dimension_semantics=sem, vmem_limit_bytes=VMEM_LIMIT)


def _bdot(a, b):
    return jnp.dot(a.astype(bf16), b.astype(bf16), preferred_element_type=f32)


def _bdot_nt(a, b):
    return lax.dot_general(a.astype(bf16), b.astype(bf16), (((1,), (1,)), ((), ())), preferred_element_type=f32)


def _bdot_tn(a, b):
    return lax.dot_general(a.astype(bf16), b.astype(bf16), (((0,), (0,)), ((), ())), preferred_element_type=f32)


def _split3(x):
    x1 = x.astype(bf16)
    r1 = x - x1.astype(f32)
    x2 = r1.astype(bf16)
    x3 = (r1 - x2.astype(f32)).astype(bf16)
    return x1, x2, x3


def _split2(x):
    x1 = x.astype(bf16)
    return x1, (x - x1.astype(f32)).astype(bf16)


def _headsum(x):
    ri = lax.broadcasted_iota(i32, (LANES, LANES), 0) // RW_HEAD
    ci = lax.broadcasted_iota(i32, (LANES, LANES), 1) // RW_HEAD
    bd = (ri == ci).astype(bf16)
    outs = []
    for i in range(x.shape[1] // LANES):
        hi, lo = _split2(x[:, i * LANES:(i + 1) * LANES])
        outs.append(jnp.dot(hi, bd, preferred_element_type=f32) + jnp.dot(lo, bd, preferred_element_type=f32))
    return jnp.concatenate(outs, axis=1)


def _softplus(x):
    return jnp.maximum(x, 0.0) + jnp.log1p(jnp.exp(-jnp.abs(x)))


def _rope(x, cos, sin, half):
    w = x.shape[1]
    reps = w // LANES
    if reps > 1:
        cos = jnp.concatenate([cos] * reps, axis=1)
        sin = jnp.concatenate([sin] * reps, axis=1)
    if 2 * half == LANES and w == LANES:
        partner = pltpu.roll(x, half, axis=1)
    else:
        lane = lax.broadcasted_iota(i32, (1, w), 1)
        first = (lane % (2 * half)) < half
        partner = jnp.where(first, pltpu.roll(x, w - half, axis=1), pltpu.roll(x, half, axis=1))
    return x * cos + partner * sin


def _inproj_body(x_ref, g_ref, w_ref, o_ref, n_ref):
    @pl.when(pl.program_id(1) == 0)
    def _():
        x = x_ref[...]
        ms = jnp.mean(x * x, axis=-1, keepdims=True)
        n_ref[...] = (x * lax.rsqrt(ms + RMS_EPS) * g_ref[...]).astype(bf16)

    o_ref[...] = jnp.dot(n_ref[...], w_ref[...], preferred_element_type=f32)


def _in_proj(x2d, gain, w_p):
    m, d = x2d.shape
    tm = min(512, m)
    tn = NP_COLS // 9
    return pl.pallas_call(
        _inproj_body,
        grid=(m // tm, NP_COLS // tn),
        in_specs=[
            pl.BlockSpec((tm, d), lambda i, j: (i, 0)),
            pl.BlockSpec((1, d), lambda i, j: (0, 0)),
            pl.BlockSpec((d, tn), lambda i, j: (0, j)),
        ],
        out_specs=pl.BlockSpec((tm, tn), lambda i, j: (i, j)),
        out_shape=jax.ShapeDtypeStruct((m, NP_COLS), f32),
        scratch_shapes=[pltpu.VMEM((tm, d), bf16)],
        compiler_params=_cparams(("parallel", "arbitrary")),
        name="in_proj",
    )(x2d, gain, w_p)


RW_PARAM_NAMES = ("mu_rkv", "mu_lora", "w0", "w2", "a0", "a2", "g2", "k_k", "k_a", "r_k", "lnx_w", "lnx_b")


def _rw_prep(c_rkv, c_lora, p_rkv, p_lora, P):
    m = c_rkv + (p_rkv - c_rkv) * P["mu_rkv"]
    ml = c_lora + (p_lora - c_lora) * P["mu_lora"]
    r, k, v = m[:, :RW_WIDTH], m[:, RW_WIDTH:2 * RW_WIDTH], m[:, 2 * RW_WIDTH:]
    wl, al, gl = ml[:, :LANES], ml[:, LANES:2 * LANES], ml[:, 2 * LANES:]
    w_log = -_softplus(-(P["w0"] + _bdot(jnp.tanh(wl), P["w2"]))) - 0.5
    logw = -jnp.exp(w_log)
    asig = jax.nn.sigmoid(P["a0"] + _bdot(al, P["a2"]))
    g = _bdot(jax.nn.sigmoid(gl), P["g2"])
    kk = k * P["k_k"]
    kkn = kk * lax.rsqrt(jnp.maximum(_headsum(kk * kk), 1e-24))
    k2 = k * (1.0 + (asig - 1.0) * P["k_a"])
    return r, logw, k2, v, -kkn, kkn * asig, g


def _rw_post(y, r, k2, v, g, P):
    inv_n = 1.0 / RW_HEAD
    mean = _headsum(y) * inv_n
    d = y - mean
    var = _headsum(d * d) * inv_n
    yn = d * lax.rsqrt(var + GN_EPS) * P["lnx_w"] + P["lnx_b"]
    bonus = _headsum(r * k2 * P["r_k"]) * v
    return (yn + bonus) * g


def _pair_chunk(r, cum, logw, k, v, a, b, S, keep, eye2, colh, mA, bdmask, nsteps):
    C = r.shape[0]
    cumC = cum[C - 1:C, :]
    iW = jnp.exp(-cum)
    eW = jnp.exp(cumC - cum)
    at = a * jnp.exp(cum - logw)
    rt = r * jnp.exp(cum)
    bt = b * iW
    kt = k * iW
    X = jnp.concatenate([at, rt], axis=0)
    scA = _bdot_nt(jnp.where(mA, X, 0.0), jnp.concatenate([bt, kt], axis=0))
    scB = _bdot_nt(jnp.where(mA, 0.0, X), jnp.concatenate([kt, bt], axis=0))
    scA = jnp.where(keep, scA, 0.0)
    scB = jnp.where(keep, scB, 0.0)
    XS = _bdot_nt(X, S)
    Lp = jnp.concatenate([jnp.where(colh, scA[:C], 0.0), jnp.where(colh, 0.0, scB[:C])], axis=0)
    T = eye2 + Lp
    if nsteps > 0:
        Pw = _bdot(Lp, Lp)
        for i in range(nsteps):
            if i < nsteps - 1:
                Z = _bdot(Pw, jnp.concatenate([T, Pw], axis=1))
                T = T + Z[:, :2 * C]
                Pw = Z[:, 2 * C:]
            else:
                T = T + _bdot(Pw, T)
    TAB = jnp.where(colh, T[:C], T[C:])
    LK = jnp.where(colh, scB[:C], scA[:C])
    vA = jnp.where(mA, v, 0.0)
    vB = jnp.where(mA, 0.0, v)
    G = XS[:C] + _bdot(LK, jnp.concatenate([vB, vA], axis=0))
    U = _bdot(TAB, jnp.concatenate([jnp.where(mA, G, 0.0), jnp.where(mA, 0.0, G)], axis=0))
    uA = jnp.where(mA, U, 0.0)
    uB = jnp.where(mA, 0.0, U)
    Y = XS[C:] + _bdot(jnp.concatenate([scA[C:], scB[C:]], axis=1), jnp.concatenate([uA, vA, vB, uB], axis=0))
    dS = _bdot_tn(jnp.concatenate([U, v], axis=0), jnp.concatenate([b * eW, k * eW], axis=0))
    S_new = S * jnp.exp(cumC) + jnp.where(bdmask, dS, 0.0)
    return Y, S_new


def _chunk_masks(C):
    row = lax.broadcasted_iota(i32, (2 * C, 2 * C), 0)
    col = lax.broadcasted_iota(i32, (2 * C, 2 * C), 1)
    t = jnp.where(row >= C, row - C, row)
    s = jnp.where(col >= C, col - C, col)
    keep = (s < t) | ((row >= C) & (s == t))
    eye2 = (row == col).astype(f32)
    colh = lax.broadcasted_iota(i32, (C, 2 * C), 1) < C
    mA = lax.broadcasted_iota(i32, (1, LANES), 1) < RW_HEAD
    r2 = lax.broadcasted_iota(i32, (LANES, LANES), 0) // RW_HEAD
    c2 = lax.broadcasted_iota(i32, (LANES, LANES), 1) // RW_HEAD
    return keep, eye2, colh, mA, r2 == c2


def _rwkv_prompt_body(chunk, rkv_ref, lora_ref, sp_rkv_ref, sp_lora_ref, s0_ref, *rest):
    np_ = len(RW_PARAM_NAMES)
    P = {n: rest[i][...] for i, n in enumerate(RW_PARAM_NAMES)}
    o_ref, sout_ref = rest[np_], rest[np_ + 1]
    (S_ref, car_rkv, car_lora, r_s, cum_s, lw_s, k_s, v_s, a_s, b_s, y_s) = rest[np_ + 2:]
    t = pl.program_id(1)
    tt = rkv_ref.shape[0]
    C = chunk

    @pl.when(t == 0)
    def _():
        S_ref[...] = s0_ref[0]
        car_rkv[...] = sp_rkv_ref[0]
        car_lora[...] = sp_lora_ref[0]

    c_rkv = rkv_ref[...]
    c_lora = lora_ref[...]
    first = lax.broadcasted_iota(i32, (tt, 1), 0) == 0
    p_rkv = jnp.where(first, car_rkv[...], pltpu.roll(c_rkv, 1, axis=0))
    p_lora = jnp.where(first, car_lora[...], pltpu.roll(c_lora, 1, axis=0))
    car_rkv[...] = c_rkv[tt - 1:tt, :]
    car_lora[...] = c_lora[tt - 1:tt, :]
    r, logw, k2, v, a, b, g = _rw_prep(c_rkv, c_lora, p_rkv, p_lora, P)

    ri = lax.broadcasted_iota(i32, (tt, tt), 0)
    ci = lax.broadcasted_iota(i32, (tt, tt), 1)
    tril = ((ri // C == ci // C) & (ci <= ri)).astype(bf16)
    cum = sum(jnp.dot(tril, piece, preferred_element_type=f32) for piece in _split3(logw))

    r_s[...] = r
    cum_s[...] = cum
    lw_s[...] = logw
    k_s[...] = k2
    v_s[...] = v
    a_s[...] = a
    b_s[...] = b
    masks = _chunk_masks(C)
    nsteps = max(int(math.ceil(math.log2(C))) - 1, 0)

    def chunk_body(ci_, carry):
        r0 = pl.multiple_of(ci_ * C, C)
        for p in range(RW_WIDTH // LANES):
            sl = (pl.ds(r0, C), slice(p * LANES, (p + 1) * LANES))
            Y, S_new = _pair_chunk(r_s[sl], cum_s[sl], lw_s[sl], k_s[sl], v_s[sl], a_s[sl], b_s[sl], S_ref[p],
                                   *masks, nsteps)
            y_s[sl] = Y
            S_ref[p] = S_new
        return carry

    lax.fori_loop(0, tt // C, chunk_body, 0)
    o_ref[...] = _rw_post(y_s[...], r, k2, v, g, P)

    @pl.when(t == pl.num_programs(1) - 1)
    def _():
        sout_ref[0] = S_ref[...]


def _const_spec(shape):
    nd = len(shape)
    return pl.BlockSpec(shape, lambda *_: (0,) * nd)


def _rwkv_prompt(c, sp_rkv, sp_lora, s0, params, nb, seq):
    tt = min(256, seq)
    chunk = min(64, tt)
    nt = seq // tt
    npairs = RW_WIDTH // LANES
    in_specs = [
        pl.BlockSpec((tt, RKV_W), lambda b, t: (b * nt + t, 0)),
        pl.BlockSpec((tt, LORA_W), lambda b, t: (b * nt + t, C_LORA // LORA_W)),
        pl.BlockSpec((1, 1, RKV_W), lambda b, t: (b, 0, 0)),
        pl.BlockSpec((1, 1, LORA_W), lambda b, t: (b, 0, 0)),
        pl.BlockSpec((1, npairs, LANES, LANES), lambda b, t: (b, 0, 0, 0)),
    ] + [_const_spec(params[n].shape) for n in RW_PARAM_NAMES]
    out_specs = [
        pl.BlockSpec((tt, RW_WIDTH), lambda b, t: (b * nt + t, 0)),
        pl.BlockSpec((1, npairs, LANES, LANES), lambda b, t: (b, 0, 0, 0)),
    ]
    scratch = [pltpu.VMEM((npairs, LANES, LANES), f32), pltpu.VMEM((1, RKV_W), f32), pltpu.VMEM((1, LORA_W), f32)]
    scratch += [pltpu.VMEM((tt, RW_WIDTH), f32) for _ in range(8)]
    return pl.pallas_call(
        functools.partial(_rwkv_prompt_body, chunk),
        grid=(nb, nt),
        in_specs=in_specs,
        out_specs=out_specs,
        out_shape=[jax.ShapeDtypeStruct((nb * seq, RW_WIDTH), f32),
                   jax.ShapeDtypeStruct((nb, npairs, LANES, LANES), f32)],
        scratch_shapes=scratch,
        compiler_params=_cparams(("parallel", "arbitrary")),
        name="rwkv_prompt",
    )(c, c, sp_rkv, sp_lora, s0, *[params[n] for n in RW_PARAM_NAMES])


def _rwkv_prep_body(rkv_ref, lora_ref, prkv_ref, plora_ref, *rest):
    np_ = len(RW_PARAM_NAMES)
    P = {n: rest[i][...] for i, n in enumerate(RW_PARAM_NAMES)}
    outs = rest[np_:]
    vals = _rw_prep(rkv_ref[...], lora_ref[...], prkv_ref[...], plora_ref[...], P)
    for o, v in zip(outs, vals):
        o[...] = v


def _rwkv_prep(c, prev_rkv, prev_lora, params):
    m = c.shape[0]
    tm = min(512, m)
    in_specs = [
        pl.BlockSpec((tm, RKV_W), lambda i: (i, 0)),
        pl.BlockSpec((tm, LORA_W), lambda i: (i, C_LORA // LORA_W)),
        pl.BlockSpec((tm, RKV_W), lambda i: (i, 0)),
        pl.BlockSpec((tm, LORA_W), lambda i: (i, 0)),
    ] + [_const_spec(params[n].shape) for n in RW_PARAM_NAMES]
    return pl.pallas_call(
        _rwkv_prep_body,
        grid=(m // tm,),
        in_specs=in_specs,
        out_specs=[pl.BlockSpec((tm, RW_WIDTH), lambda i: (i, 0)) for _ in range(7)],
        out_shape=[jax.ShapeDtypeStruct((m, RW_WIDTH), f32) for _ in range(7)],
        compiler_params=_cparams(("parallel",)),
        name="rwkv_prep",
    )(c, c, prev_rkv, prev_lora, *[params[n] for n in RW_PARAM_NAMES])


def _rwkv_seq_body(r_ref, lw_ref, k_ref, v_ref, a_ref, b_ref, s_ref, y_ref, sout_ref):
    steps = r_ref.shape[1]
    eye = (lax.broadcasted_iota(i32, (LANES, LANES), 0) == lax.broadcasted_iota(i32, (LANES, LANES), 1)).astype(f32)
    bd = (lax.broadcasted_iota(i32, (LANES, LANES), 0) // RW_HEAD) == (
        lax.broadcasted_iota(i32, (LANES, LANES), 1) // RW_HEAD)
    for p in range(RW_WIDTH // LANES):
        S = s_ref[0, p]
        for t in range(steps):
            sl = (0, slice(t, t + 1), slice(p * LANES, (p + 1) * LANES))
            r_t, k_t, v_t, a_t, b_t = r_ref[sl], k_ref[sl], v_ref[sl], a_ref[sl], b_ref[sl]
            w_t = jnp.exp(lw_ref[sl])
            sa = jnp.sum(S * a_t, axis=1, keepdims=True)
            v_col = jnp.sum(eye * v_t, axis=1, keepdims=True)
            S = S * w_t + jnp.where(bd, sa * b_t + v_col * k_t, 0.0)
            y_col = jnp.sum(S * r_t, axis=1, keepdims=True)
            y_ref[sl] = jnp.sum(eye * y_col, axis=0, keepdims=True)
        sout_ref[0, p] = S


def _rwkv_seq(ops, s0, nreq, steps):
    npairs = RW_WIDTH // LANES
    ops3 = [o.reshape(nreq, steps, RW_WIDTH) for o in ops]
    return pl.pallas_call(
        _rwkv_seq_body,
        grid=(nreq,),
        in_specs=[pl.BlockSpec((1, steps, RW_WIDTH), lambda b: (b, 0, 0)) for _ in range(6)]
        + [pl.BlockSpec((1, npairs, LANES, LANES), lambda b: (b, 0, 0, 0))],
        out_specs=[pl.BlockSpec((1, steps, RW_WIDTH), lambda b: (b, 0, 0)),
                   pl.BlockSpec((1, npairs, LANES, LANES), lambda b: (b, 0, 0, 0))],
        out_shape=[jax.ShapeDtypeStruct((nreq, steps, RW_WIDTH), f32),
                   jax.ShapeDtypeStruct((nreq, npairs, LANES, LANES), f32)],
        compiler_params=_cparams(("parallel",)),
        name="rwkv_seq",
    )(*ops3, s0)


def _rwkv_post_body(y_ref, r_ref, k_ref, v_ref, g_ref, *rest):
    np_ = len(RW_PARAM_NAMES)
    P = {n: rest[i][...] for i, n in enumerate(RW_PARAM_NAMES)}
    rest[np_][...] = _rw_post(y_ref[...], r_ref[...], k_ref[...], v_ref[...], g_ref[...], P)


def _rwkv_post(y, r, k2, v, g, params):
    m = y.shape[0]
    tm = min(512, m)
    return pl.pallas_call(
        _rwkv_post_body,
        grid=(m // tm,),
        in_specs=[pl.BlockSpec((tm, RW_WIDTH), lambda i: (i, 0)) for _ in range(5)]
        + [_const_spec(params[n].shape) for n in RW_PARAM_NAMES],
        out_specs=pl.BlockSpec((tm, RW_WIDTH), lambda i: (i, 0)),
        out_shape=jax.ShapeDtypeStruct((m, RW_WIDTH), f32),
        compiler_params=_cparams(("parallel",)),
        name="rwkv_post",
    )(y, r, k2, v, g, *[params[n] for n in RW_PARAM_NAMES])


def _sort_key(score):
    bits = pltpu.bitcast(score, i32)
    key = jnp.where(bits < 0, bits ^ jnp.int32(0x7FFFFFFF), bits)
    return jnp.where(score == 0.0, jnp.int32(0), key)


def _kth_largest(count_ge, shape, k):
    def body(i, t):
        cand = t + lax.shift_left(jnp.int32(1), jnp.int32(31) - i)
        return jnp.where(count_ge(cand) >= k, cand, t)

    return lax.fori_loop(0, 32, body, jnp.full(shape, INT_MIN, i32))


def _dsa_prompt_body(topk, q_ref, qi_ref, wi_ref, k_ref, v_ref, ki_ref, cosk_ref, sink_ref, cosi_ref, sini_ref,
                     o_ref, kout_ref, kiout_ref, kb_ref, vb_ref, kib_ref, sc_ref, bias_ref):
    qb = pl.program_id(1)
    seq = k_ref.shape[0]
    nq = q_ref.shape[0]

    @pl.when(qb == 0)
    def _():
        cos, sin = cosk_ref[...], sink_ref[...]
        kr = jnp.concatenate([_rope(k_ref[:, h * HEAD_DIM:(h + 1) * HEAD_DIM], cos, sin, HEAD_DIM // 2)
                              for h in range(KV_HEADS)], axis=1)
        kout_ref[...] = kr
        kb_ref[...] = kr.astype(bf16)
        vb_ref[...] = v_ref[...].astype(bf16)
        kir = _rope(ki_ref[...], cosi_ref[...], sini_ref[...], IDX_DIM // 2)
        kiout_ref[...] = kir[:, :IDX_DIM]
        kib_ref[...] = (kir + pltpu.roll(kir, IDX_DIM, axis=1)).astype(bf16)

    r0 = pl.multiple_of(qb * nq, nq)
    cosq, sinq = cosk_ref[pl.ds(r0, nq), :], sink_ref[pl.ds(r0, nq), :]
    cosqi, sinqi = cosi_ref[pl.ds(r0, nq), :], sini_ref[pl.ds(r0, nq), :]
    q = _rope(q_ref[...], cosq, sinq, HEAD_DIM // 2).astype(bf16)
    qi = _rope(qi_ref[...], cosqi, sinqi, IDX_DIM // 2)
    wi = wi_ref[...] * IDX_W_SCALE
    kib = kib_ref[...]
    lane = lax.broadcasted_iota(i32, (1, LANES), 1)

    sc_ref[...] = jnp.zeros_like(sc_ref)
    for h in range(IDX_HEADS):
        pair = qi[:, (h // 2) * LANES:(h // 2 + 1) * LANES]
        mine = (lane < IDX_DIM) if h % 2 == 0 else (lane >= IDX_DIM)
        d = _bdot_nt(jnp.where(mine, pair, 0.0), kib)
        sc_ref[...] += jnp.maximum(d, 0.0) * wi[:, h:h + 1]

    qpos = r0 + lax.broadcasted_iota(i32, (nq, 1), 0)
    kpos = lax.broadcasted_iota(i32, (1, seq), 1)
    causal = kpos <= qpos
    key = jnp.where(causal, _sort_key(sc_ref[...]), INT_MIN)

    def count_ge(cand):
        return jnp.sum((key >= cand).astype(f32), axis=1, keepdims=True)

    thr = _kth_largest(count_ge, (nq, 1), float(topk))
    bias_ref[...] = jnp.where(causal & (key >= thr), 0.0, NEG_BIG)

    scale = HEAD_DIM ** -0.5
    for h in range(ATT_HEADS):
        g = h // (ATT_HEADS // KV_HEADS)
        kg = kb_ref[:, g * HEAD_DIM:(g + 1) * HEAD_DIM]
        s = _bdot_nt(q[:, h * HEAD_DIM:(h + 1) * HEAD_DIM], kg) * scale + bias_ref[...]
        m = jnp.max(s, axis=1, keepdims=True)
        p = jnp.exp(s - m)
        l = jnp.sum(p, axis=1, keepdims=True)
        o = jnp.dot(p.astype(bf16), vb_ref[:, g * HEAD_DIM:(g + 1) * HEAD_DIM], preferred_element_type=f32)
        o_ref[:, h * HEAD_DIM:(h + 1) * HEAD_DIM] = o / l


def _dsa_prompt(c, tabs, nb, seq):
    nq = Q_BLOCK
    nblk = seq // nq
    topk = min(TOPK_MAX, seq // 4)
    cosk, sink, cosi, sini = tabs
    row = lambda b, j: b * nblk + j
    in_specs = [
        pl.BlockSpec((nq, ATT_WIDTH), lambda b, j: (row(b, j), C_Q // ATT_WIDTH)),
        pl.BlockSpec((nq, IDX_HEADS * IDX_DIM), lambda b, j: (row(b, j), C_QI // (IDX_HEADS * IDX_DIM))),
        pl.BlockSpec((nq, LANES), lambda b, j: (row(b, j), C_WI // LANES)),
        pl.BlockSpec((seq, KV_WIDTH), lambda b, j: (b, C_AK // KV_WIDTH)),
        pl.BlockSpec((seq, KV_WIDTH), lambda b, j: (b, C_AV // KV_WIDTH)),
        pl.BlockSpec((seq, LANES), lambda b, j: (b, C_KI // LANES)),
    ] + [_const_spec((seq, LANES)) for _ in range(4)]
    out_specs = [
        pl.BlockSpec((nq, ATT_WIDTH), lambda b, j: (row(b, j), 0)),
        pl.BlockSpec((seq, KV_WIDTH), lambda b, j: (b, 0)),
        pl.BlockSpec((seq, IDX_DIM), lambda b, j: (b, 0)),
    ]
    scratch = [pltpu.VMEM((seq, KV_WIDTH), bf16), pltpu.VMEM((seq, KV_WIDTH), bf16), pltpu.VMEM((seq, LANES), bf16),
               pltpu.VMEM((nq, seq), f32), pltpu.VMEM((nq, seq), f32)]
    return pl.pallas_call(
        functools.partial(_dsa_prompt_body, topk),
        grid=(nb, nblk),
        in_specs=in_specs,
        out_specs=out_specs,
        out_shape=[jax.ShapeDtypeStruct((nb * seq, ATT_WIDTH), f32),
                   jax.ShapeDtypeStruct((nb * seq, KV_WIDTH), f32),
                   jax.ShapeDtypeStruct((nb * seq, IDX_DIM), f32)],
        scratch_shapes=scratch,
        compiler_params=_cparams(("parallel", "arbitrary")),
        name="dsa_prompt",
    )(c, c, c, c, c, c, cosk, sink, cosi, sini)


ROWS = 8


def _dsa_sample_body(topk, n_new, pt_ref, q_ref, qi_ref, wrow_ref, kn_ref, vn_ref, kin_ref, cosk_ref, sink_ref,
                     cosi_ref, sini_ref, ck_hbm, cv_hbm, cki_hbm, o_ref, kout_ref, kiout_ref,
                     kbuf, vbuf, kibuf, sems):
    b = pl.program_id(0)
    nreq = pl.num_programs(0)
    n_pages = pt_ref.shape[1]
    page = ck_hbm.shape[1]

    def copies(req, slot):
        out = []
        for p in range(n_pages):
            pg = pt_ref[req, p]
            rows = pl.ds(p * page, page)
            out.append(pltpu.make_async_copy(ck_hbm.at[pg], kbuf.at[slot, rows], sems.at[0, slot]))
            out.append(pltpu.make_async_copy(cv_hbm.at[pg], vbuf.at[slot, rows], sems.at[1, slot]))
            out.append(pltpu.make_async_copy(cki_hbm.at[pg], kibuf.at[slot, rows], sems.at[2, slot]))
        return out

    slot = b % 2

    @pl.when(b == 0)
    def _():
        for cp in copies(0, 0):
            cp.start()

    @pl.when(b + 1 < nreq)
    def _():
        for cp in copies(b + 1, 1 - slot):
            cp.start()

    cosk, sink, cosi, sini = cosk_ref[...], sink_ref[...], cosi_ref[...], sini_ref[...]
    q = _rope(q_ref[0], cosk, sink, HEAD_DIM // 2)
    qi = _rope(qi_ref[0], cosi, sini, IDX_DIM // 2)
    kn = jnp.concatenate([_rope(kn_ref[0][:, h * HEAD_DIM:(h + 1) * HEAD_DIM], cosk, sink, HEAD_DIM // 2)
                          for h in range(KV_HEADS)], axis=1)
    kin = _rope(kin_ref[0], cosi, sini, IDX_DIM // 2)
    vn = vn_ref[0]
    kout_ref[0] = kn
    kiout_ref[0] = kin[:, :IDX_DIM]

    qis = jnp.concatenate([qi[:, h * IDX_DIM:(h + 1) * IDX_DIM] for h in range(IDX_HEADS)], axis=0)
    wrow = wrow_ref[0] * IDX_W_SCALE
    li = lax.broadcasted_iota(i32, (LANES, LANES), 0)
    lo = lax.broadcasted_iota(i32, (LANES, LANES), 1)
    sel_t = (li % ROWS == lo % ROWS).astype(bf16)

    def idx_scores(ki):
        d = _bdot_nt(ki, qis)
        hi, lw = _split2(jnp.maximum(d, 0.0) * wrow)
        return jnp.dot(hi, sel_t, preferred_element_type=f32) + jnp.dot(lw, sel_t, preferred_element_type=f32)

    for cp in copies(b, slot):
        cp.wait()

    sc_p = idx_scores(kibuf[slot])
    sc_n = idx_scores(kin[:, :IDX_DIM])
    tq = lax.broadcasted_iota(i32, (ROWS, LANES), 1) % ROWS
    jn = lax.broadcasted_iota(i32, (ROWS, LANES), 0)
    valid_n = (jn <= tq) & (jn < n_new)
    key_p = _sort_key(sc_p)
    key_n = jnp.where(valid_n, _sort_key(sc_n), INT_MIN)

    def count_ge(cand):
        return (jnp.sum((key_p >= cand).astype(f32), axis=0, keepdims=True)
                + jnp.sum((key_n >= cand).astype(f32), axis=0, keepdims=True))

    thr = _kth_largest(count_ge, (1, LANES), float(topk))
    bias_p = jnp.where(key_p >= thr, 0.0, NEG_BIG)
    bias_n = jnp.where(valid_n & (key_n >= thr), 0.0, NEG_BIG)

    eye = (li == lo).astype(f32)
    scale = HEAD_DIM ** -0.5
    hpg = ATT_HEADS // KV_HEADS
    zrows = jnp.zeros((LANES - hpg * ROWS, HEAD_DIM), f32)
    for g in range(KV_HEADS):
        qg = jnp.concatenate([q[:, (g * hpg + r) * HEAD_DIM:(g * hpg + r + 1) * HEAD_DIM] for r in range(hpg)]
                             + [zrows], axis=0)
        gs = slice(g * HEAD_DIM, (g + 1) * HEAD_DIM)
        kp = kbuf[slot, :, gs]
        vp = vbuf[slot, :, gs]
        s_p = _bdot_nt(kp, qg) * scale + bias_p
        s_n = _bdot_nt(kn[:, gs], qg) * scale + bias_n
        m = jnp.maximum(jnp.max(s_p, axis=0, keepdims=True), jnp.max(s_n, axis=0, keepdims=True))
        p_p = jnp.exp(s_p - m)
        p_n = jnp.exp(s_n - m)
        l = jnp.sum(p_p, axis=0, keepdims=True) + jnp.sum(p_n, axis=0, keepdims=True)
        o = _bdot_tn(p_p, vp) + _bdot_tn(p_n, vn[:, gs])
        o = o / jnp.sum(eye * l, axis=1, keepdims=True)
        for r in range(hpg):
            h = g * hpg + r
            o_ref[0, :, h * HEAD_DIM:(h + 1) * HEAD_DIM] = o[r * ROWS:(r + 1) * ROWS]


def _dsa_sample(csel, wrow, tabs, page_table, cache_k, cache_v, cache_kidx, n_new):
    q8, qi8, kn8, vn8, kin8 = csel
    nreq = q8.shape[0]
    n_pages = page_table.shape[1]
    page = cache_k.shape[1]
    past = n_pages * page
    topk = min(TOPK_MAX, (past + n_new) // 4)
    cosk, sink, cosi, sini = tabs
    req3 = lambda w: pl.BlockSpec((1, ROWS, w), lambda b, pt: (b, 0, 0))
    tab = pl.BlockSpec((ROWS, LANES), lambda b, pt: (0, 0))
    grid_spec = pltpu.PrefetchScalarGridSpec(
        num_scalar_prefetch=1,
        grid=(nreq,),
        in_specs=[req3(ATT_WIDTH), req3(IDX_HEADS * IDX_DIM), pl.BlockSpec((1, 1, LANES), lambda b, pt: (b, 0, 0)),
                  req3(KV_WIDTH), req3(KV_WIDTH), req3(LANES), tab, tab, tab, tab,
                  pl.BlockSpec(memory_space=pl.ANY), pl.BlockSpec(memory_space=pl.ANY),
                  pl.BlockSpec(memory_space=pl.ANY)],
        out_specs=[req3(ATT_WIDTH), req3(KV_WIDTH), req3(IDX_DIM)],
        scratch_shapes=[pltpu.VMEM((2, past, KV_WIDTH), f32), pltpu.VMEM((2, past, KV_WIDTH), f32),
                        pltpu.VMEM((2, past, IDX_DIM), f32), pltpu.SemaphoreType.DMA((3, 2))],
    )
    return pl.pallas_call(
        functools.partial(_dsa_sample_body, topk, n_new),
        grid_spec=grid_spec,
        out_shape=[jax.ShapeDtypeStruct((nreq, ROWS, ATT_WIDTH), f32),
                   jax.ShapeDtypeStruct((nreq, ROWS, KV_WIDTH), f32),
                   jax.ShapeDtypeStruct((nreq, ROWS, IDX_DIM), f32)],
        compiler_params=_cparams(("arbitrary",)),
        name="dsa_sample",
    )(page_table, q8, qi8, wrow, kn8, vn8, kin8, cosk, sink, cosi, sini, cache_k, cache_v, cache_kidx)


def _merge_body(orw_ref, oatt_ref, grw_ref, gatt_ref, prw_ref, patt_ref, o_ref):
    a = jnp.dot(orw_ref[...].astype(bf16), prw_ref[...], preferred_element_type=f32)
    b = jnp.dot(oatt_ref[...].astype(bf16), patt_ref[...], preferred_element_type=f32)
    o_ref[...] = (jax.nn.sigmoid(grw_ref[...]) * a + jax.nn.sigmoid(gatt_ref[...]) * b).astype(bf16)


def _merge(o_rw, o_att, c, p_rw, p_att):
    m = o_rw.shape[0]
    d = p_rw.shape[1]
    tm = min(512, m)
    tn = 1024
    nj = d // tn
    return pl.pallas_call(
        _merge_body,
        grid=(m // tm, nj),
        in_specs=[
            pl.BlockSpec((tm, RW_WIDTH), lambda i, j: (i, 0)),
            pl.BlockSpec((tm, ATT_WIDTH), lambda i, j: (i, 0)),
            pl.BlockSpec((tm, tn), lambda i, j: (i, C_GRW // tn + j)),
            pl.BlockSpec((tm, tn), lambda i, j: (i, C_GATT // tn + j)),
            pl.BlockSpec((RW_WIDTH, tn), lambda i, j: (0, j)),
            pl.BlockSpec((ATT_WIDTH, tn), lambda i, j: (0, j)),
        ],
        out_specs=pl.BlockSpec((tm, tn), lambda i, j: (i, j)),
        out_shape=jax.ShapeDtypeStruct((m, d), bf16),
        compiler_params=_cparams(("parallel", "arbitrary")),
        name="merge",
    )(o_rw, o_att, c, c, p_rw, p_att)


def _outproj_body(mg_ref, x_ref, w_ref, o_ref):
    o_ref[...] = x_ref[...] + jnp.dot(mg_ref[...], w_ref[...], preferred_element_type=f32)


def _outproj(merged, x2d, w_o):
    m, d = x2d.shape
    tm = min(512, m)
    tn = 1024
    return pl.pallas_call(
        _outproj_body,
        grid=(m // tm, d // tn),
        in_specs=[
            pl.BlockSpec((tm, d), lambda i, j: (i, 0)),
            pl.BlockSpec((tm, tn), lambda i, j: (i, j)),
            pl.BlockSpec((d, tn), lambda i, j: (0, j)),
        ],
        out_specs=pl.BlockSpec((tm, tn), lambda i, j: (i, j)),
        out_shape=jax.ShapeDtypeStruct((m, d), f32),
        compiler_params=_cparams(("parallel", "arbitrary")),
        name="out_proj",
    )(merged, x2d, w_o)


def _convglu_body(shift, tiles_per_seq, final_norm, h_ref, g2_ref, gf_ref, cprev_ref, wg_ref, wu_ref, cw_ref,
                  cb_ref, wd_ref, o_ref, tail_ref, n_ref, acc_ref, ext_ref, carry_ref):
    i = pl.program_id(0)
    j = pl.program_id(1)
    nj = pl.num_programs(1)
    tm = h_ref.shape[0]
    hist = 2 * shift
    base = ext_ref.shape[0] - tm

    @pl.when(j == 0)
    def _():
        h = h_ref[...]
        ms = jnp.mean(h * h, axis=-1, keepdims=True)
        n_ref[...] = (h * lax.rsqrt(ms + RMS_EPS) * g2_ref[...]).astype(bf16)
        acc_ref[...] = jnp.zeros_like(acc_ref)

    n = n_ref[...]
    gate = jnp.dot(n, wg_ref[...], preferred_element_type=f32)
    up = jnp.dot(n, wu_ref[...], preferred_element_type=f32)

    @pl.when(i % tiles_per_seq == 0)
    def _():
        ext_ref[base - hist:base, :] = cprev_ref[0]

    @pl.when(i % tiles_per_seq != 0)
    def _():
        ext_ref[base - hist:base, :] = carry_ref[j]

    ext_ref[base:, :] = gate
    cw = cw_ref[...]
    c = (cb_ref[...] + ext_ref[base - hist:base - hist + tm, :] * cw[0:1, :]
         + ext_ref[base - shift:base - shift + tm, :] * cw[1:2, :] + gate * cw[2:3, :])
    tail = ext_ref[base + tm - hist:base + tm, :]
    tail_ref[0] = tail
    carry_ref[j] = tail
    act = (c * jax.nn.sigmoid(c)) * up
    acc_ref[...] += jnp.dot(act.astype(bf16), wd_ref[...], preferred_element_type=f32)

    @pl.when(j == nj - 1)
    def _():
        out = h_ref[...] + acc_ref[...]
        if final_norm:
            ms = jnp.mean(out * out, axis=-1, keepdims=True)
            out = out * lax.rsqrt(ms + RMS_EPS) * gf_ref[...]
        o_ref[...] = out


def _convglu(h2d, conv_prev, norm2, norm_f, w_up_b, conv_w, conv_b, w_down_b, nseq_groups, shift, final_norm):
    m, d = h2d.shape
    d_ff = w_down_b.shape[0]
    rows_per_group = m // nseq_groups
    tm = min(512, rows_per_group)
    tf = 512
    nj = d_ff // tf
    tiles_per_seq = rows_per_group // tm
    hist = 2 * shift
    base = ((hist + 7) // 8) * 8
    out, tails = pl.pallas_call(
        functools.partial(_convglu_body, shift, tiles_per_seq, final_norm),
        grid=(m // tm, nj),
        in_specs=[
            pl.BlockSpec((tm, d), lambda i, j: (i, 0)),
            pl.BlockSpec((1, d), lambda i, j: (0, 0)),
            pl.BlockSpec((1, d), lambda i, j: (0, 0)),
            pl.BlockSpec((1, hist, tf), lambda i, j: (i // tiles_per_seq, 0, j)),
            pl.BlockSpec((d, tf), lambda i, j: (0, j)),
            pl.BlockSpec((d, tf), lambda i, j: (0, nj + j)),
            pl.BlockSpec((CONV_W, tf), lambda i, j: (0, j)),
            pl.BlockSpec((1, tf), lambda i, j: (0, j)),
            pl.BlockSpec((tf, d), lambda i, j: (j, 0)),
        ],
        out_specs=[
            pl.BlockSpec((tm, d), lambda i, j: (i, 0)),
            pl.BlockSpec((1, hist, tf), lambda i, j: (i, 0, j)),
        ],
        out_shape=[jax.ShapeDtypeStruct((m, d), f32), jax.ShapeDtypeStruct((m // tm, hist, d_ff), f32)],
        scratch_shapes=[pltpu.VMEM((tm, d), bf16), pltpu.VMEM((tm, d), f32), pltpu.VMEM((base + tm, tf), f32),
                        pltpu.VMEM((nj, hist, tf), f32)],
        compiler_params=_cparams(("arbitrary", "arbitrary")),
        name="convglu",
    )(h2d, norm2, norm_f, conv_prev, w_up_b, w_up_b, conv_w, conv_b, w_down_b)
    return out, tails[tiles_per_seq - 1::tiles_per_seq]


def _pad_cols(a, width):
    return jnp.pad(a, [(0, 0)] * (a.ndim - 1) + [(0, width - a.shape[-1])])


def _to_layout(a):
    rw = 3 * RW_WIDTH
    o = {}
    o["r"], o["k"], o["v"] = a[..., 0:RW_WIDTH], a[..., RW_WIDTH:2 * RW_WIDTH], a[..., 2 * RW_WIDTH:rw]
    p = rw
    o["wl"] = a[..., p:p + D_DECAY_LORA]; p += D_DECAY_LORA
    o["al"] = a[..., p:p + D_AAA_LORA]; p += D_AAA_LORA
    o["gl"] = a[..., p:p + D_GATE_LORA]; p += D_GATE_LORA
    o["q"] = a[..., p:p + ATT_WIDTH]; p += ATT_WIDTH
    o["ak"] = a[..., p:p + KV_WIDTH]; p += KV_WIDTH
    o["av"] = a[..., p:p + KV_WIDTH]; p += KV_WIDTH
    o["qi"] = a[..., p:p + IDX_HEADS * IDX_DIM]; p += IDX_HEADS * IDX_DIM
    o["ki"] = a[..., p:p + IDX_DIM]; p += IDX_DIM
    o["wi"] = a[..., p:p + IDX_HEADS]; p += IDX_HEADS
    d = (a.shape[-1] - p) // 2
    o["grw"], o["gatt"] = a[..., p:p + d], a[..., p + d:p + 2 * d]
    return jnp.concatenate([
        o["r"], o["k"], o["v"], o["q"], o["grw"], o["gatt"], o["qi"],
        _pad_cols(o["wl"], LANES), _pad_cols(o["al"], LANES), o["gl"],
        _pad_cols(o["ki"], LANES), o["ak"], o["av"], _pad_cols(o["wi"], LANES)], axis=-1)


def _rw_cols_layout(a):
    rw = 3 * RW_WIDTH
    wl = a[..., rw:rw + D_DECAY_LORA]
    al = a[..., rw + D_DECAY_LORA:rw + D_DECAY_LORA + D_AAA_LORA]
    gl = a[..., rw + D_DECAY_LORA + D_AAA_LORA:]
    return a[..., :rw], jnp.concatenate([_pad_cols(wl, LANES), _pad_cols(al, LANES), gl], axis=-1)


def _rw_cols_from_layout(c):
    return jnp.concatenate([c[..., :3 * RW_WIDTH], c[..., C_LORA:C_LORA + D_DECAY_LORA],
                            c[..., C_LORA + LANES:C_LORA + LANES + D_AAA_LORA],
                            c[..., C_LORA + 2 * LANES:C_LORA + 3 * LANES]], axis=-1)


def _rope_tables(pos, rows):
    pos = jnp.pad(pos.astype(f32), (0, rows - pos.shape[0]))
    out = []
    for dim in (HEAD_DIM, IDX_DIM):
        half = dim // 2
        inv_freq = 1.0 / (ROPE_THETA ** (jnp.arange(half, dtype=f32) / half))
        ang = pos[:, None] * inv_freq[None, :]
        cos, sin = jnp.cos(ang), jnp.sin(ang)
        reps = LANES // dim
        out.append(jnp.tile(jnp.concatenate([cos, cos], axis=1), (1, reps)))
        out.append(jnp.tile(jnp.concatenate([-sin, sin], axis=1), (1, reps)))
    return tuple(out)


def _pair_blockdiag(s):
    n = s.shape[0]
    s = s.reshape(n, RW_HEADS // 2, 2, RW_HEAD, RW_HEAD)
    z = jnp.zeros_like(s[:, :, 0])
    top = jnp.concatenate([s[:, :, 0], z], axis=-1)
    bot = jnp.concatenate([z, s[:, :, 1]], axis=-1)
    return jnp.concatenate([top, bot], axis=-2)


def _pair_unblock(s):
    n = s.shape[0]
    a = s[:, :, :RW_HEAD, :RW_HEAD]
    b = s[:, :, RW_HEAD:, RW_HEAD:]
    return jnp.stack([a, b], axis=2).reshape(n, RW_HEADS, RW_HEAD, RW_HEAD)


def kernel(x_prompt, x_sample, cache_k, cache_v, cache_kidx, state_wkv, state_shift, state_conv, page_table, norm1, w_in, rw_mu, rw_w0, rw_w2, rw_a0, rw_a2, rw_g2, rw_k_k, rw_k_a, rw_r_k, rw_lnx_w, rw_lnx_b, p_rw, p_att, w_o, norm2, w_up, conv_w, conv_b, w_down, norm_f):
    B, S, D = x_prompt.shape
    DB, DS, _ = x_sample.shape
    depth = w_in.shape[0]
    page = cache_k.shape[2]
    n_pages = page_table.shape[1]
    past_len = n_pages * page
    d_ff = w_down.shape[1]
    dt = x_prompt.dtype

    tabs_p = _rope_tables(jnp.arange(S), S)
    tabs_s = _rope_tables(past_len + jnp.arange(DS), ROWS)
    row2 = lambda a: a.reshape(1, -1)

    hp = x_prompt.reshape(B * S, D)
    hs = x_sample.reshape(DB * DS, D)
    outs = {k: [] for k in ("kp", "vp", "kip", "ks", "vs", "kis", "wkvp", "wkvs", "shp", "shs", "cvp", "cvs")}
    for l in range(depth):
        w_in_p = _to_layout(w_in[l]).astype(bf16)
        mu_rkv, mu_lora = _rw_cols_layout(rw_mu[l][None, :])
        params = dict(
            mu_rkv=mu_rkv, mu_lora=mu_lora, w0=row2(rw_w0[l]),
            w2=jnp.pad(rw_w2[l], ((0, LANES - D_DECAY_LORA), (0, 0))).astype(bf16), a0=row2(rw_a0[l]),
            a2=jnp.pad(rw_a2[l], ((0, LANES - D_AAA_LORA), (0, 0))).astype(bf16), g2=rw_g2[l].astype(bf16),
            k_k=row2(rw_k_k[l]), k_a=row2(rw_k_a[l]), r_k=row2(rw_r_k[l]), lnx_w=row2(rw_lnx_w[l]),
            lnx_b=row2(rw_lnx_b[l]))
        last = l == depth - 1

        c_p = _in_proj(hp, row2(norm1[l]), w_in_p)
        c_s = _in_proj(hs, row2(norm1[l]), w_in_p)

        sp_rkv = jnp.zeros((B, 1, RKV_W), dt)
        sp_lora = jnp.zeros((B, 1, LORA_W), dt)
        s0_p = jnp.zeros((B, RW_HEADS // 2, LANES, LANES), dt)
        o_rw_p, wkv_p = _rwkv_prompt(c_p, sp_rkv, sp_lora, s0_p, params, B, S)

        c_s3 = c_s.reshape(DB, DS, NP_COLS)
        ss_rkv, ss_lora = _rw_cols_layout(state_shift[l])
        prev_rkv = jnp.concatenate([ss_rkv[:, None, :], c_s3[:, :-1, :RKV_W]], axis=1).reshape(DB * DS, RKV_W)
        prev_lora = jnp.concatenate([ss_lora[:, None, :], c_s3[:, :-1, C_LORA:C_LORA + LORA_W]], axis=1)
        prev_lora = prev_lora.reshape(DB * DS, LORA_W)
        r_s, lw_s, k_s, v_s, a_s, b_s, g_s = _rwkv_prep(c_s, prev_rkv, prev_lora, params)
        y_s, wkv_s = _rwkv_seq((r_s, lw_s, k_s, v_s, a_s, b_s), _pair_blockdiag(state_wkv[l]), DB, DS)
        o_rw_s = _rwkv_post(y_s.reshape(DB * DS, RW_WIDTH), r_s, k_s, v_s, g_s, params)

        o_att_p, k_p, ki_p = _dsa_prompt(c_p, tabs_p, B, S)

        def rows8(lo, w):
            return jnp.pad(c_s3[:, :, lo:lo + w], ((0, 0), (0, ROWS - DS), (0, 0)))

        csel = (rows8(C_Q, ATT_WIDTH), rows8(C_QI, IDX_HEADS * IDX_DIM), rows8(C_AK, KV_WIDTH),
                rows8(C_AV, KV_WIDTH), rows8(C_KI, LANES))
        wi_s = jnp.pad(c_s3[:, :, C_WI:C_WI + IDX_HEADS], ((0, 0), (0, ROWS - DS), (0, 0)))
        wrow = jnp.transpose(wi_s, (0, 2, 1)).reshape(DB, 1, IDX_HEADS * ROWS)
        o_att_s8, k_s8, ki_s8 = _dsa_sample(
            csel, wrow, tabs_s, page_table, cache_k[l].reshape(-1, page, KV_WIDTH),
            cache_v[l].reshape(-1, page, KV_WIDTH), cache_kidx[l], DS)
        o_att_s = o_att_s8[:, :DS].reshape(DB * DS, ATT_WIDTH)

        p_rw_b, p_att_b, w_o_b = p_rw[l].astype(bf16), p_att[l].astype(bf16), w_o[l].astype(bf16)
        h_p = _outproj(_merge(o_rw_p, o_att_p, c_p, p_rw_b, p_att_b), hp, w_o_b)
        h_s = _outproj(_merge(o_rw_s, o_att_s, c_s, p_rw_b, p_att_b), hs, w_o_b)

        w_up_b, w_down_b = w_up[l].astype(bf16), w_down[l].astype(bf16)
        cv_args = (row2(norm2[l]), row2(norm_f), w_up_b, conv_w[l], row2(conv_b[l]), w_down_b)
        hp, tail_p = _convglu(h_p, jnp.zeros((B, CONV_W - 1, d_ff), dt), *cv_args, B, 1, last)
        h_s_tm = h_s.reshape(DB, DS, D).transpose(1, 0, 2).reshape(DS * DB, D)
        cprev_tm = state_conv[l].transpose(1, 0, 2).reshape(1, (CONV_W - 1) * DB, d_ff)
        hs_tm, tail_s = _convglu(h_s_tm, cprev_tm, *cv_args, 1, DB, last)
        hs = hs_tm.reshape(DS, DB, D).transpose(1, 0, 2).reshape(DB * DS, D)

        outs["kp"].append(k_p.reshape(B, S // page, page, KV_HEADS, HEAD_DIM))
        outs["vp"].append(c_p[:, C_AV:C_AV + KV_WIDTH].reshape(B, S // page, page, KV_HEADS, HEAD_DIM))
        outs["kip"].append(ki_p.reshape(B, S // page, page, IDX_DIM))
        outs["ks"].append(k_s8[:, :DS].reshape(DB, DS, KV_HEADS, HEAD_DIM))
        outs["vs"].append(c_s3[:, :, C_AV:C_AV + KV_WIDTH].reshape(DB, DS, KV_HEADS, HEAD_DIM))
        outs["kis"].append(ki_s8[:, :DS])
        outs["wkvp"].append(_pair_unblock(wkv_p))
        outs["wkvs"].append(_pair_unblock(wkv_s))
        outs["shp"].append(_rw_cols_from_layout(c_p.reshape(B, S, NP_COLS)[:, -1]))
        outs["shs"].append(_rw_cols_from_layout(c_s3[:, -1]))
        outs["cvp"].append(tail_p)
        outs["cvs"].append(tail_s.reshape(CONV_W - 1, DB, d_ff).transpose(1, 0, 2))

    y_prompt = hp.reshape(B, S, D)
    y_sample = hs.reshape(DB, DS, D)
    st = lambda k: jnp.stack(outs[k])
    return (y_prompt, y_sample, st("kp"), st("vp"), st("kip"), st("ks"), st("vs"), st("kis"), st("wkvp"),
            st("wkvs"), st("shp"), st("shs"), st("cvp"), st("cvs"))
```

```python
import functools
import math

import numpy as np
import jax
import jax.numpy as jnp
from jax import lax
from jax.experimental import pallas as pl
from jax.experimental.pallas import tpu as pltpu

f32 = jnp.float32
bf16 = jnp.bfloat16
i32 = jnp.int32

RW_HEADS = 16
RW_HEAD = 64
RW_WIDTH = RW_HEADS * RW_HEAD
D_DECAY_LORA = 96
D_AAA_LORA = 96
D_GATE_LORA = 128
GN_EPS = 64e-5
ATT_HEADS = 8
KV_HEADS = 2
HEAD_DIM = 128
ATT_WIDTH = ATT_HEADS * HEAD_DIM
KV_WIDTH = KV_HEADS * HEAD_DIM
IDX_HEADS = 16
IDX_DIM = 64
IDX_W_SCALE = (IDX_HEADS * IDX_DIM) ** -0.5
TOPK_MAX = 256
Q_BLOCK = 128
ROPE_THETA = 10000.0
CONV_W = 3
RMS_EPS = 1e-6

LANES = 128
NEG_BIG = -1e30
INT_MIN = -(2 ** 31)

C_R, C_K, C_V, C_Q, C_GRW, C_GATT, C_QI = 0, 1024, 2048, 3072, 4096, 6144, 8192
C_LORA = 9216
C_KI, C_AK, C_AV, C_WI = 9600, 9728, 9984, 10240
NP_COLS = 10368
LORA_W = 384
RKV_W = 3 * RW_WIDTH

VMEM_LIMIT = 56 * 1024 * 1024


def _cparams(sem):
    return pltpu.CompilerParams(dimension_semantics=sem, vmem_limit_bytes=VMEM_LIMIT)


def _bdot(a, b):
    return jnp.dot(a.astype(bf16), b.astype(bf16), preferred_element_type=f32)


def _bdot_nt(a, b):
    return lax.dot_general(a.astype(bf16), b.astype(bf16), (((1,), (1,)), ((), ())), preferred_element_type=f32)


def _bdot_tn(a, b):
    return lax.dot_general(a.astype(bf16), b.astype(bf16), (((0,), (0,)), ((), ())), preferred_element_type=f32)


def _split3(x):
    x1 = x.astype(bf16)
    r1 = x - x1.astype(f32)
    x2 = r1.astype(bf16)
    x3 = (r1 - x2.astype(f32)).astype(bf16)
    return x1, x2, x3


def _split2(x):
    x1 = x.astype(bf16)
    return x1, (x - x1.astype(f32)).astype(bf16)


def _headsum(x):
    ri = lax.broadcasted_iota(i32, (LANES, LANES), 0) // RW_HEAD
    ci = lax.broadcasted_iota(i32, (LANES, LANES), 1) // RW_HEAD
    bd = (ri == ci).astype(bf16)
    outs = []
    for i in range(x.shape[1] // LANES):
        hi, lo = _split2(x[:, i * LANES:(i + 1) * LANES])
        outs.append(jnp.dot(hi, bd, preferred_element_type=f32) + jnp.dot(lo, bd, preferred_element_type=f32))
    return jnp.concatenate(outs, axis=1)


def _softplus(x):
    return jnp.maximum(x, 0.0) + jnp.log1p(jnp.exp(-jnp.abs(x)))


def _rope(x, cos, sin, half):
    w = x.shape[1]
    reps = w // LANES
    if reps > 1:
        cos = jnp.concatenate([cos] * reps, axis=1)
        sin = jnp.concatenate([sin] * reps, axis=1)
    if 2 * half == LANES and w == LANES:
        partner = pltpu.roll(x, half, axis=1)
    else:
        lane = lax.broadcasted_iota(i32, (1, w), 1)
        first = (lane % (2 * half)) < half
        partner = jnp.where(first, pltpu.roll(x, w - half, axis=1), pltpu.roll(x, half, axis=1))
    return x * cos + partner * sin


def _inproj_body(x_ref, g_ref, w_ref, o_ref, n_ref):
    @pl.when(pl.program_id(1) == 0)
    def _():
        x = x_ref[...]
        ms = jnp.mean(x * x, axis=-1, keepdims=True)
        n_ref[...] = (x * lax.rsqrt(ms + RMS_EPS) * g_ref[...]).astype(bf16)

    o_ref[...] = jnp.dot(n_ref[...], w_ref[...], preferred_element_type=f32)


def _in_proj(x2d, gain, w_p):
    m, d = x2d.shape
    tm = min(512, m)
    tn = NP_COLS // 9
    return pl.pallas_call(
        _inproj_body,
        grid=(m // tm, NP_COLS // tn),
        in_specs=[
            pl.BlockSpec((tm, d), lambda i, j: (i, 0)),
            pl.BlockSpec((1, d), lambda i, j: (0, 0)),
            pl.BlockSpec((d, tn), lambda i, j: (0, j)),
        ],
        out_specs=pl.BlockSpec((tm, tn), lambda i, j: (i, j)),
        out_shape=jax.ShapeDtypeStruct((m, NP_COLS), f32),
        scratch_shapes=[pltpu.VMEM((tm, d), bf16)],
        compiler_params=_cparams(("parallel", "arbitrary")),
        name="in_proj",
    )(x2d, gain, w_p)


RW_PARAM_NAMES = ("mu_rkv", "mu_lora", "w0", "w2", "a0", "a2", "g2", "k_k", "k_a", "r_k", "lnx_w", "lnx_b")


def _rw_prep(c_rkv, c_lora, p_rkv, p_lora, P):
    m = c_rkv + (p_rkv - c_rkv) * P["mu_rkv"]
    ml = c_lora + (p_lora - c_lora) * P["mu_lora"]
    r, k, v = m[:, :RW_WIDTH], m[:, RW_WIDTH:2 * RW_WIDTH], m[:, 2 * RW_WIDTH:]
    wl, al, gl = ml[:, :LANES], ml[:, LANES:2 * LANES], ml[:, 2 * LANES:]
    w_log = -_softplus(-(P["w0"] + _bdot(jnp.tanh(wl), P["w2"]))) - 0.5
    logw = -jnp.exp(w_log)
    asig = jax.nn.sigmoid(P["a0"] + _bdot(al, P["a2"]))
    g = _bdot(jax.nn.sigmoid(gl), P["g2"])
    kk = k * P["k_k"]
    kkn = kk * lax.rsqrt(jnp.maximum(_headsum(kk * kk), 1e-24))
    k2 = k * (1.0 + (asig - 1.0) * P["k_a"])
    return r, logw, k2, v, -kkn, kkn * asig, g


def _rw_post(y, r, k2, v, g, P):
    inv_n = 1.0 / RW_HEAD
    mean = _headsum(y) * inv_n
    d = y - mean
    var = _headsum(d * d) * inv_n
    yn = d * lax.rsqrt(var + GN_EPS) * P["lnx_w"] + P["lnx_b"]
    bonus = _headsum(r * k2 * P["r_k"]) * v
    return (yn + bonus) * g


def _chunk_step(at, rt, bt, kt, be, ke, v, wc, S, masks, nsteps):
    keep, eye2, colh, mA, bdmask = masks
    R = range(len(at))
    C = at[0].shape[0]
    X = [jnp.concatenate([at[p], rt[p]], axis=0) for p in R]
    scA = [jnp.where(keep, _bdot_nt(jnp.where(mA, X[p], 0.0), jnp.concatenate([bt[p], kt[p]], axis=0)), 0.0)
           for p in R]
    scB = [jnp.where(keep, _bdot_nt(jnp.where(mA, 0.0, X[p]), jnp.concatenate([kt[p], bt[p]], axis=0)), 0.0)
           for p in R]
    XS = [_bdot_nt(X[p], S[p]) for p in R]
    Lp = [jnp.concatenate([jnp.where(colh, scA[p][:C], 0.0), jnp.where(colh, 0.0, scB[p][:C])], axis=0) for p in R]
    T = [eye2 + Lp[p] for p in R]
    if nsteps > 0:
        Pw = [_bdot(Lp[p], Lp[p]) for p in R]
        for i in range(nsteps):
            if i < nsteps - 1:
                Z = [_bdot(Pw[p], jnp.concatenate([T[p], Pw[p]], axis=1)) for p in R]
                T = [T[p] + Z[p][:, :2 * C] for p in R]
                Pw = [Z[p][:, 2 * C:] for p in R]
            else:
                T = [T[p] + _bdot(Pw[p], T[p]) for p in R]
    vA = [jnp.where(mA, v[p], 0.0) for p in R]
    vB = [jnp.where(mA, 0.0, v[p]) for p in R]
    G = [XS[p][:C] + _bdot(jnp.where(colh, scB[p][:C], scA[p][:C]), jnp.concatenate([vB[p], vA[p]], axis=0))
         for p in R]
    U = [_bdot(jnp.where(colh, T[p][:C], T[p][C:]),
               jnp.concatenate([jnp.where(mA, G[p], 0.0), jnp.where(mA, 0.0, G[p])], axis=0)) for p in R]
    Y = [XS[p][C:] + _bdot(jnp.concatenate([scA[p][C:], scB[p][C:]], axis=1),
                           jnp.concatenate([jnp.where(mA, U[p], 0.0), vA[p], vB[p], jnp.where(mA, 0.0, U[p])], axis=0))
         for p in R]
    dS = [_bdot_tn(jnp.concatenate([U[p], v[p]], axis=0), jnp.concatenate([be[p], ke[p]], axis=0)) for p in R]
    S_new = [S[p] * wc[p] + jnp.where(bdmask, dS[p], 0.0) for p in R]
    return Y, S_new


def _chunk_masks(C):
    row = lax.broadcasted_iota(i32, (2 * C, 2 * C), 0)
    col = lax.broadcasted_iota(i32, (2 * C, 2 * C), 1)
    t = jnp.where(row >= C, row - C, row)
    s = jnp.where(col >= C, col - C, col)
    keep = (s < t) | ((row >= C) & (s == t))
    eye2 = (row == col).astype(f32)
    colh = lax.broadcasted_iota(i32, (C, 2 * C), 1) < C
    mA = lax.broadcasted_iota(i32, (1, LANES), 1) < RW_HEAD
    r2 = lax.broadcasted_iota(i32, (LANES, LANES), 0) // RW_HEAD
    c2 = lax.broadcasted_iota(i32, (LANES, LANES), 1) // RW_HEAD
    return keep, eye2, colh, mA, r2 == c2


def _rwkv_prompt_body(chunk, rkv_ref, lora_ref, sp_rkv_ref, sp_lora_ref, s0_ref, *rest):
    np_ = len(RW_PARAM_NAMES)
    P = {n: rest[i][...] for i, n in enumerate(RW_PARAM_NAMES)}
    o_ref, sout_ref = rest[np_], rest[np_ + 1]
    (S_ref, car_rkv, car_lora, at_s, rt_s, bt_s, kt_s, be_s, ke_s, v_s, cum_s, y_s) = rest[np_ + 2:]
    t = pl.program_id(1)
    tt = rkv_ref.shape[0]
    C = chunk

    @pl.when(t == 0)
    def _():
        S_ref[...] = s0_ref[0]
        car_rkv[...] = sp_rkv_ref[0]
        car_lora[...] = sp_lora_ref[0]

    c_rkv = rkv_ref[...]
    c_lora = lora_ref[...]
    first = lax.broadcasted_iota(i32, (tt, 1), 0) == 0
    p_rkv = jnp.where(first, car_rkv[...], pltpu.roll(c_rkv, 1, axis=0))
    p_lora = jnp.where(first, car_lora[...], pltpu.roll(c_lora, 1, axis=0))
    car_rkv[...] = c_rkv[tt - 1:tt, :]
    car_lora[...] = c_lora[tt - 1:tt, :]
    r, logw, k2, v, a, b, g = _rw_prep(c_rkv, c_lora, p_rkv, p_lora, P)

    ri = lax.broadcasted_iota(i32, (tt, tt), 0)
    ci = lax.broadcasted_iota(i32, (tt, tt), 1)
    same = ri // C == ci // C
    tril = (same & (ci <= ri)).astype(bf16)
    triu = (same & (ci > ri)).astype(bf16)
    pieces = _split3(logw)
    cum = sum(jnp.dot(tril, piece, preferred_element_type=f32) for piece in pieces)
    rev = sum(jnp.dot(triu, piece, preferred_element_type=f32) for piece in pieces)
    iw = jnp.exp(-cum)
    ew = jnp.exp(rev)
    at_s[...] = a * jnp.exp(cum - logw)
    rt_s[...] = r * jnp.exp(cum)
    bt_s[...] = b * iw
    kt_s[...] = k2 * iw
    be_s[...] = b * ew
    ke_s[...] = k2 * ew
    v_s[...] = v
    cum_s[...] = cum
    masks = _chunk_masks(C)
    nsteps = max(int(math.ceil(math.log2(C))) - 1, 0)
    pairs = range(RW_WIDTH // LANES)

    def chunk_body(ci_, carry):
        r0 = pl.multiple_of(ci_ * C, C)
        lanes = [slice(p * LANES, (p + 1) * LANES) for p in pairs]
        ld = lambda ref: [ref[pl.ds(r0, C), lanes[p]] for p in pairs]
        last8 = pl.multiple_of(r0 + C - 8, 8)
        wc = [jnp.exp(cum_s[pl.ds(last8, 8), lanes[p]][7:8]) for p in pairs]
        Y, S_new = _chunk_step(ld(at_s), ld(rt_s), ld(bt_s), ld(kt_s), ld(be_s), ld(ke_s), ld(v_s), wc,
                               [S_ref[p] for p in pairs], masks, nsteps)
        for p in pairs:
            y_s[pl.ds(r0, C), lanes[p]] = Y[p]
            S_ref[p] = S_new[p]
        return carry

    lax.fori_loop(0, tt // C, chunk_body, 0)
    o_ref[...] = _rw_post(y_s[...], r, k2, v, g, P)

    @pl.when(t == pl.num_programs(1) - 1)
    def _():
        sout_ref[0] = S_ref[...]


def _const_spec(shape):
    nd = len(shape)
    return pl.BlockSpec(shape, lambda *_: (0,) * nd)


def _rwkv_prompt(c, sp_rkv, sp_lora, s0, params, nb, seq):
    tt = min(256, seq)
    chunk = min(64, tt)
    nt = seq // tt
    npairs = RW_WIDTH // LANES
    in_specs = [
        pl.BlockSpec((tt, RKV_W), lambda b, t: (b * nt + t, 0)),
        pl.BlockSpec((tt, LORA_W), lambda b, t: (b * nt + t, C_LORA // LORA_W)),
        pl.BlockSpec((1, 1, RKV_W), lambda b, t: (b, 0, 0)),
        pl.BlockSpec((1, 1, LORA_W), lambda b, t: (b, 0, 0)),
        pl.BlockSpec((1, npairs, LANES, LANES), lambda b, t: (b, 0, 0, 0)),
    ] + [_const_spec(params[n].shape) for n in RW_PARAM_NAMES]
    out_specs = [
        pl.BlockSpec((tt, RW_WIDTH), lambda b, t: (b * nt + t, 0)),
        pl.BlockSpec((1, npairs, LANES, LANES), lambda b, t: (b, 0, 0, 0)),
    ]
    scratch = [pltpu.VMEM((npairs, LANES, LANES), f32), pltpu.VMEM((1, RKV_W), f32), pltpu.VMEM((1, LORA_W), f32)]
    scratch += [pltpu.VMEM((tt, RW_WIDTH), f32) for _ in range(9)]
    return pl.pallas_call(
        functools.partial(_rwkv_prompt_body, chunk),
        grid=(nb, nt),
        in_specs=in_specs,
        out_specs=out_specs,
        out_shape=[jax.ShapeDtypeStruct((nb * seq, RW_WIDTH), f32),
                   jax.ShapeDtypeStruct((nb, npairs, LANES, LANES), f32)],
        scratch_shapes=scratch,
        compiler_params=_cparams(("parallel", "arbitrary")),
        name="rwkv_prompt",
    )(c, c, sp_rkv, sp_lora, s0, *[params[n] for n in RW_PARAM_NAMES])


def _rwkv_prep_body(rkv_ref, lora_ref, prkv_ref, plora_ref, *rest):
    np_ = len(RW_PARAM_NAMES)
    P = {n: rest[i][...] for i, n in enumerate(RW_PARAM_NAMES)}
    outs = rest[np_:]
    vals = _rw_prep(rkv_ref[...], lora_ref[...], prkv_ref[...], plora_ref[...], P)
    for o, v in zip(outs, vals):
        o[...] = v


def _rwkv_prep(c, prev_rkv, prev_lora, params):
    m = c.shape[0]
    tm = min(512, m)
    in_specs = [
        pl.BlockSpec((tm, RKV_W), lambda i: (i, 0)),
        pl.BlockSpec((tm, LORA_W), lambda i: (i, C_LORA // LORA_W)),
        pl.BlockSpec((tm, RKV_W), lambda i: (i, 0)),
        pl.BlockSpec((tm, LORA_W), lambda i: (i, 0)),
    ] + [_const_spec(params[n].shape) for n in RW_PARAM_NAMES]
    return pl.pallas_call(
        _rwkv_prep_body,
        grid=(m // tm,),
        in_specs=in_specs,
        out_specs=[pl.BlockSpec((tm, RW_WIDTH), lambda i: (i, 0)) for _ in range(7)],
        out_shape=[jax.ShapeDtypeStruct((m, RW_WIDTH), f32) for _ in range(7)],
        compiler_params=_cparams(("parallel",)),
        name="rwkv_prep",
    )(c, c, prev_rkv, prev_lora, *[params[n] for n in RW_PARAM_NAMES])


def _rwkv_seq_body(r_ref, lw_ref, k_ref, v_ref, a_ref, b_ref, s_ref, y_ref, sout_ref):
    steps = r_ref.shape[1]
    eye = (lax.broadcasted_iota(i32, (LANES, LANES), 0) == lax.broadcasted_iota(i32, (LANES, LANES), 1)).astype(f32)
    bd = (lax.broadcasted_iota(i32, (LANES, LANES), 0) // RW_HEAD) == (
        lax.broadcasted_iota(i32, (LANES, LANES), 1) // RW_HEAD)
    pairs = range(RW_WIDTH // LANES)
    S = [s_ref[0, p] for p in pairs]
    for t in range(steps):
        sl = [(0, slice(t, t + 1), slice(p * LANES, (p + 1) * LANES)) for p in pairs]
        sa = [jnp.sum(S[p] * a_ref[sl[p]], axis=1, keepdims=True) for p in pairs]
        v_col = [jnp.sum(eye * v_ref[sl[p]], axis=1, keepdims=True) for p in pairs]
        S = [S[p] * jnp.exp(lw_ref[sl[p]]) + jnp.where(bd, sa[p] * b_ref[sl[p]] + v_col[p] * k_ref[sl[p]], 0.0)
             for p in pairs]
        y_col = [jnp.sum(S[p] * r_ref[sl[p]], axis=1, keepdims=True) for p in pairs]
        for p in pairs:
            y_ref[sl[p]] = jnp.sum(eye * y_col[p], axis=0, keepdims=True)
    for p in pairs:
        sout_ref[0, p] = S[p]


def _rwkv_seq(ops, s0, nreq, steps):
    npairs = RW_WIDTH // LANES
    ops3 = [o.reshape(nreq, steps, RW_WIDTH) for o in ops]
    return pl.pallas_call(
        _rwkv_seq_body,
        grid=(nreq,),
        in_specs=[pl.BlockSpec((1, steps, RW_WIDTH), lambda b: (b, 0, 0)) for _ in range(6)]
        + [pl.BlockSpec((1, npairs, LANES, LANES), lambda b: (b, 0, 0, 0))],
        out_specs=[pl.BlockSpec((1, steps, RW_WIDTH), lambda b: (b, 0, 0)),
                   pl.BlockSpec((1, npairs, LANES, LANES), lambda b: (b, 0, 0, 0))],
        out_shape=[jax.ShapeDtypeStruct((nreq, steps, RW_WIDTH), f32),
                   jax.ShapeDtypeStruct((nreq, npairs, LANES, LANES), f32)],
        compiler_params=_cparams(("parallel",)),
        name="rwkv_seq",
    )(*ops3, s0)


def _rwkv_post_body(y_ref, r_ref, k_ref, v_ref, g_ref, *rest):
    np_ = len(RW_PARAM_NAMES)
    P = {n: rest[i][...] for i, n in enumerate(RW_PARAM_NAMES)}
    rest[np_][...] = _rw_post(y_ref[...], r_ref[...], k_ref[...], v_ref[...], g_ref[...], P)


def _rwkv_post(y, r, k2, v, g, params):
    m = y.shape[0]
    tm = min(512, m)
    return pl.pallas_call(
        _rwkv_post_body,
        grid=(m // tm,),
        in_specs=[pl.BlockSpec((tm, RW_WIDTH), lambda i: (i, 0)) for _ in range(5)]
        + [_const_spec(params[n].shape) for n in RW_PARAM_NAMES],
        out_specs=pl.BlockSpec((tm, RW_WIDTH), lambda i: (i, 0)),
        out_shape=jax.ShapeDtypeStruct((m, RW_WIDTH), f32),
        compiler_params=_cparams(("parallel",)),
        name="rwkv_post",
    )(y, r, k2, v, g, *[params[n] for n in RW_PARAM_NAMES])


def _sort_key(score):
    bits = pltpu.bitcast(score, i32)
    key = jnp.where(bits < 0, bits ^ jnp.int32(0x7FFFFFFF), bits)
    return jnp.where(score == 0.0, jnp.int32(0), key)


def _kth_largest(count_ge, shape, k):
    def body(i, t):
        cand = t + lax.shift_left(jnp.int32(1), jnp.int32(31) - i)
        return jnp.where(count_ge(cand) >= k, cand, t)

    return lax.fori_loop(0, 32, body, jnp.full(shape, INT_MIN, i32))


def _dsa_prompt_body(topk, q_ref, qi_ref, wi_ref, k_ref, v_ref, ki_ref, cosk_ref, sink_ref, cosi_ref, sini_ref,
                     o_ref, kout_ref, kiout_ref, kb_ref, vb_ref, kib_ref, bias_ref):
    qb = pl.program_id(1)
    seq = k_ref.shape[0]
    nq = q_ref.shape[0]

    @pl.when(qb == 0)
    def _():
        cos, sin = cosk_ref[...], sink_ref[...]
        kr = jnp.concatenate([_rope(k_ref[:, h * HEAD_DIM:(h + 1) * HEAD_DIM], cos, sin, HEAD_DIM // 2)
                              for h in range(KV_HEADS)], axis=1)
        kout_ref[...] = kr
        kb_ref[...] = kr.astype(bf16)
        vb_ref[...] = v_ref[...].astype(bf16)
        kir = _rope(ki_ref[...], cosi_ref[...], sini_ref[...], IDX_DIM // 2)
        kiout_ref[...] = kir[:, :IDX_DIM]
        kib_ref[...] = (kir + pltpu.roll(kir, IDX_DIM, axis=1)).astype(bf16)

    r0 = pl.multiple_of(qb * nq, nq)
    cosq, sinq = cosk_ref[pl.ds(r0, nq), :], sink_ref[pl.ds(r0, nq), :]
    cosqi, sinqi = cosi_ref[pl.ds(r0, nq), :], sini_ref[pl.ds(r0, nq), :]
    q = _rope(q_ref[...], cosq, sinq, HEAD_DIM // 2).astype(bf16)
    qi = _rope(qi_ref[...], cosqi, sinqi, IDX_DIM // 2)
    wi = wi_ref[...] * IDX_W_SCALE
    lane = lax.broadcasted_iota(i32, (1, LANES), 1)
    scale = HEAD_DIM ** -0.5
    hpg = ATT_HEADS // KV_HEADS

    def process(ext):
        kib = kib_ref[:ext, :]
        sc = None
        for h in range(IDX_HEADS):
            pair = qi[:, (h // 2) * LANES:(h // 2 + 1) * LANES]
            mine = (lane < IDX_DIM) if h % 2 == 0 else (lane >= IDX_DIM)
            term = jnp.maximum(_bdot_nt(jnp.where(mine, pair, 0.0), kib), 0.0) * wi[:, h:h + 1]
            sc = term if sc is None else sc + term

        qpos = r0 + lax.broadcasted_iota(i32, (nq, 1), 0)
        kpos = lax.broadcasted_iota(i32, (1, ext), 1)
        causal = kpos <= qpos
        key = jnp.where(causal, _sort_key(sc), INT_MIN)

        def count_ge(cand):
            return jnp.sum((key >= cand).astype(f32), axis=1, keepdims=True)

        thr = _kth_largest(count_ge, (nq, 1), float(topk))
        bias_ref[:, :ext] = jnp.where(causal & (key >= thr), 0.0, NEG_BIG)

        def qk(h):
            kg = kb_ref[:ext, (h // hpg) * HEAD_DIM:(h // hpg + 1) * HEAD_DIM]
            return _bdot_nt(q[:, h * HEAD_DIM:(h + 1) * HEAD_DIM], kg)

        s_next = qk(0)
        for h in range(ATT_HEADS):
            s = s_next * scale + bias_ref[:, :ext]
            if h + 1 < ATT_HEADS:
                s_next = qk(h + 1)
            m = jnp.max(s, axis=1, keepdims=True)
            p = jnp.exp(s - m)
            l = jnp.sum(p, axis=1, keepdims=True)
            vg = vb_ref[:ext, (h // hpg) * HEAD_DIM:(h // hpg + 1) * HEAD_DIM]
            o = jnp.dot(p.astype(bf16), vg, preferred_element_type=f32)
            o_ref[:, h * HEAD_DIM:(h + 1) * HEAD_DIM] = o / l

    nvar = 4 if seq // nq % 4 == 0 else 1
    per = seq // nq // nvar
    for var in range(nvar):
        pl.when(qb // per == var)(functools.partial(process, (var + 1) * per * nq))


def _dsa_prompt(c, tabs, nb, seq):
    nq = Q_BLOCK
    nblk = seq // nq
    topk = min(TOPK_MAX, seq // 4)
    cosk, sink, cosi, sini = tabs
    row = lambda b, j: b * nblk + j
    in_specs = [
        pl.BlockSpec((nq, ATT_WIDTH), lambda b, j: (row(b, j), C_Q // ATT_WIDTH)),
        pl.BlockSpec((nq, IDX_HEADS * IDX_DIM), lambda b, j: (row(b, j), C_QI // (IDX_HEADS * IDX_DIM))),
        pl.BlockSpec((nq, LANES), lambda b, j: (row(b, j), C_WI // LANES)),
        pl.BlockSpec((seq, KV_WIDTH), lambda b, j: (b, C_AK // KV_WIDTH)),
        pl.BlockSpec((seq, KV_WIDTH), lambda b, j: (b, C_AV // KV_WIDTH)),
        pl.BlockSpec((seq, LANES), lambda b, j: (b, C_KI // LANES)),
    ] + [_const_spec((seq, LANES)) for _ in range(4)]
    out_specs = [
        pl.BlockSpec((nq, ATT_WIDTH), lambda b, j: (row(b, j), 0)),
        pl.BlockSpec((seq, KV_WIDTH), lambda b, j: (b, 0)),
        pl.BlockSpec((seq, IDX_DIM), lambda b, j: (b, 0)),
    ]
    scratch = [pltpu.VMEM((seq, KV_WIDTH), bf16), pltpu.VMEM((seq, KV_WIDTH), bf16), pltpu.VMEM((seq, LANES), bf16),
               pltpu.VMEM((nq, seq), f32)]
    return pl.pallas_call(
        functools.partial(_dsa_prompt_body, topk),
        grid=(nb, nblk),
        in_specs=in_specs,
        out_specs=out_specs,
        out_shape=[jax.ShapeDtypeStruct((nb * seq, ATT_WIDTH), f32),
                   jax.ShapeDtypeStruct((nb * seq, KV_WIDTH), f32),
                   jax.ShapeDtypeStruct((nb * seq, IDX_DIM), f32)],
        scratch_shapes=scratch,
        compiler_params=_cparams(("parallel", "arbitrary")),
        name="dsa_prompt",
    )(c, c, c, c, c, c, cosk, sink, cosi, sini)


ROWS = 8


def _page_copies(pt_ref, req, n_pages, srcs_dsts_sems):
    out = []
    for p in range(n_pages):
        pg = pt_ref[req, p]
        for hbm, dst, sem in srcs_dsts_sems:
            out.append(pltpu.make_async_copy(hbm.at[pg], dst(p), sem))
    return out


def _dsa_select_body(topk, n_new, group, pt_ref, qi_ref, wrow_ref, kin_ref, cosi_ref, sini_ref, cki_hbm,
                     sel_ref, kiout_ref, kibuf, sems):
    i = pl.program_id(0)
    nsteps = pl.num_programs(0)
    n_pages = pt_ref.shape[1]
    page = cki_hbm.shape[1]
    past = n_pages * page
    G = group

    def copies(step, slot):
        out = []
        for j in range(G):
            out += _page_copies(pt_ref, step * G + j, n_pages,
                                [(cki_hbm, lambda p, j=j: kibuf.at[slot, j, pl.ds(p * page, page)], sems.at[slot])])
        return out

    slot = i % 2

    @pl.when(i == 0)
    def _():
        for cp in copies(0, 0):
            cp.start()

    @pl.when(i + 1 < nsteps)
    def _():
        for cp in copies(i + 1, 1 - slot):
            cp.start()

    cosi, sini = cosi_ref[...], sini_ref[...]
    li = lax.broadcasted_iota(i32, (LANES, LANES), 0)
    lo = lax.broadcasted_iota(i32, (LANES, LANES), 1)
    qis, kins, wrows, sels = [], [], [], []
    for j in range(G):
        qi = _rope(qi_ref[j], cosi, sini, IDX_DIM // 2)
        kin = _rope(kin_ref[j], cosi, sini, IDX_DIM // 2)[:, :IDX_DIM]
        kiout_ref[j] = kin
        kins.append(kin)
        qis.append(jnp.concatenate([qi[:, h * IDX_DIM:(h + 1) * IDX_DIM] for h in range(IDX_HEADS)], axis=0))
        wrows.append(wrow_ref[0, :, j * LANES:(j + 1) * LANES] * IDX_W_SCALE)
        sels.append((((li % ROWS) * G + j == lo) & (li % ROWS < n_new)).astype(bf16))

    def idx_scores(ki, j):
        hi, lw = _split2(jnp.maximum(_bdot_nt(ki, qis[j]), 0.0) * wrows[j])
        return (jnp.dot(hi, sels[j], preferred_element_type=f32) + jnp.dot(lw, sels[j], preferred_element_type=f32))

    sc_n = sum(idx_scores(kins[j], j) for j in range(G))

    for cp in copies(i, slot):
        cp.wait()

    sc_p = sum(idx_scores(kibuf[slot, j], j) for j in range(G))
    pad = jnp.zeros((LANES - ROWS, LANES), f32)
    sc = jnp.concatenate([sc_p, sc_n, pad], axis=0).T
    nrow = n_new * G
    sc = sc[:nrow]
    width = past + LANES
    t_row = lax.broadcasted_iota(i32, (nrow, width), 0) // G
    col = lax.broadcasted_iota(i32, (nrow, width), 1)
    valid = (col < past) | ((col - past <= t_row) & (col - past < n_new))
    key = jnp.where(valid, _sort_key(sc), INT_MIN)

    def count_ge(cand):
        return jnp.sum((key >= cand).astype(f32), axis=1, keepdims=True)

    thr = _kth_largest(count_ge, (nrow, 1), float(topk))
    chosen = (valid & (key >= thr)).astype(f32)
    chosen = jnp.concatenate([chosen, jnp.zeros((LANES - nrow, width), f32)], axis=0)
    sel_ref[0] = chosen.T


def _dsa_sample_body(n_new, group, pt_ref, q_ref, kn_ref, vn_ref, sel_ref, cosk_ref, sink_ref, ck_hbm, cv_hbm,
                     o_ref, kout_ref, kbuf, vbuf, sems):
    b = pl.program_id(0)
    nreq = pl.num_programs(0)
    n_pages = pt_ref.shape[1]
    prow = ck_hbm.shape[1]
    past = n_pages * prow // KV_HEADS
    G = group

    def copies(req, slot):
        return _page_copies(pt_ref, req, n_pages,
                            [(ck_hbm, lambda p: kbuf.at[slot, pl.ds(p * prow, prow)], sems.at[0, slot]),
                             (cv_hbm, lambda p: vbuf.at[slot, pl.ds(p * prow, prow)], sems.at[1, slot])])

    slot = b % 2

    @pl.when(b == 0)
    def _():
        for cp in copies(0, 0):
            cp.start()

    @pl.when(b + 1 < nreq)
    def _():
        for cp in copies(b + 1, 1 - slot):
            cp.start()

    cosk, sink = cosk_ref[...], sink_ref[...]
    q = _rope(q_ref[0], cosk, sink, HEAD_DIM // 2)
    kn = jnp.concatenate([_rope(kn_ref[0][:, h * HEAD_DIM:(h + 1) * HEAD_DIM], cosk, sink, HEAD_DIM // 2)
                          for h in range(KV_HEADS)], axis=1)
    vn = vn_ref[0]
    kout_ref[0] = kn

    li = lax.broadcasted_iota(i32, (LANES, LANES), 0)
    lo = lax.broadcasted_iota(i32, (LANES, LANES), 1)
    route = ((lo % ROWS) * G + b % G == li) & (lo % ROWS < n_new)
    chosen = jnp.dot(sel_ref[0].astype(bf16), route.astype(bf16), preferred_element_type=f32)
    bias = (chosen - 1.0) * (-NEG_BIG)
    bias_p, bias_n = bias[:past], bias[past:past + ROWS]

    eye = (li == lo).astype(f32)
    scale = HEAD_DIM ** -0.5
    hpg = ATT_HEADS // KV_HEADS
    zrows = jnp.zeros((LANES - hpg * ROWS, HEAD_DIM), f32)

    for cp in copies(b, slot):
        cp.wait()

    for g in range(KV_HEADS):
        qg = jnp.concatenate([q[:, (g * hpg + r) * HEAD_DIM:(g * hpg + r + 1) * HEAD_DIM] for r in range(hpg)]
                             + [zrows], axis=0)
        gs = slice(g * HEAD_DIM, (g + 1) * HEAD_DIM)
        kp = kbuf[slot, pl.ds(g, past, stride=KV_HEADS), :]
        vp = vbuf[slot, pl.ds(g, past, stride=KV_HEADS), :]
        s_p = _bdot_nt(kp, qg) * scale + bias_p
        s_n = _bdot_nt(kn[:, gs], qg) * scale + bias_n
        m = jnp.maximum(jnp.max(s_p, axis=0, keepdims=True), jnp.max(s_n, axis=0, keepdims=True))
        p_p = jnp.exp(s_p - m)
        p_n = jnp.exp(s_n - m)
        l = jnp.sum(p_p, axis=0, keepdims=True) + jnp.sum(p_n, axis=0, keepdims=True)
        o = _bdot_tn(p_p, vp) + _bdot_tn(p_n, vn[:, gs])
        o = o / jnp.sum(eye * l, axis=1, keepdims=True)
        for r in range(hpg):
            h = g * hpg + r
            o_ref[0, :, h * HEAD_DIM:(h + 1) * HEAD_DIM] = o[r * ROWS:(r + 1) * ROWS]


def _dsa_sample(csel, wrow, tabs, page_table, cache_k, cache_v, cache_kidx, n_new):
    q8, qi8, kn8, vn8, kin8 = csel
    nreq = q8.shape[0]
    n_pages = page_table.shape[1]
    page = cache_kidx.shape[1]
    past = n_pages * page
    topk = min(TOPK_MAX, (past + n_new) // 4)
    group = min(LANES // (2 * ROWS), nreq)
    cosk, sink, cosi, sini = tabs
    anyspec = pl.BlockSpec(memory_space=pl.ANY)
    tab = pl.BlockSpec((ROWS, LANES), lambda b, pt: (0, 0))

    grp3 = lambda w: pl.BlockSpec((group, ROWS, w), lambda i, pt: (i, 0, 0))
    sel, ki_new = pl.pallas_call(
        functools.partial(_dsa_select_body, topk, n_new, group),
        grid_spec=pltpu.PrefetchScalarGridSpec(
            num_scalar_prefetch=1,
            grid=(nreq // group,),
            in_specs=[grp3(IDX_HEADS * IDX_DIM), pl.BlockSpec((1, 1, group * LANES), lambda i, pt: (i, 0, 0)),
                      grp3(LANES), tab, tab, anyspec],
            out_specs=[pl.BlockSpec((1, past + LANES, LANES), lambda i, pt: (i, 0, 0)), grp3(IDX_DIM)],
            scratch_shapes=[pltpu.VMEM((2, group, past, IDX_DIM), f32), pltpu.SemaphoreType.DMA((2,))],
        ),
        out_shape=[jax.ShapeDtypeStruct((nreq // group, past + LANES, LANES), f32),
                   jax.ShapeDtypeStruct((nreq, ROWS, IDX_DIM), f32)],
        compiler_params=_cparams(("arbitrary",)),
        name="dsa_select",
    )(page_table, qi8, wrow.reshape(nreq // group, 1, group * LANES), kin8, cosi, sini, cache_kidx)

    req3 = lambda w: pl.BlockSpec((1, ROWS, w), lambda b, pt: (b, 0, 0))
    prow = cache_k.shape[1]
    o, k_new = pl.pallas_call(
        functools.partial(_dsa_sample_body, n_new, group),
        grid_spec=pltpu.PrefetchScalarGridSpec(
            num_scalar_prefetch=1,
            grid=(nreq,),
            in_specs=[req3(ATT_WIDTH), req3(KV_WIDTH), req3(KV_WIDTH),
                      pl.BlockSpec((1, past + LANES, LANES), lambda b, pt: (b // group, 0, 0)), tab, tab,
                      anyspec, anyspec],
            out_specs=[req3(ATT_WIDTH), req3(KV_WIDTH)],
            scratch_shapes=[pltpu.VMEM((2, n_pages * prow, HEAD_DIM), f32),
                            pltpu.VMEM((2, n_pages * prow, HEAD_DIM), f32), pltpu.SemaphoreType.DMA((2, 2))],
        ),
        out_shape=[jax.ShapeDtypeStruct((nreq, ROWS, ATT_WIDTH), f32),
                   jax.ShapeDtypeStruct((nreq, ROWS, KV_WIDTH), f32)],
        compiler_params=_cparams(("arbitrary",)),
        name="dsa_sample",
    )(page_table, q8, kn8, vn8, sel, cosk, sink, cache_k, cache_v)
    return o, k_new, ki_new


def _merge_body(orw_ref, oatt_ref, grw_ref, gatt_ref, prw_ref, patt_ref, o_ref):
    a = jnp.dot(orw_ref[...].astype(bf16), prw_ref[...], preferred_element_type=f32)
    b = jnp.dot(oatt_ref[...].astype(bf16), patt_ref[...], preferred_element_type=f32)
    o_ref[...] = (jax.nn.sigmoid(grw_ref[...]) * a + jax.nn.sigmoid(gatt_ref[...]) * b).astype(bf16)


def _merge(o_rw, o_att, c, p_rw, p_att):
    m = o_rw.shape[0]
    d = p_rw.shape[1]
    tm = min(512, m)
    tn = 1024
    nj = d // tn
    return pl.pallas_call(
        _merge_body,
        grid=(m // tm, nj),
        in_specs=[
            pl.BlockSpec((tm, RW_WIDTH), lambda i, j: (i, 0)),
            pl.BlockSpec((tm, ATT_WIDTH), lambda i, j: (i, 0)),
            pl.BlockSpec((tm, tn), lambda i, j: (i, C_GRW // tn + j)),
            pl.BlockSpec((tm, tn), lambda i, j: (i, C_GATT // tn + j)),
            pl.BlockSpec((RW_WIDTH, tn), lambda i, j: (0, j)),
            pl.BlockSpec((ATT_WIDTH, tn), lambda i, j: (0, j)),
        ],
        out_specs=pl.BlockSpec((tm, tn), lambda i, j: (i, j)),
        out_shape=jax.ShapeDtypeStruct((m, d), bf16),
        compiler_params=_cparams(("parallel", "arbitrary")),
        name="merge",
    )(o_rw, o_att, c, c, p_rw, p_att)


def _outproj_body(mg_ref, x_ref, w_ref, o_ref):
    o_ref[...] = x_ref[...] + jnp.dot(mg_ref[...], w_ref[...], preferred_element_type=f32)


def _outproj(merged, x2d, w_o):
    m, d = x2d.shape
    tm = min(512, m)
    tn = 1024
    return pl.pallas_call(
        _outproj_body,
        grid=(m // tm, d // tn),
        in_specs=[
            pl.BlockSpec((tm, d), lambda i, j: (i, 0)),
            pl.BlockSpec((tm, tn), lambda i, j: (i, j)),
            pl.BlockSpec((d, tn), lambda i, j: (0, j)),
        ],
        out_specs=pl.BlockSpec((tm, tn), lambda i, j: (i, j)),
        out_shape=jax.ShapeDtypeStruct((m, d), f32),
        compiler_params=_cparams(("parallel", "arbitrary")),
        name="out_proj",
    )(merged, x2d, w_o)


def _convglu_body(shift, tiles_per_seq, final_norm, h_ref, g2_ref, gf_ref, cprev_ref, wg_ref, wu_ref, cw_ref,
                  cb_ref, wd_ref, o_ref, tail_ref, n_ref, acc_ref, ext_ref, carry_ref):
    i = pl.program_id(0)
    j = pl.program_id(1)
    nj = pl.num_programs(1)
    tm = h_ref.shape[0]
    hist = 2 * shift
    base = ext_ref.shape[0] - tm

    @pl.when(j == 0)
    def _():
        h = h_ref[...]
        ms = jnp.mean(h * h, axis=-1, keepdims=True)
        n_ref[...] = (h * lax.rsqrt(ms + RMS_EPS) * g2_ref[...]).astype(bf16)
        acc_ref[...] = jnp.zeros_like(acc_ref)

    n = n_ref[...]
    gate = jnp.dot(n, wg_ref[...], preferred_element_type=f32)
    up = jnp.dot(n, wu_ref[...], preferred_element_type=f32)

    @pl.when(i % tiles_per_seq == 0)
    def _():
        ext_ref[base - hist:base, :] = cprev_ref[0]

    @pl.when(i % tiles_per_seq != 0)
    def _():
        ext_ref[base - hist:base, :] = carry_ref[j]

    ext_ref[base:, :] = gate
    cw = cw_ref[...]
    c = (cb_ref[...] + ext_ref[base - hist:base - hist + tm, :] * cw[0:1, :]
         + ext_ref[base - shift:base - shift + tm, :] * cw[1:2, :] + gate * cw[2:3, :])
    tail = ext_ref[base + tm - hist:base + tm, :]
    tail_ref[0] = tail
    carry_ref[j] = tail
    act = (c * jax.nn.sigmoid(c)) * up
    acc_ref[...] += jnp.dot(act.astype(bf16), wd_ref[...], preferred_element_type=f32)

    @pl.when(j == nj - 1)
    def _():
        out = h_ref[...] + acc_ref[...]
        if final_norm:
            ms = jnp.mean(out * out, axis=-1, keepdims=True)
            out = out * lax.rsqrt(ms + RMS_EPS) * gf_ref[...]
        o_ref[...] = out


def _convglu(h2d, conv_prev, norm2, norm_f, w_up_b, conv_w, conv_b, w_down_b, nseq_groups, shift, final_norm):
    m, d = h2d.shape
    d_ff = w_down_b.shape[0]
    rows_per_group = m // nseq_groups
    tm = min(512, rows_per_group)
    tf = 512
    nj = d_ff // tf
    tiles_per_seq = rows_per_group // tm
    hist = 2 * shift
    base = ((hist + 7) // 8) * 8
    out, tails = pl.pallas_call(
        functools.partial(_convglu_body, shift, tiles_per_seq, final_norm),
        grid=(m // tm, nj),
        in_specs=[
            pl.BlockSpec((tm, d), lambda i, j: (i, 0)),
            pl.BlockSpec((1, d), lambda i, j: (0, 0)),
            pl.BlockSpec((1, d), lambda i, j: (0, 0)),
            pl.BlockSpec((1, hist, tf), lambda i, j: (i // tiles_per_seq, 0, j)),
            pl.BlockSpec((d, tf), lambda i, j: (0, j)),
            pl.BlockSpec((d, tf), lambda i, j: (0, nj + j)),
            pl.BlockSpec((CONV_W, tf), lambda i, j: (0, j)),
            pl.BlockSpec((1, tf), lambda i, j: (0, j)),
            pl.BlockSpec((tf, d), lambda i, j: (j, 0)),
        ],
        out_specs=[
            pl.BlockSpec((tm, d), lambda i, j: (i, 0)),
            pl.BlockSpec((1, hist, tf), lambda i, j: (i, 0, j)),
        ],
        out_shape=[jax.ShapeDtypeStruct((m, d), f32), jax.ShapeDtypeStruct((m // tm, hist, d_ff), f32)],
        scratch_shapes=[pltpu.VMEM((tm, d), bf16), pltpu.VMEM((tm, d), f32), pltpu.VMEM((base + tm, tf), f32),
                        pltpu.VMEM((nj, hist, tf), f32)],
        compiler_params=_cparams(("arbitrary", "arbitrary")),
        name="convglu",
    )(h2d, norm2, norm_f, conv_prev, w_up_b, w_up_b, conv_w, conv_b, w_down_b)
    return out, tails[tiles_per_seq - 1::tiles_per_seq]


def _pad_cols(a, width):
    return jnp.pad(a, [(0, 0)] * (a.ndim - 1) + [(0, width - a.shape[-1])])


def _to_layout(a):
    rw = 3 * RW_WIDTH
    o = {}
    o["r"], o["k"], o["v"] = a[..., 0:RW_WIDTH], a[..., RW_WIDTH:2 * RW_WIDTH], a[..., 2 * RW_WIDTH:rw]
    p = rw
    o["wl"] = a[..., p:p + D_DECAY_LORA]; p += D_DECAY_LORA
    o["al"] = a[..., p:p + D_AAA_LORA]; p += D_AAA_LORA
    o["gl"] = a[..., p:p + D_GATE_LORA]; p += D_GATE_LORA
    o["q"] = a[..., p:p + ATT_WIDTH]; p += ATT_WIDTH
    o["ak"] = a[..., p:p + KV_WIDTH]; p += KV_WIDTH
    o["av"] = a[..., p:p + KV_WIDTH]; p += KV_WIDTH
    o["qi"] = a[..., p:p + IDX_HEADS * IDX_DIM]; p += IDX_HEADS * IDX_DIM
    o["ki"] = a[..., p:p + IDX_DIM]; p += IDX_DIM
    o["wi"] = a[..., p:p + IDX_HEADS]; p += IDX_HEADS
    d = (a.shape[-1] - p) // 2
    o["grw"], o["gatt"] = a[..., p:p + d], a[..., p + d:p + 2 * d]
    return jnp.concatenate([
        o["r"], o["k"], o["v"], o["q"], o["grw"], o["gatt"], o["qi"],
        _pad_cols(o["wl"], LANES), _pad_cols(o["al"], LANES), o["gl"],
        _pad_cols(o["ki"], LANES), o["ak"], o["av"], _pad_cols(o["wi"], LANES)], axis=-1)


def _rw_cols_layout(a):
    rw = 3 * RW_WIDTH
    wl = a[..., rw:rw + D_DECAY_LORA]
    al = a[..., rw + D_DECAY_LORA:rw + D_DECAY_LORA + D_AAA_LORA]
    gl = a[..., rw + D_DECAY_LORA + D_AAA_LORA:]
    return a[..., :rw], jnp.concatenate([_pad_cols(wl, LANES), _pad_cols(al, LANES), gl], axis=-1)


def _rw_cols_from_layout(c):
    return jnp.concatenate([c[..., :3 * RW_WIDTH], c[..., C_LORA:C_LORA + D_DECAY_LORA],
                            c[..., C_LORA + LANES:C_LORA + LANES + D_AAA_LORA],
                            c[..., C_LORA + 2 * LANES:C_LORA + 3 * LANES]], axis=-1)


def _rope_tables(pos, rows):
    pos = jnp.pad(pos.astype(f32), (0, rows - pos.shape[0]))
    out = []
    for dim in (HEAD_DIM, IDX_DIM):
        half = dim // 2
        inv_freq = 1.0 / (ROPE_THETA ** (jnp.arange(half, dtype=f32) / half))
        ang = pos[:, None] * inv_freq[None, :]
        cos, sin = jnp.cos(ang), jnp.sin(ang)
        reps = LANES // dim
        out.append(jnp.tile(jnp.concatenate([cos, cos], axis=1), (1, reps)))
        out.append(jnp.tile(jnp.concatenate([-sin, sin], axis=1), (1, reps)))
    return tuple(out)


def _pair_blockdiag(s):
    n = s.shape[0]
    s = s.reshape(n, RW_HEADS // 2, 2, RW_HEAD, RW_HEAD)
    z = jnp.zeros_like(s[:, :, 0])
    top = jnp.concatenate([s[:, :, 0], z], axis=-1)
    bot = jnp.concatenate([z, s[:, :, 1]], axis=-1)
    return jnp.concatenate([top, bot], axis=-2)


def _pair_unblock(s):
    n = s.shape[0]
    a = s[:, :, :RW_HEAD, :RW_HEAD]
    b = s[:, :, RW_HEAD:, RW_HEAD:]
    return jnp.stack([a, b], axis=2).reshape(n, RW_HEADS, RW_HEAD, RW_HEAD)


def kernel(x_prompt, x_sample, cache_k, cache_v, cache_kidx, state_wkv, state_shift, state_conv, page_table, norm1, w_in, rw_mu, rw_w0, rw_w2, rw_a0, rw_a2, rw_g2, rw_k_k, rw_k_a, rw_r_k, rw_lnx_w, rw_lnx_b, p_rw, p_att, w_o, norm2, w_up, conv_w, conv_b, w_down, norm_f):
    B, S, D = x_prompt.shape
    DB, DS, _ = x_sample.shape
    depth = w_in.shape[0]
    page = cache_k.shape[2]
    n_pages = page_table.shape[1]
    past_len = n_pages * page
    d_ff = w_down.shape[1]
    dt = x_prompt.dtype

    tabs_p = _rope_tables(jnp.arange(S), S)
    tabs_s = _rope_tables(past_len + jnp.arange(DS), ROWS)
    row2 = lambda a: a.reshape(1, -1)

    hp = x_prompt.reshape(B * S, D)
    hs = x_sample.reshape(DB * DS, D)
    outs = {k: [] for k in ("kp", "vp", "kip", "ks", "vs", "kis", "wkvp", "wkvs", "shp", "shs", "cvp", "cvs")}
    for l in range(depth):
        w_in_p = _to_layout(w_in[l]).astype(bf16)
        mu_rkv, mu_lora = _rw_cols_layout(rw_mu[l][None, :])
        params = dict(
            mu_rkv=mu_rkv, mu_lora=mu_lora, w0=row2(rw_w0[l]),
            w2=jnp.pad(rw_w2[l], ((0, LANES - D_DECAY_LORA), (0, 0))).astype(bf16), a0=row2(rw_a0[l]),
            a2=jnp.pad(rw_a2[l], ((0, LANES - D_AAA_LORA), (0, 0))).astype(bf16), g2=rw_g2[l].astype(bf16),
            k_k=row2(rw_k_k[l]), k_a=row2(rw_k_a[l]), r_k=row2(rw_r_k[l]), lnx_w=row2(rw_lnx_w[l]),
            lnx_b=row2(rw_lnx_b[l]))
        last = l == depth - 1

        c_p = _in_proj(hp, row2(norm1[l]), w_in_p)
        c_s = _in_proj(hs, row2(norm1[l]), w_in_p)

        sp_rkv = jnp.zeros((B, 1, RKV_W), dt)
        sp_lora = jnp.zeros((B, 1, LORA_W), dt)
        s0_p = jnp.zeros((B, RW_HEADS // 2, LANES, LANES), dt)
        o_rw_p, wkv_p = _rwkv_prompt(c_p, sp_rkv, sp_lora, s0_p, params, B, S)

        c_s3 = c_s.reshape(DB, DS, NP_COLS)
        ss_rkv, ss_lora = _rw_cols_layout(state_shift[l])
        prev_rkv = jnp.concatenate([ss_rkv[:, None, :], c_s3[:, :-1, :RKV_W]], axis=1).reshape(DB * DS, RKV_W)
        prev_lora = jnp.concatenate([ss_lora[:, None, :], c_s3[:, :-1, C_LORA:C_LORA + LORA_W]], axis=1)
        prev_lora = prev_lora.reshape(DB * DS, LORA_W)
        r_s, lw_s, k_s, v_s, a_s, b_s, g_s = _rwkv_prep(c_s, prev_rkv, prev_lora, params)
        y_s, wkv_s = _rwkv_seq((r_s, lw_s, k_s, v_s, a_s, b_s), _pair_blockdiag(state_wkv[l]), DB, DS)
        o_rw_s = _rwkv_post(y_s.reshape(DB * DS, RW_WIDTH), r_s, k_s, v_s, g_s, params)

        o_att_p, k_p, ki_p = _dsa_prompt(c_p, tabs_p, B, S)

        def rows8(lo, w):
            return jnp.pad(c_s3[:, :, lo:lo + w], ((0, 0), (0, ROWS - DS), (0, 0)))

        csel = (rows8(C_Q, ATT_WIDTH), rows8(C_QI, IDX_HEADS * IDX_DIM), rows8(C_AK, KV_WIDTH),
                rows8(C_AV, KV_WIDTH), rows8(C_KI, LANES))
        wi_s = jnp.pad(c_s3[:, :, C_WI:C_WI + IDX_HEADS], ((0, 0), (0, ROWS - DS), (0, 0)))
        wrow = jnp.transpose(wi_s, (0, 2, 1)).reshape(DB, 1, IDX_HEADS * ROWS)
        o_att_s8, k_s8, ki_s8 = _dsa_sample(
            csel, wrow, tabs_s, page_table, cache_k[l].reshape(-1, page * KV_HEADS, HEAD_DIM),
            cache_v[l].reshape(-1, page * KV_HEADS, HEAD_DIM), cache_kidx[l], DS)
        o_att_s = o_att_s8[:, :DS].reshape(DB * DS, ATT_WIDTH)

        p_rw_b, p_att_b, w_o_b = p_rw[l].astype(bf16), p_att[l].astype(bf16), w_o[l].astype(bf16)
        h_p = _outproj(_merge(o_rw_p, o_att_p, c_p, p_rw_b, p_att_b), hp, w_o_b)
        h_s = _outproj(_merge(o_rw_s, o_att_s, c_s, p_rw_b, p_att_b), hs, w_o_b)

        w_up_b, w_down_b = w_up[l].astype(bf16), w_down[l].astype(bf16)
        cv_args = (row2(norm2[l]), row2(norm_f), w_up_b, conv_w[l], row2(conv_b[l]), w_down_b)
        hp, tail_p = _convglu(h_p, jnp.zeros((B, CONV_W - 1, d_ff), dt), *cv_args, B, 1, last)
        h_s_tm = h_s.reshape(DB, DS, D).transpose(1, 0, 2).reshape(DS * DB, D)
        cprev_tm = state_conv[l].transpose(1, 0, 2).reshape(1, (CONV_W - 1) * DB, d_ff)
        hs_tm, tail_s = _convglu(h_s_tm, cprev_tm, *cv_args, 1, DB, last)
        hs = hs_tm.reshape(DS, DB, D).transpose(1, 0, 2).reshape(DB * DS, D)

        outs["kp"].append(k_p.reshape(B, S // page, page, KV_HEADS, HEAD_DIM))
        outs["vp"].append(c_p[:, C_AV:C_AV + KV_WIDTH].reshape(B, S // page, page, KV_HEADS, HEAD_DIM))
        outs["kip"].append(ki_p.reshape(B, S // page, page, IDX_DIM))
        outs["ks"].append(k_s8[:, :DS].reshape(DB, DS, KV_HEADS, HEAD_DIM))
        outs["vs"].append(c_s3[:, :, C_AV:C_AV + KV_WIDTH].reshape(DB, DS, KV_HEADS, HEAD_DIM))
        outs["kis"].append(ki_s8[:, :DS])
        outs["wkvp"].append(_pair_unblock(wkv_p))
        outs["wkvs"].append(_pair_unblock(wkv_s))
        outs["shp"].append(_rw_cols_from_layout(c_p.reshape(B, S, NP_COLS)[:, -1]))
        outs["shs"].append(_rw_cols_from_layout(c_s3[:, -1]))
        outs["cvp"].append(tail_p)
        outs["cvs"].append(tail_s.reshape(CONV_W - 1, DB, d_ff).transpose(1, 0, 2))

    y_prompt = hp.reshape(B, S, D)
    y_sample = hs.reshape(DB, DS, D)
    st = lambda k: jnp.stack(outs[k])
    return (y_prompt, y_sample, st("kp"), st("vp"), st("kip"), st("ks"), st("vs"), st("kis"), st("wkvp"),
            st("wkvs"), st("shp"), st("shs"), st("cvp"), st("cvs"))
```

```python
import functools
import math

import numpy as np
import jax
import jax.numpy as jnp
from jax import lax
from jax.experimental import pallas as pl
from jax.experimental.pallas import tpu as pltpu

f32 = jnp.float32
bf16 = jnp.bfloat16
i32 = jnp.int32

RW_HEADS = 16
RW_HEAD = 64
RW_WIDTH = RW_HEADS * RW_HEAD
D_DECAY_LORA = 96
D_AAA_LORA = 96
D_GATE_LORA = 128
GN_EPS = 64e-5
ATT_HEADS = 8
KV_HEADS = 2
HEAD_DIM = 128
ATT_WIDTH = ATT_HEADS * HEAD_DIM
KV_WIDTH = KV_HEADS * HEAD_DIM
IDX_HEADS = 16
IDX_DIM = 64
IDX_W_SCALE = (IDX_HEADS * IDX_DIM) ** -0.5
TOPK_MAX = 256
Q_BLOCK = 128
ROPE_THETA = 10000.0
CONV_W = 3
RMS_EPS = 1e-6

LANES = 128
NEG_BIG = -1e30
INT_MIN = -(2 ** 31)

C_R, C_K, C_V, C_Q, C_GRW, C_GATT, C_QI = 0, 1024, 2048, 3072, 4096, 6144, 8192
C_LORA = 9216
C_KI, C_AK, C_AV, C_WI = 9600, 9728, 9984, 10240
NP_COLS = 10368
LORA_W = 384
RKV_W = 3 * RW_WIDTH

VMEM_LIMIT = 56 * 1024 * 1024


def _cparams(sem):
    return pltpu.CompilerParams(dimension_semantics=sem, vmem_limit_bytes=VMEM_LIMIT)


def _bdot(a, b):
    return jnp.dot(a.astype(bf16), b.astype(bf16), preferred_element_type=f32)


def _bdot_nt(a, b):
    return lax.dot_general(a.astype(bf16), b.astype(bf16), (((1,), (1,)), ((), ())), preferred_element_type=f32)


def _bdot_tn(a, b):
    return lax.dot_general(a.astype(bf16), b.astype(bf16), (((0,), (0,)), ((), ())), preferred_element_type=f32)


def _split3(x):
    x1 = x.astype(bf16)
    r1 = x - x1.astype(f32)
    x2 = r1.astype(bf16)
    x3 = (r1 - x2.astype(f32)).astype(bf16)
    return x1, x2, x3


def _split2(x):
    x1 = x.astype(bf16)
    return x1, (x - x1.astype(f32)).astype(bf16)


def _headsum(x):
    ri = lax.broadcasted_iota(i32, (LANES, LANES), 0) // RW_HEAD
    ci = lax.broadcasted_iota(i32, (LANES, LANES), 1) // RW_HEAD
    bd = (ri == ci).astype(bf16)
    outs = []
    for i in range(x.shape[1] // LANES):
        hi, lo = _split2(x[:, i * LANES:(i + 1) * LANES])
        outs.append(jnp.dot(hi, bd, preferred_element_type=f32) + jnp.dot(lo, bd, preferred_element_type=f32))
    return jnp.concatenate(outs, axis=1)


def _softplus(x):
    return jnp.maximum(x, 0.0) + jnp.log(1.0 + jnp.exp(-jnp.abs(x)))


def _rope(x, cos, sin, half):
    w = x.shape[1]
    reps = w // LANES
    if reps > 1:
        cos = jnp.concatenate([cos] * reps, axis=1)
        sin = jnp.concatenate([sin] * reps, axis=1)
    if 2 * half == LANES and w == LANES:
        partner = pltpu.roll(x, half, axis=1)
    else:
        lane = lax.broadcasted_iota(i32, (1, w), 1)
        first = (lane % (2 * half)) < half
        partner = jnp.where(first, pltpu.roll(x, w - half, axis=1), pltpu.roll(x, half, axis=1))
    return x * cos + partner * sin


def _inproj_body(x_ref, g_ref, w_ref, o_ref, n_ref):
    @pl.when(pl.program_id(1) == 0)
    def _():
        x = x_ref[...]
        ms = jnp.mean(x * x, axis=-1, keepdims=True)
        n_ref[...] = (x * lax.rsqrt(ms + RMS_EPS) * g_ref[...]).astype(bf16)

    o_ref[...] = jnp.dot(n_ref[...], w_ref[...], preferred_element_type=f32)


def _in_proj(x2d, gain, w_p):
    m, d = x2d.shape
    tm = min(512, m)
    tn = NP_COLS // 9
    return pl.pallas_call(
        _inproj_body,
        grid=(m // tm, NP_COLS // tn),
        in_specs=[
            pl.BlockSpec((tm, d), lambda i, j: (i, 0)),
            pl.BlockSpec((1, d), lambda i, j: (0, 0)),
            pl.BlockSpec((d, tn), lambda i, j: (0, j)),
        ],
        out_specs=pl.BlockSpec((tm, tn), lambda i, j: (i, j)),
        out_shape=jax.ShapeDtypeStruct((m, NP_COLS), f32),
        scratch_shapes=[pltpu.VMEM((tm, d), bf16)],
        compiler_params=_cparams(("parallel", "arbitrary")),
        name="in_proj",
    )(x2d, gain, w_p)


RW_PARAM_NAMES = ("mu_rkv", "mu_lora", "w0", "w2", "a0", "a2", "g2", "k_k", "k_a", "r_k", "lnx_w", "lnx_b")


def _rw_prep(c_rkv, c_lora, p_rkv, p_lora, P):
    m = c_rkv + (p_rkv - c_rkv) * P["mu_rkv"]
    ml = c_lora + (p_lora - c_lora) * P["mu_lora"]
    r, k, v = m[:, :RW_WIDTH], m[:, RW_WIDTH:2 * RW_WIDTH], m[:, 2 * RW_WIDTH:]
    wl, al, gl = ml[:, :LANES], ml[:, LANES:2 * LANES], ml[:, 2 * LANES:]
    w_log = -_softplus(-(P["w0"] + _bdot(jnp.tanh(wl), P["w2"]))) - 0.5
    logw = -jnp.exp(w_log)
    asig = jax.nn.sigmoid(P["a0"] + _bdot(al, P["a2"]))
    g = _bdot(jax.nn.sigmoid(gl), P["g2"])
    kk = k * P["k_k"]
    kkn = kk * lax.rsqrt(jnp.maximum(_headsum(kk * kk), 1e-24))
    k2 = k * (1.0 + (asig - 1.0) * P["k_a"])
    return r, logw, k2, v, -kkn, kkn * asig, g


def _rw_post(y, r, k2, v, g, P):
    inv_n = 1.0 / RW_HEAD
    mean = _headsum(y) * inv_n
    d = y - mean
    var = _headsum(d * d) * inv_n
    yn = d * lax.rsqrt(var + GN_EPS) * P["lnx_w"] + P["lnx_b"]
    bonus = _headsum(r * k2 * P["r_k"]) * v
    return (yn + bonus) * g


def _chunk_step(at, rt, bt, kt, be, ke, v, wc, S, masks, nsteps):
    keep, eye2, colh, mA, bdmask = masks
    R = range(len(at))
    C = at[0].shape[0]
    X = [jnp.concatenate([at[p], rt[p]], axis=0) for p in R]
    scA = [jnp.where(keep, _bdot_nt(jnp.where(mA, X[p], 0.0), jnp.concatenate([bt[p], kt[p]], axis=0)), 0.0)
           for p in R]
    scB = [jnp.where(keep, _bdot_nt(jnp.where(mA, 0.0, X[p]), jnp.concatenate([kt[p], bt[p]], axis=0)), 0.0)
           for p in R]
    XS = [_bdot_nt(X[p], S[p]) for p in R]
    Lp = [jnp.concatenate([jnp.where(colh, scA[p][:C], 0.0), jnp.where(colh, 0.0, scB[p][:C])], axis=0) for p in R]
    T = [eye2 + Lp[p] for p in R]
    if nsteps > 0:
        Pw = [_bdot(Lp[p], Lp[p]) for p in R]
        for i in range(nsteps):
            if i < nsteps - 1:
                Z = [_bdot(Pw[p], jnp.concatenate([T[p], Pw[p]], axis=1)) for p in R]
                T = [T[p] + Z[p][:, :2 * C] for p in R]
                Pw = [Z[p][:, 2 * C:] for p in R]
            else:
                T = [T[p] + _bdot(Pw[p], T[p]) for p in R]
    vA = [jnp.where(mA, v[p], 0.0) for p in R]
    vB = [jnp.where(mA, 0.0, v[p]) for p in R]
    G = [XS[p][:C] + _bdot(jnp.where(colh, scB[p][:C], scA[p][:C]), jnp.concatenate([vB[p], vA[p]], axis=0))
         for p in R]
    U = [_bdot(jnp.where(colh, T[p][:C], T[p][C:]),
               jnp.concatenate([jnp.where(mA, G[p], 0.0), jnp.where(mA, 0.0, G[p])], axis=0)) for p in R]
    Y = [XS[p][C:] + _bdot(jnp.concatenate([scA[p][C:], scB[p][C:]], axis=1),
                           jnp.concatenate([jnp.where(mA, U[p], 0.0), vA[p], vB[p], jnp.where(mA, 0.0, U[p])], axis=0))
         for p in R]
    dS = [_bdot_tn(jnp.concatenate([U[p], v[p]], axis=0), jnp.concatenate([be[p], ke[p]], axis=0)) for p in R]
    S_new = [S[p] * wc[p] + jnp.where(bdmask, dS[p], 0.0) for p in R]
    return Y, S_new


def _chunk_masks(C):
    row = lax.broadcasted_iota(i32, (2 * C, 2 * C), 0)
    col = lax.broadcasted_iota(i32, (2 * C, 2 * C), 1)
    t = jnp.where(row >= C, row - C, row)
    s = jnp.where(col >= C, col - C, col)
    keep = (s < t) | ((row >= C) & (s == t))
    eye2 = (row == col).astype(f32)
    colh = lax.broadcasted_iota(i32, (C, 2 * C), 1) < C
    mA = lax.broadcasted_iota(i32, (1, LANES), 1) < RW_HEAD
    r2 = lax.broadcasted_iota(i32, (LANES, LANES), 0) // RW_HEAD
    c2 = lax.broadcasted_iota(i32, (LANES, LANES), 1) // RW_HEAD
    return keep, eye2, colh, mA, r2 == c2


def _rwkv_prompt_body(chunk, rkv_ref, lora_ref, sp_rkv_ref, sp_lora_ref, s0_ref, *rest):
    np_ = len(RW_PARAM_NAMES)
    P = {n: rest[i][...] for i, n in enumerate(RW_PARAM_NAMES)}
    o_ref, sout_ref = rest[np_], rest[np_ + 1]
    (S_ref, car_rkv, car_lora, at_s, rt_s, bt_s, kt_s, be_s, ke_s, v_s, cum_s, y_s) = rest[np_ + 2:]
    t = pl.program_id(1)
    tt = rkv_ref.shape[0]
    C = chunk

    @pl.when(t == 0)
    def _():
        S_ref[...] = s0_ref[0]
        car_rkv[...] = sp_rkv_ref[0]
        car_lora[...] = sp_lora_ref[0]

    c_rkv = rkv_ref[...]
    c_lora = lora_ref[...]
    first = lax.broadcasted_iota(i32, (tt, 1), 0) == 0
    p_rkv = jnp.where(first, car_rkv[...], pltpu.roll(c_rkv, 1, axis=0))
    p_lora = jnp.where(first, car_lora[...], pltpu.roll(c_lora, 1, axis=0))
    car_rkv[...] = c_rkv[tt - 1:tt, :]
    car_lora[...] = c_lora[tt - 1:tt, :]
    r, logw, k2, v, a, b, g = _rw_prep(c_rkv, c_lora, p_rkv, p_lora, P)

    ri = lax.broadcasted_iota(i32, (tt, tt), 0)
    ci = lax.broadcasted_iota(i32, (tt, tt), 1)
    same = ri // C == ci // C
    tril = (same & (ci <= ri)).astype(bf16)
    triu = (same & (ci > ri)).astype(bf16)
    pieces = _split3(logw)
    cum = sum(jnp.dot(tril, piece, preferred_element_type=f32) for piece in pieces)
    rev = sum(jnp.dot(triu, piece, preferred_element_type=f32) for piece in pieces)
    iw = jnp.exp(-cum)
    ew = jnp.exp(rev)
    at_s[...] = a * jnp.exp(cum - logw)
    rt_s[...] = r * jnp.exp(cum)
    bt_s[...] = b * iw
    kt_s[...] = k2 * iw
    be_s[...] = b * ew
    ke_s[...] = k2 * ew
    v_s[...] = v
    cum_s[...] = cum
    masks = _chunk_masks(C)
    nsteps = max(int(math.ceil(math.log2(C))) - 1, 0)
    pairs = range(RW_WIDTH // LANES)

    def chunk_body(ci_, carry):
        r0 = pl.multiple_of(ci_ * C, C)
        lanes = [slice(p * LANES, (p + 1) * LANES) for p in pairs]
        ld = lambda ref: [ref[pl.ds(r0, C), lanes[p]] for p in pairs]
        last8 = pl.multiple_of(r0 + C - 8, 8)
        wc = [jnp.exp(cum_s[pl.ds(last8, 8), lanes[p]][7:8]) for p in pairs]
        Y, S_new = _chunk_step(ld(at_s), ld(rt_s), ld(bt_s), ld(kt_s), ld(be_s), ld(ke_s), ld(v_s), wc,
                               [S_ref[p] for p in pairs], masks, nsteps)
        for p in pairs:
            y_s[pl.ds(r0, C), lanes[p]] = Y[p]
            S_ref[p] = S_new[p]
        return carry

    lax.fori_loop(0, tt // C, chunk_body, 0)
    o_ref[...] = _rw_post(y_s[...], r, k2, v, g, P)

    @pl.when(t == pl.num_programs(1) - 1)
    def _():
        sout_ref[0] = S_ref[...]


def _const_spec(shape):
    nd = len(shape)
    return pl.BlockSpec(shape, lambda *_: (0,) * nd)


def _rwkv_prompt(c, sp_rkv, sp_lora, s0, params, nb, seq):
    tt = min(256, seq)
    chunk = min(64, tt)
    nt = seq // tt
    npairs = RW_WIDTH // LANES
    in_specs = [
        pl.BlockSpec((tt, RKV_W), lambda b, t: (b * nt + t, 0)),
        pl.BlockSpec((tt, LORA_W), lambda b, t: (b * nt + t, C_LORA // LORA_W)),
        pl.BlockSpec((1, 1, RKV_W), lambda b, t: (b, 0, 0)),
        pl.BlockSpec((1, 1, LORA_W), lambda b, t: (b, 0, 0)),
        pl.BlockSpec((1, npairs, LANES, LANES), lambda b, t: (b, 0, 0, 0)),
    ] + [_const_spec(params[n].shape) for n in RW_PARAM_NAMES]
    out_specs = [
        pl.BlockSpec((tt, RW_WIDTH), lambda b, t: (b * nt + t, 0)),
        pl.BlockSpec((1, npairs, LANES, LANES), lambda b, t: (b, 0, 0, 0)),
    ]
    scratch = [pltpu.VMEM((npairs, LANES, LANES), f32), pltpu.VMEM((1, RKV_W), f32), pltpu.VMEM((1, LORA_W), f32)]
    scratch += [pltpu.VMEM((tt, RW_WIDTH), f32) for _ in range(9)]
    return pl.pallas_call(
        functools.partial(_rwkv_prompt_body, chunk),
        grid=(nb, nt),
        in_specs=in_specs,
        out_specs=out_specs,
        out_shape=[jax.ShapeDtypeStruct((nb * seq, RW_WIDTH), f32),
                   jax.ShapeDtypeStruct((nb, npairs, LANES, LANES), f32)],
        scratch_shapes=scratch,
        compiler_params=_cparams(("parallel", "arbitrary")),
        name="rwkv_prompt",
    )(c, c, sp_rkv, sp_lora, s0, *[params[n] for n in RW_PARAM_NAMES])


def _rwkv_prep_body(rkv_ref, lora_ref, prkv_ref, plora_ref, *rest):
    np_ = len(RW_PARAM_NAMES)
    P = {n: rest[i][...] for i, n in enumerate(RW_PARAM_NAMES)}
    outs = rest[np_:]
    vals = _rw_prep(rkv_ref[...], lora_ref[...], prkv_ref[...], plora_ref[...], P)
    for o, v in zip(outs, vals):
        o[...] = v


def _rwkv_prep(c, prev_rkv, prev_lora, params):
    m = c.shape[0]
    tm = min(512, m)
    in_specs = [
        pl.BlockSpec((tm, RKV_W), lambda i: (i, 0)),
        pl.BlockSpec((tm, LORA_W), lambda i: (i, C_LORA // LORA_W)),
        pl.BlockSpec((tm, RKV_W), lambda i: (i, 0)),
        pl.BlockSpec((tm, LORA_W), lambda i: (i, 0)),
    ] + [_const_spec(params[n].shape) for n in RW_PARAM_NAMES]
    return pl.pallas_call(
        _rwkv_prep_body,
        grid=(m // tm,),
        in_specs=in_specs,
        out_specs=[pl.BlockSpec((tm, RW_WIDTH), lambda i: (i, 0)) for _ in range(7)],
        out_shape=[jax.ShapeDtypeStruct((m, RW_WIDTH), f32) for _ in range(7)],
        compiler_params=_cparams(("parallel",)),
        name="rwkv_prep",
    )(c, c, prev_rkv, prev_lora, *[params[n] for n in RW_PARAM_NAMES])


def _rwkv_seq_body(r_ref, lw_ref, k_ref, v_ref, a_ref, b_ref, s_ref, y_ref, sout_ref):
    steps = r_ref.shape[1]
    eye = (lax.broadcasted_iota(i32, (LANES, LANES), 0) == lax.broadcasted_iota(i32, (LANES, LANES), 1)).astype(f32)
    bd = (lax.broadcasted_iota(i32, (LANES, LANES), 0) // RW_HEAD) == (
        lax.broadcasted_iota(i32, (LANES, LANES), 1) // RW_HEAD)
    pairs = range(RW_WIDTH // LANES)
    z = jnp.zeros((RW_HEAD, RW_HEAD), f32)
    S = [jnp.concatenate([jnp.concatenate([s_ref[0, 2 * p], z], axis=1),
                          jnp.concatenate([z, s_ref[0, 2 * p + 1]], axis=1)], axis=0) for p in pairs]
    for t in range(steps):
        sl = [(0, slice(t, t + 1), slice(p * LANES, (p + 1) * LANES)) for p in pairs]
        sa = [jnp.sum(S[p] * a_ref[sl[p]], axis=1, keepdims=True) for p in pairs]
        v_col = [jnp.sum(eye * v_ref[sl[p]], axis=1, keepdims=True) for p in pairs]
        S = [S[p] * jnp.exp(lw_ref[sl[p]]) + jnp.where(bd, sa[p] * b_ref[sl[p]] + v_col[p] * k_ref[sl[p]], 0.0)
             for p in pairs]
        y_col = [jnp.sum(S[p] * r_ref[sl[p]], axis=1, keepdims=True) for p in pairs]
        for p in pairs:
            y_ref[sl[p]] = jnp.sum(eye * y_col[p], axis=0, keepdims=True)
    for p in pairs:
        sout_ref[0, 2 * p] = S[p][:RW_HEAD, :RW_HEAD]
        sout_ref[0, 2 * p + 1] = S[p][RW_HEAD:, RW_HEAD:]


def _rwkv_seq(ops, s0, nreq, steps):
    ops3 = [o.reshape(nreq, steps, RW_WIDTH) for o in ops]
    sspec = pl.BlockSpec((1, RW_HEADS, RW_HEAD, RW_HEAD), lambda b: (b, 0, 0, 0))
    return pl.pallas_call(
        _rwkv_seq_body,
        grid=(nreq,),
        in_specs=[pl.BlockSpec((1, steps, RW_WIDTH), lambda b: (b, 0, 0)) for _ in range(6)] + [sspec],
        out_specs=[pl.BlockSpec((1, steps, RW_WIDTH), lambda b: (b, 0, 0)), sspec],
        out_shape=[jax.ShapeDtypeStruct((nreq, steps, RW_WIDTH), f32),
                   jax.ShapeDtypeStruct((nreq, RW_HEADS, RW_HEAD, RW_HEAD), f32)],
        compiler_params=_cparams(("parallel",)),
        name="rwkv_seq",
    )(*ops3, s0)


def _rwkv_post_body(y_ref, r_ref, k_ref, v_ref, g_ref, *rest):
    np_ = len(RW_PARAM_NAMES)
    P = {n: rest[i][...] for i, n in enumerate(RW_PARAM_NAMES)}
    rest[np_][...] = _rw_post(y_ref[...], r_ref[...], k_ref[...], v_ref[...], g_ref[...], P)


def _rwkv_post(y, r, k2, v, g, params):
    m = y.shape[0]
    tm = min(512, m)
    return pl.pallas_call(
        _rwkv_post_body,
        grid=(m // tm,),
        in_specs=[pl.BlockSpec((tm, RW_WIDTH), lambda i: (i, 0)) for _ in range(5)]
        + [_const_spec(params[n].shape) for n in RW_PARAM_NAMES],
        out_specs=pl.BlockSpec((tm, RW_WIDTH), lambda i: (i, 0)),
        out_shape=jax.ShapeDtypeStruct((m, RW_WIDTH), f32),
        compiler_params=_cparams(("parallel",)),
        name="rwkv_post",
    )(y, r, k2, v, g, *[params[n] for n in RW_PARAM_NAMES])


def _sort_key(score):
    bits = pltpu.bitcast(score, i32)
    key = jnp.where(bits < 0, bits ^ jnp.int32(0x7FFFFFFF), bits)
    return jnp.where(score == 0.0, jnp.int32(0), key)


def _kth_largest(keys, k):
    def body(i, ts):
        bit = lax.shift_left(jnp.int32(1), jnp.int32(31) - i)
        cands = [t + bit for t in ts]
        cnts = [jnp.sum((key >= c).astype(f32), axis=1, keepdims=True) for key, c in zip(keys, cands)]
        return tuple(jnp.where(n >= k, c, t) for n, c, t in zip(cnts, cands, ts))

    init = tuple(jnp.full((key.shape[0], 1), INT_MIN, i32) for key in keys)
    return lax.fori_loop(0, 32, body, init, unroll=4)


def _row_groups(x, n):
    step = x.shape[0] // n
    return [x[g * step:(g + 1) * step] for g in range(n)]


def _dsa_prompt_body(topk, q_ref, qi_ref, wi_ref, k_ref, v_ref, ki_ref, cosk_ref, sink_ref, cosi_ref, sini_ref,
                     o_ref, kout_ref, kiout_ref, kb_ref, vb_ref, kib_ref, bias_ref):
    qb = pl.program_id(1)
    seq = k_ref.shape[0]
    nq = q_ref.shape[0]

    @pl.when(qb == 0)
    def _():
        cos, sin = cosk_ref[...], sink_ref[...]
        kr = jnp.concatenate([_rope(k_ref[:, h * HEAD_DIM:(h + 1) * HEAD_DIM], cos, sin, HEAD_DIM // 2)
                              for h in range(KV_HEADS)], axis=1)
        kout_ref[...] = kr
        kb_ref[...] = kr.astype(bf16)
        vb_ref[...] = v_ref[...].astype(bf16)
        kir = _rope(ki_ref[...], cosi_ref[...], sini_ref[...], IDX_DIM // 2)
        kiout_ref[...] = kir[:, :IDX_DIM]
        kib_ref[...] = (kir + pltpu.roll(kir, IDX_DIM, axis=1)).astype(bf16)

    r0 = pl.multiple_of(qb * nq, nq)
    cosq, sinq = cosk_ref[pl.ds(r0, nq), :], sink_ref[pl.ds(r0, nq), :]
    cosqi, sinqi = cosi_ref[pl.ds(r0, nq), :], sini_ref[pl.ds(r0, nq), :]
    q = _rope(q_ref[...], cosq, sinq, HEAD_DIM // 2).astype(bf16)
    qi = _rope(qi_ref[...], cosqi, sinqi, IDX_DIM // 2)
    wi = wi_ref[...] * IDX_W_SCALE
    lane = lax.broadcasted_iota(i32, (1, LANES), 1)
    scale = HEAD_DIM ** -0.5
    hpg = ATT_HEADS // KV_HEADS

    def process(ext):
        kib = kib_ref[:ext, :]
        sc = None
        for h in range(IDX_HEADS):
            pair = qi[:, (h // 2) * LANES:(h // 2 + 1) * LANES]
            mine = (lane < IDX_DIM) if h % 2 == 0 else (lane >= IDX_DIM)
            term = jnp.maximum(_bdot_nt(jnp.where(mine, pair, 0.0), kib), 0.0) * wi[:, h:h + 1]
            sc = term if sc is None else sc + term

        qpos = r0 + lax.broadcasted_iota(i32, (nq, 1), 0)
        kpos = lax.broadcasted_iota(i32, (1, ext), 1)
        causal = kpos <= qpos
        key = jnp.where(causal, _sort_key(sc), INT_MIN)
        thr = jnp.concatenate(_kth_largest(_row_groups(key, 4), float(topk)), axis=0)
        bias_ref[:, :ext] = jnp.where(causal & (key >= thr), 0.0, NEG_BIG)

        def qk(h):
            kg = kb_ref[:ext, (h // hpg) * HEAD_DIM:(h // hpg + 1) * HEAD_DIM]
            return _bdot_nt(q[:, h * HEAD_DIM:(h + 1) * HEAD_DIM], kg)

        s_next = qk(0)
        for h in range(ATT_HEADS):
            s = s_next * scale + bias_ref[:, :ext]
            if h + 1 < ATT_HEADS:
                s_next = qk(h + 1)
            m = jnp.max(s, axis=1, keepdims=True)
            p = jnp.exp(s - m)
            l = jnp.sum(p, axis=1, keepdims=True)
            vg = vb_ref[:ext, (h // hpg) * HEAD_DIM:(h // hpg + 1) * HEAD_DIM]
            o = jnp.dot(p.astype(bf16), vg, preferred_element_type=f32)
            o_ref[:, h * HEAD_DIM:(h + 1) * HEAD_DIM] = o / l

    nvar = 4 if seq // nq % 4 == 0 else 1
    per = seq // nq // nvar
    for var in range(nvar):
        pl.when(qb // per == var)(functools.partial(process, (var + 1) * per * nq))


def _dsa_prompt(c, tabs, nb, seq):
    nq = Q_BLOCK
    nblk = seq // nq
    topk = min(TOPK_MAX, seq // 4)
    cosk, sink, cosi, sini = tabs
    row = lambda b, j: b * nblk + j
    in_specs = [
        pl.BlockSpec((nq, ATT_WIDTH), lambda b, j: (row(b, j), C_Q // ATT_WIDTH)),
        pl.BlockSpec((nq, IDX_HEADS * IDX_DIM), lambda b, j: (row(b, j), C_QI // (IDX_HEADS * IDX_DIM))),
        pl.BlockSpec((nq, LANES), lambda b, j: (row(b, j), C_WI // LANES)),
        pl.BlockSpec((seq, KV_WIDTH), lambda b, j: (b, C_AK // KV_WIDTH)),
        pl.BlockSpec((seq, KV_WIDTH), lambda b, j: (b, C_AV // KV_WIDTH)),
        pl.BlockSpec((seq, LANES), lambda b, j: (b, C_KI // LANES)),
    ] + [_const_spec((seq, LANES)) for _ in range(4)]
    out_specs = [
        pl.BlockSpec((nq, ATT_WIDTH), lambda b, j: (row(b, j), 0)),
        pl.BlockSpec((seq, KV_WIDTH), lambda b, j: (b, 0)),
        pl.BlockSpec((seq, IDX_DIM), lambda b, j: (b, 0)),
    ]
    scratch = [pltpu.VMEM((seq, KV_WIDTH), bf16), pltpu.VMEM((seq, KV_WIDTH), bf16), pltpu.VMEM((seq, LANES), bf16),
               pltpu.VMEM((nq, seq), f32)]
    return pl.pallas_call(
        functools.partial(_dsa_prompt_body, topk),
        grid=(nb, nblk),
        in_specs=in_specs,
        out_specs=out_specs,
        out_shape=[jax.ShapeDtypeStruct((nb * seq, ATT_WIDTH), f32),
                   jax.ShapeDtypeStruct((nb * seq, KV_WIDTH), f32),
                   jax.ShapeDtypeStruct((nb * seq, IDX_DIM), f32)],
        scratch_shapes=scratch,
        compiler_params=_cparams(("parallel", "arbitrary")),
        name="dsa_prompt",
    )(c, c, c, c, c, c, cosk, sink, cosi, sini)


ROWS = 8


def _page_copies(pt_ref, req, n_pages, srcs_dsts_sems):
    out = []
    for p in range(n_pages):
        pg = pt_ref[req, p]
        for hbm, dst, sem in srcs_dsts_sems:
            out.append(pltpu.make_async_copy(hbm.at[pg], dst(p), sem))
    return out


def _dsa_select_body(topk, n_new, group, pt_ref, qi_ref, wrow_ref, kin_ref, cosi_ref, sini_ref, cki_hbm,
                     sel_ref, kiout_ref, kibuf, sems):
    i = pl.program_id(0)
    nsteps = pl.num_programs(0)
    n_pages = pt_ref.shape[1]
    page = cki_hbm.shape[1]
    past = n_pages * page
    G = group

    def copies(step, slot):
        out = []
        for j in range(G):
            out += _page_copies(pt_ref, step * G + j, n_pages,
                                [(cki_hbm, lambda p, j=j: kibuf.at[slot, j, pl.ds(p * page, page)], sems.at[slot])])
        return out

    slot = i % 2

    @pl.when(i == 0)
    def _():
        for cp in copies(0, 0):
            cp.start()

    @pl.when(i + 1 < nsteps)
    def _():
        for cp in copies(i + 1, 1 - slot):
            cp.start()

    cosi, sini = cosi_ref[...], sini_ref[...]
    li = lax.broadcasted_iota(i32, (LANES, LANES), 0)
    lo = lax.broadcasted_iota(i32, (LANES, LANES), 1)
    qis, kins, wrows, sels = [], [], [], []
    for j in range(G):
        qi = _rope(qi_ref[j], cosi, sini, IDX_DIM // 2)
        kin = _rope(kin_ref[j], cosi, sini, IDX_DIM // 2)[:, :IDX_DIM]
        kiout_ref[j] = kin
        kins.append(kin)
        qis.append(jnp.concatenate([qi[:, h * IDX_DIM:(h + 1) * IDX_DIM] for h in range(IDX_HEADS)], axis=0))
        wrows.append(wrow_ref[0, :, j * LANES:(j + 1) * LANES] * IDX_W_SCALE)
        sels.append((((li % ROWS) * G + j == lo) & (li % ROWS < n_new)).astype(bf16))

    def idx_scores(ki, j):
        hi, lw = _split2(jnp.maximum(_bdot_nt(ki, qis[j]), 0.0) * wrows[j])
        return (jnp.dot(hi, sels[j], preferred_element_type=f32) + jnp.dot(lw, sels[j], preferred_element_type=f32))

    sc_n = sum(idx_scores(kins[j], j) for j in range(G))

    for cp in copies(i, slot):
        cp.wait()

    sc_p = sum(idx_scores(kibuf[slot, j], j) for j in range(G))
    pad = jnp.zeros((LANES - ROWS, LANES), f32)
    sc = jnp.concatenate([sc_p, sc_n, pad], axis=0).T
    nrow = n_new * G
    sc = sc[:nrow]
    width = past + LANES
    t_row = lax.broadcasted_iota(i32, (nrow, width), 0) // G
    col = lax.broadcasted_iota(i32, (nrow, width), 1)
    valid = (col < past) | ((col - past <= t_row) & (col - past < n_new))
    key = jnp.where(valid, _sort_key(sc), INT_MIN)
    thr = jnp.concatenate(_kth_largest(_row_groups(key, nrow // ROWS), float(topk)), axis=0)
    chosen = (valid & (key >= thr)).astype(f32)
    chosen = jnp.concatenate([chosen, jnp.zeros((LANES - nrow, width), f32)], axis=0)
    sel_ref[0] = chosen.T


def _dsa_sample_body(n_new, group, pt_ref, q_ref, kn_ref, vn_ref, sel_ref, cosk_ref, sink_ref, ck_hbm, cv_hbm,
                     o_ref, kout_ref, kbuf, vbuf, sems):
    b = pl.program_id(0)
    nreq = pl.num_programs(0)
    n_pages = pt_ref.shape[1]
    prow = ck_hbm.shape[1]
    past = n_pages * prow // KV_HEADS
    G = group

    def copies(req, slot):
        return _page_copies(pt_ref, req, n_pages,
                            [(ck_hbm, lambda p: kbuf.at[slot, pl.ds(p * prow, prow)], sems.at[0, slot]),
                             (cv_hbm, lambda p: vbuf.at[slot, pl.ds(p * prow, prow)], sems.at[1, slot])])

    slot = b % 2

    @pl.when(b == 0)
    def _():
        for cp in copies(0, 0):
            cp.start()

    @pl.when(b + 1 < nreq)
    def _():
        for cp in copies(b + 1, 1 - slot):
            cp.start()

    cosk, sink = cosk_ref[...], sink_ref[...]
    q = _rope(q_ref[0], cosk, sink, HEAD_DIM // 2)
    kn = jnp.concatenate([_rope(kn_ref[0][:, h * HEAD_DIM:(h + 1) * HEAD_DIM], cosk, sink, HEAD_DIM // 2)
                          for h in range(KV_HEADS)], axis=1)
    vn = vn_ref[0]
    kout_ref[0] = kn

    li = lax.broadcasted_iota(i32, (LANES, LANES), 0)
    lo = lax.broadcasted_iota(i32, (LANES, LANES), 1)
    route = ((lo % ROWS) * G + b % G == li) & (lo % ROWS < n_new)
    chosen = jnp.dot(sel_ref[0].astype(bf16), route.astype(bf16), preferred_element_type=f32)
    bias = (chosen - 1.0) * (-NEG_BIG)
    bias_p, bias_n = bias[:past], bias[past:past + ROWS]

    eye = (li == lo).astype(f32)
    scale = HEAD_DIM ** -0.5
    hpg = ATT_HEADS // KV_HEADS
    zrows = jnp.zeros((LANES - hpg * ROWS, HEAD_DIM), f32)

    for cp in copies(b, slot):
        cp.wait()

    for g in range(KV_HEADS):
        qg = jnp.concatenate([q[:, (g * hpg + r) * HEAD_DIM:(g * hpg + r + 1) * HEAD_DIM] for r in range(hpg)]
                             + [zrows], axis=0)
        gs = slice(g * HEAD_DIM, (g + 1) * HEAD_DIM)
        kp = kbuf[slot, pl.ds(g, past, stride=KV_HEADS), :]
        vp = vbuf[slot, pl.ds(g, past, stride=KV_HEADS), :]
        s_p = _bdot_nt(kp, qg) * scale + bias_p
        s_n = _bdot_nt(kn[:, gs], qg) * scale + bias_n
        m = jnp.maximum(jnp.max(s_p, axis=0, keepdims=True), jnp.max(s_n, axis=0, keepdims=True))
        p_p = jnp.exp(s_p - m)
        p_n = jnp.exp(s_n - m)
        l = jnp.sum(p_p, axis=0, keepdims=True) + jnp.sum(p_n, axis=0, keepdims=True)
        o = _bdot_tn(p_p, vp) + _bdot_tn(p_n, vn[:, gs])
        o = o / jnp.sum(eye * l, axis=1, keepdims=True)
        for r in range(hpg):
            h = g * hpg + r
            o_ref[0, :, h * HEAD_DIM:(h + 1) * HEAD_DIM] = o[r * ROWS:(r + 1) * ROWS]


def _dsa_sample(csel, wrow, tabs, page_table, cache_k, cache_v, cache_kidx, n_new):
    q8, qi8, kn8, vn8, kin8 = csel
    nreq = q8.shape[0]
    n_pages = page_table.shape[1]
    page = cache_kidx.shape[1]
    past = n_pages * page
    topk = min(TOPK_MAX, (past + n_new) // 4)
    group = min(LANES // (2 * ROWS), nreq)
    cosk, sink, cosi, sini = tabs
    anyspec = pl.BlockSpec(memory_space=pl.ANY)
    tab = pl.BlockSpec((ROWS, LANES), lambda b, pt: (0, 0))

    grp3 = lambda w: pl.BlockSpec((group, ROWS, w), lambda i, pt: (i, 0, 0))
    sel, ki_new = pl.pallas_call(
        functools.partial(_dsa_select_body, topk, n_new, group),
        grid_spec=pltpu.PrefetchScalarGridSpec(
            num_scalar_prefetch=1,
            grid=(nreq // group,),
            in_specs=[grp3(IDX_HEADS * IDX_DIM), pl.BlockSpec((1, 1, group * LANES), lambda i, pt: (i, 0, 0)),
                      grp3(LANES), tab, tab, anyspec],
            out_specs=[pl.BlockSpec((1, past + LANES, LANES), lambda i, pt: (i, 0, 0)), grp3(IDX_DIM)],
            scratch_shapes=[pltpu.VMEM((2, group, past, IDX_DIM), f32), pltpu.SemaphoreType.DMA((2,))],
        ),
        out_shape=[jax.ShapeDtypeStruct((nreq // group, past + LANES, LANES), f32),
                   jax.ShapeDtypeStruct((nreq, ROWS, IDX_DIM), f32)],
        compiler_params=_cparams(("arbitrary",)),
        name="dsa_select",
    )(page_table, qi8, wrow.reshape(nreq // group, 1, group * LANES), kin8, cosi, sini, cache_kidx)

    req3 = lambda w: pl.BlockSpec((1, ROWS, w), lambda b, pt: (b, 0, 0))
    prow = cache_k.shape[1]
    o, k_new = pl.pallas_call(
        functools.partial(_dsa_sample_body, n_new, group),
        grid_spec=pltpu.PrefetchScalarGridSpec(
            num_scalar_prefetch=1,
            grid=(nreq,),
            in_specs=[req3(ATT_WIDTH), req3(KV_WIDTH), req3(KV_WIDTH),
                      pl.BlockSpec((1, past + LANES, LANES), lambda b, pt: (b // group, 0, 0)), tab, tab,
                      anyspec, anyspec],
            out_specs=[req3(ATT_WIDTH), req3(KV_WIDTH)],
            scratch_shapes=[pltpu.VMEM((2, n_pages * prow, HEAD_DIM), f32),
                            pltpu.VMEM((2, n_pages * prow, HEAD_DIM), f32), pltpu.SemaphoreType.DMA((2, 2))],
        ),
        out_shape=[jax.ShapeDtypeStruct((nreq, ROWS, ATT_WIDTH), f32),
                   jax.ShapeDtypeStruct((nreq, ROWS, KV_WIDTH), f32)],
        compiler_params=_cparams(("arbitrary",)),
        name="dsa_sample",
    )(page_table, q8, kn8, vn8, sel, cosk, sink, cache_k, cache_v)
    return o, k_new, ki_new


def _merge_body(orw_ref, oatt_ref, grw_ref, gatt_ref, prw_ref, patt_ref, o_ref):
    a = jnp.dot(orw_ref[...].astype(bf16), prw_ref[...], preferred_element_type=f32)
    b = jnp.dot(oatt_ref[...].astype(bf16), patt_ref[...], preferred_element_type=f32)
    o_ref[...] = (jax.nn.sigmoid(grw_ref[...]) * a + jax.nn.sigmoid(gatt_ref[...]) * b).astype(bf16)


def _merge(o_rw, o_att, c, p_rw, p_att):
    m = o_rw.shape[0]
    d = p_rw.shape[1]
    tm = min(512, m)
    tn = 1024
    nj = d // tn
    return pl.pallas_call(
        _merge_body,
        grid=(m // tm, nj),
        in_specs=[
            pl.BlockSpec((tm, RW_WIDTH), lambda i, j: (i, 0)),
            pl.BlockSpec((tm, ATT_WIDTH), lambda i, j: (i, 0)),
            pl.BlockSpec((tm, tn), lambda i, j: (i, C_GRW // tn + j)),
            pl.BlockSpec((tm, tn), lambda i, j: (i, C_GATT // tn + j)),
            pl.BlockSpec((RW_WIDTH, tn), lambda i, j: (0, j)),
            pl.BlockSpec((ATT_WIDTH, tn), lambda i, j: (0, j)),
        ],
        out_specs=pl.BlockSpec((tm, tn), lambda i, j: (i, j)),
        out_shape=jax.ShapeDtypeStruct((m, d), bf16),
        compiler_params=_cparams(("parallel", "arbitrary")),
        name="merge",
    )(o_rw, o_att, c, c, p_rw, p_att)


def _outproj_body(mg_ref, x_ref, w_ref, o_ref):
    o_ref[...] = x_ref[...] + jnp.dot(mg_ref[...], w_ref[...], preferred_element_type=f32)


def _outproj(merged, x2d, w_o):
    m, d = x2d.shape
    tm = min(512, m)
    tn = 1024
    return pl.pallas_call(
        _outproj_body,
        grid=(m // tm, d // tn),
        in_specs=[
            pl.BlockSpec((tm, d), lambda i, j: (i, 0)),
            pl.BlockSpec((tm, tn), lambda i, j: (i, j)),
            pl.BlockSpec((d, tn), lambda i, j: (0, j)),
        ],
        out_specs=pl.BlockSpec((tm, tn), lambda i, j: (i, j)),
        out_shape=jax.ShapeDtypeStruct((m, d), f32),
        compiler_params=_cparams(("parallel", "arbitrary")),
        name="out_proj",
    )(merged, x2d, w_o)


def _convglu_body(shift, tiles_per_seq, final_norm, h_ref, g2_ref, gf_ref, cprev_ref, wg_ref, wu_ref, cw_ref,
                  cb_ref, wd_ref, o_ref, tail_ref, n_ref, acc_ref, ext_ref, carry_ref, act_ref):
    i = pl.program_id(0)
    j = pl.program_id(1)
    nj = pl.num_programs(1) - 1
    tm = h_ref.shape[0]
    hist = 2 * shift
    base = ext_ref.shape[0] - tm

    def up_stage():
        n = n_ref[...]
        gate = jnp.dot(n, wg_ref[...], preferred_element_type=f32)
        up = jnp.dot(n, wu_ref[...], preferred_element_type=f32)
        ext_ref[base - hist:base, :] = jnp.where(i % tiles_per_seq == 0, cprev_ref[0], carry_ref[j])
        ext_ref[base:, :] = gate
        cw = cw_ref[...]
        c = (cb_ref[...] + ext_ref[base - hist:base - hist + tm, :] * cw[0:1, :]
             + ext_ref[base - shift:base - shift + tm, :] * cw[1:2, :] + gate * cw[2:3, :])
        tail = ext_ref[base + tm - hist:base + tm, :]
        tail_ref[0] = tail
        carry_ref[j] = tail
        act_ref[j % 2] = ((c * jax.nn.sigmoid(c)) * up).astype(bf16)

    def down_stage():
        acc_ref[...] += jnp.dot(act_ref[(j + 1) % 2], wd_ref[...], preferred_element_type=f32)

    @pl.when(j == 0)
    def _():
        @pl.when(i == 0)
        def _():
            carry_ref[...] = jnp.zeros_like(carry_ref)

        h = h_ref[...]
        ms = jnp.mean(h * h, axis=-1, keepdims=True)
        n_ref[...] = (h * lax.rsqrt(ms + RMS_EPS) * g2_ref[...]).astype(bf16)
        acc_ref[...] = jnp.zeros_like(acc_ref)
        up_stage()

    @pl.when((j > 0) & (j < nj))
    def _():
        up_stage()
        down_stage()

    @pl.when(j == nj)
    def _():
        down_stage()
        out = h_ref[...] + acc_ref[...]
        if final_norm:
            ms = jnp.mean(out * out, axis=-1, keepdims=True)
            out = out * lax.rsqrt(ms + RMS_EPS) * gf_ref[...]
        o_ref[...] = out


def _convglu(h2d, conv_prev, norm2, norm_f, w_up_b, conv_w, conv_b, w_down_b, nseq_groups, shift, final_norm):
    m, d = h2d.shape
    d_ff = w_down_b.shape[0]
    rows_per_group = m // nseq_groups
    tm = min(512, rows_per_group)
    tf = 512
    nj = d_ff // tf
    tiles_per_seq = rows_per_group // tm
    hist = 2 * shift
    base = ((hist + 7) // 8) * 8
    up_j = lambda j: jnp.minimum(j, nj - 1)
    down_j = lambda j: jnp.maximum(j - 1, 0)
    out, tails = pl.pallas_call(
        functools.partial(_convglu_body, shift, tiles_per_seq, final_norm),
        grid=(m // tm, nj + 1),
        in_specs=[
            pl.BlockSpec((tm, d), lambda i, j: (i, 0)),
            pl.BlockSpec((1, d), lambda i, j: (0, 0)),
            pl.BlockSpec((1, d), lambda i, j: (0, 0)),
            pl.BlockSpec((1, hist, tf), lambda i, j: (i // tiles_per_seq, 0, up_j(j))),
            pl.BlockSpec((d, tf), lambda i, j: (0, up_j(j))),
            pl.BlockSpec((d, tf), lambda i, j: (0, nj + up_j(j))),
            pl.BlockSpec((CONV_W, tf), lambda i, j: (0, up_j(j))),
            pl.BlockSpec((1, tf), lambda i, j: (0, up_j(j))),
            pl.BlockSpec((tf, d), lambda i, j: (down_j(j), 0)),
        ],
        out_specs=[
            pl.BlockSpec((tm, d), lambda i, j: (i, 0)),
            pl.BlockSpec((1, hist, tf), lambda i, j: (i, 0, up_j(j))),
        ],
        out_shape=[jax.ShapeDtypeStruct((m, d), f32), jax.ShapeDtypeStruct((m // tm, hist, d_ff), f32)],
        scratch_shapes=[pltpu.VMEM((tm, d), bf16), pltpu.VMEM((tm, d), f32), pltpu.VMEM((base + tm, tf), f32),
                        pltpu.VMEM((nj, hist, tf), f32), pltpu.VMEM((2, tm, tf), bf16)],
        compiler_params=_cparams(("arbitrary", "arbitrary")),
        name="convglu",
    )(h2d, norm2, norm_f, conv_prev, w_up_b, w_up_b, conv_w, conv_b, w_down_b)
    return out, tails[tiles_per_seq - 1::tiles_per_seq]


def _pad_cols(a, width):
    return jnp.pad(a, [(0, 0)] * (a.ndim - 1) + [(0, width - a.shape[-1])])


def _to_layout(a):
    rw = 3 * RW_WIDTH
    o = {}
    o["r"], o["k"], o["v"] = a[..., 0:RW_WIDTH], a[..., RW_WIDTH:2 * RW_WIDTH], a[..., 2 * RW_WIDTH:rw]
    p = rw
    o["wl"] = a[..., p:p + D_DECAY_LORA]; p += D_DECAY_LORA
    o["al"] = a[..., p:p + D_AAA_LORA]; p += D_AAA_LORA
    o["gl"] = a[..., p:p + D_GATE_LORA]; p += D_GATE_LORA
    o["q"] = a[..., p:p + ATT_WIDTH]; p += ATT_WIDTH
    o["ak"] = a[..., p:p + KV_WIDTH]; p += KV_WIDTH
    o["av"] = a[..., p:p + KV_WIDTH]; p += KV_WIDTH
    o["qi"] = a[..., p:p + IDX_HEADS * IDX_DIM]; p += IDX_HEADS * IDX_DIM
    o["ki"] = a[..., p:p + IDX_DIM]; p += IDX_DIM
    o["wi"] = a[..., p:p + IDX_HEADS]; p += IDX_HEADS
    d = (a.shape[-1] - p) // 2
    o["grw"], o["gatt"] = a[..., p:p + d], a[..., p + d:p + 2 * d]
    return jnp.concatenate([
        o["r"], o["k"], o["v"], o["q"], o["grw"], o["gatt"], o["qi"],
        _pad_cols(o["wl"], LANES), _pad_cols(o["al"], LANES), o["gl"],
        _pad_cols(o["ki"], LANES), o["ak"], o["av"], _pad_cols(o["wi"], LANES)], axis=-1)


def _rw_cols_layout(a):
    rw = 3 * RW_WIDTH
    wl = a[..., rw:rw + D_DECAY_LORA]
    al = a[..., rw + D_DECAY_LORA:rw + D_DECAY_LORA + D_AAA_LORA]
    gl = a[..., rw + D_DECAY_LORA + D_AAA_LORA:]
    return a[..., :rw], jnp.concatenate([_pad_cols(wl, LANES), _pad_cols(al, LANES), gl], axis=-1)


def _rw_cols_from_layout(c):
    return jnp.concatenate([c[..., :3 * RW_WIDTH], c[..., C_LORA:C_LORA + D_DECAY_LORA],
                            c[..., C_LORA + LANES:C_LORA + LANES + D_AAA_LORA],
                            c[..., C_LORA + 2 * LANES:C_LORA + 3 * LANES]], axis=-1)


def _rope_tables(pos, rows):
    pos = jnp.pad(pos.astype(f32), (0, rows - pos.shape[0]))
    out = []
    for dim in (HEAD_DIM, IDX_DIM):
        half = dim // 2
        inv_freq = 1.0 / (ROPE_THETA ** (jnp.arange(half, dtype=f32) / half))
        ang = pos[:, None] * inv_freq[None, :]
        cos, sin = jnp.cos(ang), jnp.sin(ang)
        reps = LANES // dim
        out.append(jnp.tile(jnp.concatenate([cos, cos], axis=1), (1, reps)))
        out.append(jnp.tile(jnp.concatenate([-sin, sin], axis=1), (1, reps)))
    return tuple(out)


def _pair_unblock(s):
    n = s.shape[0]
    a = s[:, :, :RW_HEAD, :RW_HEAD]
    b = s[:, :, RW_HEAD:, RW_HEAD:]
    return jnp.stack([a, b], axis=2).reshape(n, RW_HEADS, RW_HEAD, RW_HEAD)


def kernel(x_prompt, x_sample, cache_k, cache_v, cache_kidx, state_wkv, state_shift, state_conv, page_table, norm1, w_in, rw_mu, rw_w0, rw_w2, rw_a0, rw_a2, rw_g2, rw_k_k, rw_k_a, rw_r_k, rw_lnx_w, rw_lnx_b, p_rw, p_att, w_o, norm2, w_up, conv_w, conv_b, w_down, norm_f):
    B, S, D = x_prompt.shape
    DB, DS, _ = x_sample.shape
    depth = w_in.shape[0]
    page = cache_k.shape[2]
    n_pages = page_table.shape[1]
    past_len = n_pages * page
    d_ff = w_down.shape[1]
    dt = x_prompt.dtype

    tabs_p = _rope_tables(jnp.arange(S), S)
    tabs_s = _rope_tables(past_len + jnp.arange(DS), ROWS)
    row2 = lambda a: a.reshape(1, -1)

    hp = x_prompt.reshape(B * S, D)
    hs = x_sample.reshape(DB * DS, D)
    outs = {k: [] for k in ("kp", "vp", "kip", "ks", "vs", "kis", "wkvp", "wkvs", "shp", "shs", "cvp", "cvs")}
    for l in range(depth):
        w_in_p = _to_layout(w_in[l]).astype(bf16)
        mu_rkv, mu_lora = _rw_cols_layout(rw_mu[l][None, :])
        params = dict(
            mu_rkv=mu_rkv, mu_lora=mu_lora, w0=row2(rw_w0[l]),
            w2=jnp.pad(rw_w2[l], ((0, LANES - D_DECAY_LORA), (0, 0))).astype(bf16), a0=row2(rw_a0[l]),
            a2=jnp.pad(rw_a2[l], ((0, LANES - D_AAA_LORA), (0, 0))).astype(bf16), g2=rw_g2[l].astype(bf16),
            k_k=row2(rw_k_k[l]), k_a=row2(rw_k_a[l]), r_k=row2(rw_r_k[l]), lnx_w=row2(rw_lnx_w[l]),
            lnx_b=row2(rw_lnx_b[l]))
        last = l == depth - 1

        c_p = _in_proj(hp, row2(norm1[l]), w_in_p)
        c_s = _in_proj(hs, row2(norm1[l]), w_in_p)

        sp_rkv = jnp.zeros((B, 1, RKV_W), dt)
        sp_lora = jnp.zeros((B, 1, LORA_W), dt)
        s0_p = jnp.zeros((B, RW_HEADS // 2, LANES, LANES), dt)
        o_rw_p, wkv_p = _rwkv_prompt(c_p, sp_rkv, sp_lora, s0_p, params, B, S)

        c_s3 = c_s.reshape(DB, DS, NP_COLS)
        ss_rkv, ss_lora = _rw_cols_layout(state_shift[l])
        prev_rkv = jnp.concatenate([ss_rkv[:, None, :], c_s3[:, :-1, :RKV_W]], axis=1).reshape(DB * DS, RKV_W)
        prev_lora = jnp.concatenate([ss_lora[:, None, :], c_s3[:, :-1, C_LORA:C_LORA + LORA_W]], axis=1)
        prev_lora = prev_lora.reshape(DB * DS, LORA_W)
        r_s, lw_s, k_s, v_s, a_s, b_s, g_s = _rwkv_prep(c_s, prev_rkv, prev_lora, params)
        y_s, wkv_s = _rwkv_seq((r_s, lw_s, k_s, v_s, a_s, b_s), state_wkv[l], DB, DS)
        o_rw_s = _rwkv_post(y_s.reshape(DB * DS, RW_WIDTH), r_s, k_s, v_s, g_s, params)

        o_att_p, k_p, ki_p = _dsa_prompt(c_p, tabs_p, B, S)

        def rows8(lo, w):
            return jnp.pad(c_s3[:, :, lo:lo + w], ((0, 0), (0, ROWS - DS), (0, 0)))

        csel = (rows8(C_Q, ATT_WIDTH), rows8(C_QI, IDX_HEADS * IDX_DIM), rows8(C_AK, KV_WIDTH),
                rows8(C_AV, KV_WIDTH), rows8(C_KI, LANES))
        wi_s = jnp.pad(c_s3[:, :, C_WI:C_WI + IDX_HEADS], ((0, 0), (0, ROWS - DS), (0, 0)))
        wrow = jnp.transpose(wi_s, (0, 2, 1)).reshape(DB, 1, IDX_HEADS * ROWS)
        o_att_s8, k_s8, ki_s8 = _dsa_sample(
            csel, wrow, tabs_s, page_table, cache_k[l].reshape(-1, page * KV_HEADS, HEAD_DIM),
            cache_v[l].reshape(-1, page * KV_HEADS, HEAD_DIM), cache_kidx[l], DS)
        o_att_s = o_att_s8[:, :DS].reshape(DB * DS, ATT_WIDTH)

        p_rw_b, p_att_b, w_o_b = p_rw[l].astype(bf16), p_att[l].astype(bf16), w_o[l].astype(bf16)
        h_p = _outproj(_merge(o_rw_p, o_att_p, c_p, p_rw_b, p_att_b), hp, w_o_b)
        h_s = _outproj(_merge(o_rw_s, o_att_s, c_s, p_rw_b, p_att_b), hs, w_o_b)

        w_up_b, w_down_b = w_up[l].astype(bf16), w_down[l].astype(bf16)
        cv_args = (row2(norm2[l]), row2(norm_f), w_up_b, conv_w[l], row2(conv_b[l]), w_down_b)
        hp, tail_p = _convglu(h_p, jnp.zeros((B, CONV_W - 1, d_ff), dt), *cv_args, B, 1, last)
        h_s_tm = h_s.reshape(DB, DS, D).transpose(1, 0, 2).reshape(DS * DB, D)
        cprev_tm = state_conv[l].transpose(1, 0, 2).reshape(1, (CONV_W - 1) * DB, d_ff)
        hs_tm, tail_s = _convglu(h_s_tm, cprev_tm, *cv_args, 1, DB, last)
        hs = hs_tm.reshape(DS, DB, D).transpose(1, 0, 2).reshape(DB * DS, D)

        outs["kp"].append(k_p.reshape(B, S // page, page, KV_HEADS, HEAD_DIM))
        outs["vp"].append(c_p[:, C_AV:C_AV + KV_WIDTH].reshape(B, S // page, page, KV_HEADS, HEAD_DIM))
        outs["kip"].append(ki_p.reshape(B, S // page, page, IDX_DIM))
        outs["ks"].append(k_s8[:, :DS].reshape(DB, DS, KV_HEADS, HEAD_DIM))
        outs["vs"].append(c_s3[:, :, C_AV:C_AV + KV_WIDTH].reshape(DB, DS, KV_HEADS, HEAD_DIM))
        outs["kis"].append(ki_s8[:, :DS])
        outs["wkvp"].append(_pair_unblock(wkv_p))
        outs["wkvs"].append(wkv_s)
        outs["shp"].append(_rw_cols_from_layout(c_p.reshape(B, S, NP_COLS)[:, -1]))
        outs["shs"].append(_rw_cols_from_layout(c_s3[:, -1]))
        outs["cvp"].append(tail_p)
        outs["cvs"].append(tail_s.reshape(CONV_W - 1, DB, d_ff).transpose(1, 0, 2))

    y_prompt = hp.reshape(B, S, D)
    y_sample = hs.reshape(DB, DS, D)
    st = lambda k: jnp.stack(outs[k])
    return (y_prompt, y_sample, st("kp"), st("vp"), st("kip"), st("ks"), st("vs"), st("kis"), st("wkvp"),
            st("wkvs"), st("shp"), st("shs"), st("cvp"), st("cvs"))
```

```python
import functools
import math

import numpy as np
import jax
import jax.numpy as jnp
from jax import lax
from jax.experimental import pallas as pl
from jax.experimental.pallas import tpu as pltpu

f32 = jnp.float32
bf16 = jnp.bfloat16
i32 = jnp.int32

RW_HEADS = 16
RW_HEAD = 64
RW_WIDTH = RW_HEADS * RW_HEAD
D_DECAY_LORA = 96
D_AAA_LORA = 96
D_GATE_LORA = 128
GN_EPS = 64e-5
ATT_HEADS = 8
KV_HEADS = 2
HEAD_DIM = 128
ATT_WIDTH = ATT_HEADS * HEAD_DIM
KV_WIDTH = KV_HEADS * HEAD_DIM
IDX_HEADS = 16
IDX_DIM = 64
IDX_W_SCALE = (IDX_HEADS * IDX_DIM) ** -0.5
TOPK_MAX = 256
Q_BLOCK = 128
ROPE_THETA = 10000.0
CONV_W = 3
RMS_EPS = 1e-6

LANES = 128
NEG_BIG = -1e30
INT_MIN = -(2 ** 31)

C_R, C_K, C_V, C_Q, C_GRW, C_GATT, C_QI = 0, 1024, 2048, 3072, 4096, 6144, 8192
C_LORA = 9216
C_KI, C_AK, C_AV, C_WI = 9600, 9728, 9984, 10240
NP_COLS = 10368
LORA_W = 384
RKV_W = 3 * RW_WIDTH

VMEM_LIMIT = 56 * 1024 * 1024


def _cparams(sem):
    return pltpu.CompilerParams(dimension_semantics=sem, vmem_limit_bytes=VMEM_LIMIT)


def _bdot(a, b):
    return jnp.dot(a.astype(bf16), b.astype(bf16), preferred_element_type=f32)


def _bdot_nt(a, b):
    return lax.dot_general(a.astype(bf16), b.astype(bf16), (((1,), (1,)), ((), ())), preferred_element_type=f32)


def _bdot_tn(a, b):
    return lax.dot_general(a.astype(bf16), b.astype(bf16), (((0,), (0,)), ((), ())), preferred_element_type=f32)


def _split3(x):
    x1 = x.astype(bf16)
    r1 = x - x1.astype(f32)
    x2 = r1.astype(bf16)
    x3 = (r1 - x2.astype(f32)).astype(bf16)
    return x1, x2, x3


def _split2(x):
    x1 = x.astype(bf16)
    return x1, (x - x1.astype(f32)).astype(bf16)


def _headsum(x):
    ri = lax.broadcasted_iota(i32, (LANES, LANES), 0) // RW_HEAD
    ci = lax.broadcasted_iota(i32, (LANES, LANES), 1) // RW_HEAD
    bd = (ri == ci).astype(bf16)
    outs = []
    for i in range(x.shape[1] // LANES):
        hi, lo = _split2(x[:, i * LANES:(i + 1) * LANES])
        outs.append(jnp.dot(hi, bd, preferred_element_type=f32) + jnp.dot(lo, bd, preferred_element_type=f32))
    return jnp.concatenate(outs, axis=1)


def _softplus(x):
    return jnp.maximum(x, 0.0) + jnp.log(1.0 + jnp.exp(-jnp.abs(x)))


def _rope(x, cos, sin, half):
    w = x.shape[1]
    reps = w // LANES
    if reps > 1:
        cos = jnp.concatenate([cos] * reps, axis=1)
        sin = jnp.concatenate([sin] * reps, axis=1)
    if 2 * half == LANES and w == LANES:
        partner = pltpu.roll(x, half, axis=1)
    else:
        lane = lax.broadcasted_iota(i32, (1, w), 1)
        first = (lane % (2 * half)) < half
        partner = jnp.where(first, pltpu.roll(x, w - half, axis=1), pltpu.roll(x, half, axis=1))
    return x * cos + partner * sin


def _inproj_body(x_ref, g_ref, w_ref, o_ref, n_ref):
    @pl.when(pl.program_id(1) == 0)
    def _():
        x = x_ref[...]
        ms = jnp.mean(x * x, axis=-1, keepdims=True)
        n_ref[...] = (x * lax.rsqrt(ms + RMS_EPS) * g_ref[...]).astype(bf16)

    o_ref[...] = jnp.dot(n_ref[...], w_ref[...], preferred_element_type=f32)


def _in_proj(x2d, gain, w_p):
    m, d = x2d.shape
    tm = 1024 if m % 1024 == 0 else min(512, m)
    tn = NP_COLS // 9
    return pl.pallas_call(
        _inproj_body,
        grid=(m // tm, NP_COLS // tn),
        in_specs=[
            pl.BlockSpec((tm, d), lambda i, j: (i, 0)),
            pl.BlockSpec((1, d), lambda i, j: (0, 0)),
            pl.BlockSpec((d, tn), lambda i, j: (0, j)),
        ],
        out_specs=pl.BlockSpec((tm, tn), lambda i, j: (i, j)),
        out_shape=jax.ShapeDtypeStruct((m, NP_COLS), f32),
        scratch_shapes=[pltpu.VMEM((tm, d), bf16)],
        compiler_params=_cparams(("parallel", "arbitrary")),
        name="in_proj",
    )(x2d, gain, w_p)


RW_PARAM_NAMES = ("mu_rkv", "mu_lora", "w0", "w2", "a0", "a2", "g2", "k_k", "k_a", "r_k", "lnx_w", "lnx_b")


def _rw_prep(c_rkv, c_lora, p_rkv, p_lora, P):
    m = c_rkv + (p_rkv - c_rkv) * P["mu_rkv"]
    ml = c_lora + (p_lora - c_lora) * P["mu_lora"]
    r, k, v = m[:, :RW_WIDTH], m[:, RW_WIDTH:2 * RW_WIDTH], m[:, 2 * RW_WIDTH:]
    wl, al, gl = ml[:, :LANES], ml[:, LANES:2 * LANES], ml[:, 2 * LANES:]
    w_log = -_softplus(-(P["w0"] + _bdot(jnp.tanh(wl), P["w2"]))) - 0.5
    logw = -jnp.exp(w_log)
    asig = jax.nn.sigmoid(P["a0"] + _bdot(al, P["a2"]))
    g = _bdot(jax.nn.sigmoid(gl), P["g2"])
    kk = k * P["k_k"]
    kkn = kk * lax.rsqrt(jnp.maximum(_headsum(kk * kk), 1e-24))
    k2 = k * (1.0 + (asig - 1.0) * P["k_a"])
    return r, logw, k2, v, -kkn, kkn * asig, g


def _rw_post(y, r, k2, v, g, P):
    inv_n = 1.0 / RW_HEAD
    mean = _headsum(y) * inv_n
    d = y - mean
    var = _headsum(d * d) * inv_n
    yn = d * lax.rsqrt(var + GN_EPS) * P["lnx_w"] + P["lnx_b"]
    bonus = _headsum(r * k2 * P["r_k"]) * v
    return (yn + bonus) * g


def _chunk_step(at, rt, bt, kt, be, ke, v, wc, S, masks, nsteps):
    keep, eye2, colh, mA, bdmask = masks
    R = range(len(at))
    C = at[0].shape[0]
    X = [jnp.concatenate([at[p], rt[p]], axis=0) for p in R]
    scA = [jnp.where(keep, _bdot_nt(jnp.where(mA, X[p], 0.0), jnp.concatenate([bt[p], kt[p]], axis=0)), 0.0)
           for p in R]
    scB = [jnp.where(keep, _bdot_nt(jnp.where(mA, 0.0, X[p]), jnp.concatenate([kt[p], bt[p]], axis=0)), 0.0)
           for p in R]
    XS = [_bdot_nt(X[p], S[p]) for p in R]
    Lp = [jnp.concatenate([jnp.where(colh, scA[p][:C], 0.0), jnp.where(colh, 0.0, scB[p][:C])], axis=0) for p in R]
    T = [eye2 + Lp[p] for p in R]
    if nsteps > 0:
        Pw = [_bdot(Lp[p], Lp[p]) for p in R]
        for i in range(nsteps):
            if i < nsteps - 1:
                Z = [_bdot(Pw[p], jnp.concatenate([T[p], Pw[p]], axis=1)) for p in R]
                T = [T[p] + Z[p][:, :2 * C] for p in R]
                Pw = [Z[p][:, 2 * C:] for p in R]
            else:
                T = [T[p] + _bdot(Pw[p], T[p]) for p in R]
    vA = [jnp.where(mA, v[p], 0.0) for p in R]
    vB = [jnp.where(mA, 0.0, v[p]) for p in R]
    G = [XS[p][:C] + _bdot(jnp.where(colh, scB[p][:C], scA[p][:C]), jnp.concatenate([vB[p], vA[p]], axis=0))
         for p in R]
    U = [_bdot(jnp.where(colh, T[p][:C], T[p][C:]),
               jnp.concatenate([jnp.where(mA, G[p], 0.0), jnp.where(mA, 0.0, G[p])], axis=0)) for p in R]
    Y = [XS[p][C:] + _bdot(jnp.concatenate([scA[p][C:], scB[p][C:]], axis=1),
                           jnp.concatenate([jnp.where(mA, U[p], 0.0), vA[p], vB[p], jnp.where(mA, 0.0, U[p])], axis=0))
         for p in R]
    dS = [_bdot_tn(jnp.concatenate([U[p], v[p]], axis=0), jnp.concatenate([be[p], ke[p]], axis=0)) for p in R]
    S_new = [S[p] * wc[p] + jnp.where(bdmask, dS[p], 0.0) for p in R]
    return Y, S_new


def _chunk_masks(C):
    row = lax.broadcasted_iota(i32, (2 * C, 2 * C), 0)
    col = lax.broadcasted_iota(i32, (2 * C, 2 * C), 1)
    t = jnp.where(row >= C, row - C, row)
    s = jnp.where(col >= C, col - C, col)
    keep = (s < t) | ((row >= C) & (s == t))
    eye2 = (row == col).astype(f32)
    colh = lax.broadcasted_iota(i32, (C, 2 * C), 1) < C
    mA = lax.broadcasted_iota(i32, (1, LANES), 1) < RW_HEAD
    r2 = lax.broadcasted_iota(i32, (LANES, LANES), 0) // RW_HEAD
    c2 = lax.broadcasted_iota(i32, (LANES, LANES), 1) // RW_HEAD
    return keep, eye2, colh, mA, r2 == c2


def _rwkv_prompt_body(chunk, rkv_ref, lora_ref, sp_rkv_ref, sp_lora_ref, s0_ref, *rest):
    np_ = len(RW_PARAM_NAMES)
    P = {n: rest[i][...] for i, n in enumerate(RW_PARAM_NAMES)}
    o_ref, sout_ref = rest[np_], rest[np_ + 1]
    (S_ref, car_rkv, car_lora, at_s, rt_s, bt_s, kt_s, be_s, ke_s, v_s, cum_s, y_s) = rest[np_ + 2:]
    t = pl.program_id(1)
    tt = rkv_ref.shape[0]
    C = chunk

    @pl.when(t == 0)
    def _():
        S_ref[...] = s0_ref[0]
        car_rkv[...] = sp_rkv_ref[0]
        car_lora[...] = sp_lora_ref[0]

    c_rkv = rkv_ref[...]
    c_lora = lora_ref[...]
    first = lax.broadcasted_iota(i32, (tt, 1), 0) == 0
    p_rkv = jnp.where(first, car_rkv[...], pltpu.roll(c_rkv, 1, axis=0))
    p_lora = jnp.where(first, car_lora[...], pltpu.roll(c_lora, 1, axis=0))
    car_rkv[...] = c_rkv[tt - 1:tt, :]
    car_lora[...] = c_lora[tt - 1:tt, :]
    r, logw, k2, v, a, b, g = _rw_prep(c_rkv, c_lora, p_rkv, p_lora, P)

    ri = lax.broadcasted_iota(i32, (tt, tt), 0)
    ci = lax.broadcasted_iota(i32, (tt, tt), 1)
    same = ri // C == ci // C
    tril = (same & (ci <= ri)).astype(bf16)
    triu = (same & (ci > ri)).astype(bf16)
    pieces = _split3(logw)
    cum = sum(jnp.dot(tril, piece, preferred_element_type=f32) for piece in pieces)
    rev = sum(jnp.dot(triu, piece, preferred_element_type=f32) for piece in pieces)
    iw = jnp.exp(-cum)
    ew = jnp.exp(rev)
    at_s[...] = a * jnp.exp(cum - logw)
    rt_s[...] = r * jnp.exp(cum)
    bt_s[...] = b * iw
    kt_s[...] = k2 * iw
    be_s[...] = b * ew
    ke_s[...] = k2 * ew
    v_s[...] = v
    cum_s[...] = cum
    masks = _chunk_masks(C)
    nsteps = max(int(math.ceil(math.log2(C))) - 1, 0)
    pairs = range(RW_WIDTH // LANES)

    def chunk_body(ci_, carry):
        r0 = pl.multiple_of(ci_ * C, C)
        lanes = [slice(p * LANES, (p + 1) * LANES) for p in pairs]
        ld = lambda ref: [ref[pl.ds(r0, C), lanes[p]] for p in pairs]
        last8 = pl.multiple_of(r0 + C - 8, 8)
        wc = [jnp.exp(cum_s[pl.ds(last8, 8), lanes[p]][7:8]) for p in pairs]
        Y, S_new = _chunk_step(ld(at_s), ld(rt_s), ld(bt_s), ld(kt_s), ld(be_s), ld(ke_s), ld(v_s), wc,
                               [S_ref[p] for p in pairs], masks, nsteps)
        for p in pairs:
            y_s[pl.ds(r0, C), lanes[p]] = Y[p]
            S_ref[p] = S_new[p]
        return carry

    lax.fori_loop(0, tt // C, chunk_body, 0)
    o_ref[...] = _rw_post(y_s[...], r, k2, v, g, P)

    @pl.when(t == pl.num_programs(1) - 1)
    def _():
        sout_ref[0] = S_ref[...]


def _const_spec(shape):
    nd = len(shape)
    return pl.BlockSpec(shape, lambda *_: (0,) * nd)


def _rwkv_prompt(c, sp_rkv, sp_lora, s0, params, nb, seq):
    tt = min(256, seq)
    chunk = min(64, tt)
    nt = seq // tt
    npairs = RW_WIDTH // LANES
    in_specs = [
        pl.BlockSpec((tt, RKV_W), lambda b, t: (b * nt + t, 0)),
        pl.BlockSpec((tt, LORA_W), lambda b, t: (b * nt + t, C_LORA // LORA_W)),
        pl.BlockSpec((1, 1, RKV_W), lambda b, t: (b, 0, 0)),
        pl.BlockSpec((1, 1, LORA_W), lambda b, t: (b, 0, 0)),
        pl.BlockSpec((1, npairs, LANES, LANES), lambda b, t: (b, 0, 0, 0)),
    ] + [_const_spec(params[n].shape) for n in RW_PARAM_NAMES]
    out_specs = [
        pl.BlockSpec((tt, RW_WIDTH), lambda b, t: (b * nt + t, 0)),
        pl.BlockSpec((1, npairs, LANES, LANES), lambda b, t: (b, 0, 0, 0)),
    ]
    scratch = [pltpu.VMEM((npairs, LANES, LANES), f32), pltpu.VMEM((1, RKV_W), f32), pltpu.VMEM((1, LORA_W), f32)]
    scratch += [pltpu.VMEM((tt, RW_WIDTH), f32) for _ in range(9)]
    return pl.pallas_call(
        functools.partial(_rwkv_prompt_body, chunk),
        grid=(nb, nt),
        in_specs=in_specs,
        out_specs=out_specs,
        out_shape=[jax.ShapeDtypeStruct((nb * seq, RW_WIDTH), f32),
                   jax.ShapeDtypeStruct((nb, npairs, LANES, LANES), f32)],
        scratch_shapes=scratch,
        compiler_params=_cparams(("parallel", "arbitrary")),
        name="rwkv_prompt",
    )(c, c, sp_rkv, sp_lora, s0, *[params[n] for n in RW_PARAM_NAMES])


def _rwkv_prep_body(rkv_ref, lora_ref, prkv_ref, plora_ref, *rest):
    np_ = len(RW_PARAM_NAMES)
    P = {n: rest[i][...] for i, n in enumerate(RW_PARAM_NAMES)}
    outs = rest[np_:]
    vals = _rw_prep(rkv_ref[...], lora_ref[...], prkv_ref[...], plora_ref[...], P)
    for o, v in zip(outs, vals):
        o[...] = v


def _rwkv_prep(c, prev_rkv, prev_lora, params):
    m = c.shape[0]
    tm = min(512, m)
    in_specs = [
        pl.BlockSpec((tm, RKV_W), lambda i: (i, 0)),
        pl.BlockSpec((tm, LORA_W), lambda i: (i, C_LORA // LORA_W)),
        pl.BlockSpec((tm, RKV_W), lambda i: (i, 0)),
        pl.BlockSpec((tm, LORA_W), lambda i: (i, 0)),
    ] + [_const_spec(params[n].shape) for n in RW_PARAM_NAMES]
    return pl.pallas_call(
        _rwkv_prep_body,
        grid=(m // tm,),
        in_specs=in_specs,
        out_specs=[pl.BlockSpec((tm, RW_WIDTH), lambda i: (i, 0)) for _ in range(7)],
        out_shape=[jax.ShapeDtypeStruct((m, RW_WIDTH), f32) for _ in range(7)],
        compiler_params=_cparams(("parallel",)),
        name="rwkv_prep",
    )(c, c, prev_rkv, prev_lora, *[params[n] for n in RW_PARAM_NAMES])


def _rwkv_seq_body(r_ref, lw_ref, k_ref, v_ref, a_ref, b_ref, s_ref, y_ref, sout_ref):
    steps = r_ref.shape[0]
    w = [jnp.exp(lw_ref[t]) for t in range(steps)]
    for i in range(RW_HEAD):
        Si = s_ref[0, i]
        for t in range(steps):
            sa = jnp.sum(Si * a_ref[t], axis=0, keepdims=True)
            Si = Si * w[t] + sa * b_ref[t] + v_ref[t, i:i + 1, :] * k_ref[t]
            y_ref[t, i:i + 1, :] = jnp.sum(Si * r_ref[t], axis=0, keepdims=True)
        sout_ref[0, i] = Si


def _rwkv_seq(ops, s0, nreq, steps):
    ops_t = [o.reshape(nreq, steps, RW_WIDTH).transpose(1, 2, 0) for o in ops]
    vec = pl.BlockSpec((steps, RW_HEAD, nreq), lambda h: (0, h, 0))
    st = pl.BlockSpec((1, RW_HEAD, RW_HEAD, nreq), lambda h: (h, 0, 0, 0))
    y_t, s_t = pl.pallas_call(
        _rwkv_seq_body,
        grid=(RW_HEADS,),
        in_specs=[vec] * 6 + [st],
        out_specs=[vec, st],
        out_shape=[jax.ShapeDtypeStruct((steps, RW_WIDTH, nreq), f32),
                   jax.ShapeDtypeStruct((RW_HEADS, RW_HEAD, RW_HEAD, nreq), f32)],
        compiler_params=_cparams(("parallel",)),
        name="rwkv_seq",
    )(*ops_t, s0.transpose(1, 2, 3, 0))
    return y_t.transpose(2, 0, 1).reshape(nreq * steps, RW_WIDTH), s_t.transpose(3, 0, 1, 2)


def _rwkv_post_body(y_ref, r_ref, k_ref, v_ref, g_ref, *rest):
    np_ = len(RW_PARAM_NAMES)
    P = {n: rest[i][...] for i, n in enumerate(RW_PARAM_NAMES)}
    rest[np_][...] = _rw_post(y_ref[...], r_ref[...], k_ref[...], v_ref[...], g_ref[...], P)


def _rwkv_post(y, r, k2, v, g, params):
    m = y.shape[0]
    tm = min(512, m)
    return pl.pallas_call(
        _rwkv_post_body,
        grid=(m // tm,),
        in_specs=[pl.BlockSpec((tm, RW_WIDTH), lambda i: (i, 0)) for _ in range(5)]
        + [_const_spec(params[n].shape) for n in RW_PARAM_NAMES],
        out_specs=pl.BlockSpec((tm, RW_WIDTH), lambda i: (i, 0)),
        out_shape=jax.ShapeDtypeStruct((m, RW_WIDTH), f32),
        compiler_params=_cparams(("parallel",)),
        name="rwkv_post",
    )(y, r, k2, v, g, *[params[n] for n in RW_PARAM_NAMES])


def _sort_key(score):
    bits = pltpu.bitcast(score, i32)
    key = jnp.where(bits < 0, bits ^ jnp.int32(0x7FFFFFFF), bits)
    return jnp.where(score == 0.0, jnp.int32(0), key)


def _kth_largest(keys, k):
    def body(i, ts):
        bit = lax.shift_left(jnp.int32(1), jnp.int32(31) - i)
        cands = [t + bit for t in ts]
        cnts = [jnp.sum((key >= c).astype(f32), axis=1, keepdims=True) for key, c in zip(keys, cands)]
        return tuple(jnp.where(n >= k, c, t) for n, c, t in zip(cnts, cands, ts))

    init = tuple(jnp.full((key.shape[0], 1), INT_MIN, i32) for key in keys)
    return lax.fori_loop(0, 32, body, init, unroll=4)


def _row_groups(x, n):
    step = x.shape[0] // n
    return [x[g * step:(g + 1) * step] for g in range(n)]


def _dsa_prompt_body(topk, q_ref, qi_ref, wi_ref, k_ref, v_ref, ki_ref, cosk_ref, sink_ref, cosi_ref, sini_ref,
                     o_ref, kout_ref, kiout_ref, kb_ref, vb_ref, kib_ref, bias_ref):
    qb = pl.program_id(1)
    seq = k_ref.shape[0]
    nq = q_ref.shape[0]

    @pl.when(qb == 0)
    def _():
        cos, sin = cosk_ref[...], sink_ref[...]
        kr = jnp.concatenate([_rope(k_ref[:, h * HEAD_DIM:(h + 1) * HEAD_DIM], cos, sin, HEAD_DIM // 2)
                              for h in range(KV_HEADS)], axis=1)
        kout_ref[...] = kr
        kb_ref[...] = kr.astype(bf16)
        vb_ref[...] = v_ref[...].astype(bf16)
        kir = _rope(ki_ref[...], cosi_ref[...], sini_ref[...], IDX_DIM // 2)
        kiout_ref[...] = kir[:, :IDX_DIM]
        kib_ref[...] = (kir + pltpu.roll(kir, IDX_DIM, axis=1)).astype(bf16)

    r0 = pl.multiple_of(qb * nq, nq)
    cosq, sinq = cosk_ref[pl.ds(r0, nq), :], sink_ref[pl.ds(r0, nq), :]
    cosqi, sinqi = cosi_ref[pl.ds(r0, nq), :], sini_ref[pl.ds(r0, nq), :]
    q = _rope(q_ref[...], cosq, sinq, HEAD_DIM // 2).astype(bf16)
    qi = _rope(qi_ref[...], cosqi, sinqi, IDX_DIM // 2)
    wi = wi_ref[...] * IDX_W_SCALE
    lane = lax.broadcasted_iota(i32, (1, LANES), 1)
    scale = HEAD_DIM ** -0.5
    hpg = ATT_HEADS // KV_HEADS

    def process(ext):
        kib = kib_ref[:ext, :]
        sc = None
        for h in range(IDX_HEADS):
            pair = qi[:, (h // 2) * LANES:(h // 2 + 1) * LANES]
            mine = (lane < IDX_DIM) if h % 2 == 0 else (lane >= IDX_DIM)
            term = jnp.maximum(_bdot_nt(jnp.where(mine, pair, 0.0), kib), 0.0) * wi[:, h:h + 1]
            sc = term if sc is None else sc + term

        qpos = r0 + lax.broadcasted_iota(i32, (nq, 1), 0)
        kpos = lax.broadcasted_iota(i32, (1, ext), 1)
        causal = kpos <= qpos
        key = jnp.where(causal, _sort_key(sc), INT_MIN)
        thr = jnp.concatenate(_kth_largest(_row_groups(key, 4), float(topk)), axis=0)
        bias_ref[:, :ext] = jnp.where(causal & (key >= thr), 0.0, NEG_BIG)

        def qk(h):
            kg = kb_ref[:ext, (h // hpg) * HEAD_DIM:(h // hpg + 1) * HEAD_DIM]
            return _bdot_nt(q[:, h * HEAD_DIM:(h + 1) * HEAD_DIM], kg)

        s_next = qk(0)
        for h in range(ATT_HEADS):
            s = s_next * scale + bias_ref[:, :ext]
            if h + 1 < ATT_HEADS:
                s_next = qk(h + 1)
            m = jnp.max(s, axis=1, keepdims=True)
            p = jnp.exp(s - m)
            l = jnp.sum(p, axis=1, keepdims=True)
            vg = vb_ref[:ext, (h // hpg) * HEAD_DIM:(h // hpg + 1) * HEAD_DIM]
            o = jnp.dot(p.astype(bf16), vg, preferred_element_type=f32)
            o_ref[:, h * HEAD_DIM:(h + 1) * HEAD_DIM] = o / l

    nvar = 4 if seq // nq % 4 == 0 else 1
    per = seq // nq // nvar
    for var in range(nvar):
        pl.when(qb // per == var)(functools.partial(process, (var + 1) * per * nq))


def _dsa_prompt(c, tabs, nb, seq):
    nq = Q_BLOCK
    nblk = seq // nq
    topk = min(TOPK_MAX, seq // 4)
    cosk, sink, cosi, sini = tabs
    row = lambda b, j: b * nblk + j
    in_specs = [
        pl.BlockSpec((nq, ATT_WIDTH), lambda b, j: (row(b, j), C_Q // ATT_WIDTH)),
        pl.BlockSpec((nq, IDX_HEADS * IDX_DIM), lambda b, j: (row(b, j), C_QI // (IDX_HEADS * IDX_DIM))),
        pl.BlockSpec((nq, LANES), lambda b, j: (row(b, j), C_WI // LANES)),
        pl.BlockSpec((seq, KV_WIDTH), lambda b, j: (b, C_AK // KV_WIDTH)),
        pl.BlockSpec((seq, KV_WIDTH), lambda b, j: (b, C_AV // KV_WIDTH)),
        pl.BlockSpec((seq, LANES), lambda b, j: (b, C_KI // LANES)),
    ] + [_const_spec((seq, LANES)) for _ in range(4)]
    out_specs = [
        pl.BlockSpec((nq, ATT_WIDTH), lambda b, j: (row(b, j), 0)),
        pl.BlockSpec((seq, KV_WIDTH), lambda b, j: (b, 0)),
        pl.BlockSpec((seq, IDX_DIM), lambda b, j: (b, 0)),
    ]
    scratch = [pltpu.VMEM((seq, KV_WIDTH), bf16), pltpu.VMEM((seq, KV_WIDTH), bf16), pltpu.VMEM((seq, LANES), bf16),
               pltpu.VMEM((nq, seq), f32)]
    return pl.pallas_call(
        functools.partial(_dsa_prompt_body, topk),
        grid=(nb, nblk),
        in_specs=in_specs,
        out_specs=out_specs,
        out_shape=[jax.ShapeDtypeStruct((nb * seq, ATT_WIDTH), f32),
                   jax.ShapeDtypeStruct((nb * seq, KV_WIDTH), f32),
                   jax.ShapeDtypeStruct((nb * seq, IDX_DIM), f32)],
        scratch_shapes=scratch,
        compiler_params=_cparams(("parallel", "arbitrary")),
        name="dsa_prompt",
    )(c, c, c, c, c, c, cosk, sink, cosi, sini)


ROWS = 8


def _page_copies(pt_ref, req, n_pages, srcs_dsts_sems):
    out = []
    for p in range(n_pages):
        pg = pt_ref[req, p]
        for hbm, dst, sem in srcs_dsts_sems:
            out.append(pltpu.make_async_copy(hbm.at[pg], dst(p), sem))
    return out


def _dsa_select_body(topk, n_new, group, pt_ref, qi_ref, wrow_ref, kin_ref, cosi_ref, sini_ref, cki_hbm,
                     sel_ref, kiout_ref, kibuf, sems):
    i = pl.program_id(0)
    nsteps = pl.num_programs(0)
    n_pages = pt_ref.shape[1]
    page = cki_hbm.shape[2]
    past = n_pages * page
    G = group

    def copies(step, slot):
        out = []
        for j in range(G):
            out += _page_copies(pt_ref, step * G + j, n_pages,
                                [(cki_hbm, lambda p, j=j: kibuf.at[slot, j, :, pl.ds(p * page, page)],
                                  sems.at[slot])])
        return out

    slot = i % 2

    @pl.when(i == 0)
    def _():
        for cp in copies(0, 0):
            cp.start()

    @pl.when(i + 1 < nsteps)
    def _():
        for cp in copies(i + 1, 1 - slot):
            cp.start()

    cosi, sini = cosi_ref[...], sini_ref[...]
    li = lax.broadcasted_iota(i32, (LANES, LANES), 0)
    lo = lax.broadcasted_iota(i32, (LANES, LANES), 1)
    eye = (li == lo).astype(f32)
    qis, kins, wcols = [], [], []
    for j in range(G):
        qi = _rope(qi_ref[j], cosi, sini, IDX_DIM // 2)
        kin = _rope(kin_ref[j], cosi, sini, IDX_DIM // 2)[:, :IDX_DIM]
        kiout_ref[j] = kin
        kins.append(jnp.concatenate([kin, jnp.zeros((LANES - ROWS, IDX_DIM), f32)], axis=0))
        qis.append(jnp.concatenate([qi[:, h * IDX_DIM:(h + 1) * IDX_DIM] for h in range(IDX_HEADS)], axis=0))
        wrow = wrow_ref[0, :, j * LANES:(j + 1) * LANES] * IDX_W_SCALE
        wcols.append(jnp.sum(eye * wrow, axis=1, keepdims=True))

    for cp in copies(i, slot):
        cp.wait()

    rows = []
    for j in range(G):
        d = jnp.concatenate([_bdot(qis[j], kibuf[slot, j]), _bdot_nt(qis[j], kins[j])], axis=1)
        term = jnp.maximum(d, 0.0) * wcols[j]
        rows.append(sum(term[h * ROWS:(h + 1) * ROWS] for h in range(IDX_HEADS)))
    sc = jnp.concatenate(rows, axis=0)
    nrow = ROWS * G
    width = past + LANES
    t_row = lax.broadcasted_iota(i32, (nrow, width), 0) % ROWS
    col = lax.broadcasted_iota(i32, (nrow, width), 1)
    valid = (col < past) | ((col - past <= t_row) & (col - past < n_new))
    key = jnp.where(valid, _sort_key(sc), INT_MIN)
    thr = jnp.concatenate(_kth_largest(_row_groups(key, G), float(topk)), axis=0)
    chosen = (valid & (key >= thr)).astype(f32)
    if nrow < LANES:
        chosen = jnp.concatenate([chosen, jnp.zeros((LANES - nrow, width), f32)], axis=0)
    sel_ref[0] = chosen.T


def _dsa_sample_body(n_new, group, pt_ref, q_ref, kn_ref, vn_ref, sel_ref, cosk_ref, sink_ref, ck_hbm, cv_hbm,
                     o_ref, kout_ref, kbuf, vbuf, sems):
    b = pl.program_id(0)
    nreq = pl.num_programs(0)
    n_pages = pt_ref.shape[1]
    prow = ck_hbm.shape[1]
    past = n_pages * prow // KV_HEADS
    G = group

    def copies(req, slot):
        return _page_copies(pt_ref, req, n_pages,
                            [(ck_hbm, lambda p: kbuf.at[slot, pl.ds(p * prow, prow)], sems.at[0, slot]),
                             (cv_hbm, lambda p: vbuf.at[slot, pl.ds(p * prow, prow)], sems.at[1, slot])])

    slot = b % 2

    @pl.when(b == 0)
    def _():
        for cp in copies(0, 0):
            cp.start()

    @pl.when(b + 1 < nreq)
    def _():
        for cp in copies(b + 1, 1 - slot):
            cp.start()

    cosk, sink = cosk_ref[...], sink_ref[...]
    q = _rope(q_ref[0], cosk, sink, HEAD_DIM // 2)
    kn = jnp.concatenate([_rope(kn_ref[0][:, h * HEAD_DIM:(h + 1) * HEAD_DIM], cosk, sink, HEAD_DIM // 2)
                          for h in range(KV_HEADS)], axis=1)
    vn = vn_ref[0]
    kout_ref[0] = kn

    li = lax.broadcasted_iota(i32, (LANES, LANES), 0)
    lo = lax.broadcasted_iota(i32, (LANES, LANES), 1)
    route = ((b % G) * ROWS + lo % ROWS == li) & (lo % ROWS < n_new)
    chosen = jnp.dot(sel_ref[0].astype(bf16), route.astype(bf16), preferred_element_type=f32)
    bias = (chosen - 1.0) * (-NEG_BIG)
    bias_p, bias_n = bias[:past], bias[past:past + ROWS]

    eye = (li == lo).astype(f32)
    scale = HEAD_DIM ** -0.5
    hpg = ATT_HEADS // KV_HEADS
    zrows = jnp.zeros((LANES - hpg * ROWS, HEAD_DIM), f32)

    for cp in copies(b, slot):
        cp.wait()

    for g in range(KV_HEADS):
        qg = jnp.concatenate([q[:, (g * hpg + r) * HEAD_DIM:(g * hpg + r + 1) * HEAD_DIM] for r in range(hpg)]
                             + [zrows], axis=0)
        gs = slice(g * HEAD_DIM, (g + 1) * HEAD_DIM)
        kp = kbuf[slot, pl.ds(g, past, stride=KV_HEADS), :]
        vp = vbuf[slot, pl.ds(g, past, stride=KV_HEADS), :]
        s_p = _bdot_nt(kp, qg) * scale + bias_p
        s_n = _bdot_nt(kn[:, gs], qg) * scale + bias_n
        m = jnp.maximum(jnp.max(s_p, axis=0, keepdims=True), jnp.max(s_n, axis=0, keepdims=True))
        p_p = jnp.exp(s_p - m)
        p_n = jnp.exp(s_n - m)
        l = jnp.sum(p_p, axis=0, keepdims=True) + jnp.sum(p_n, axis=0, keepdims=True)
        o = _bdot_tn(p_p, vp) + _bdot_tn(p_n, vn[:, gs])
        o = o / jnp.sum(eye * l, axis=1, keepdims=True)
        for r in range(hpg):
            h = g * hpg + r
            o_ref[0, :, h * HEAD_DIM:(h + 1) * HEAD_DIM] = o[r * ROWS:(r + 1) * ROWS]


def _dsa_sample(csel, wrow, tabs, page_table, cache_k, cache_v, cache_kidx, n_new):
    q8, qi8, kn8, vn8, kin8 = csel
    nreq = q8.shape[0]
    n_pages = page_table.shape[1]
    page = cache_kidx.shape[2]
    past = n_pages * page
    topk = min(TOPK_MAX, (past + n_new) // 4)
    group = min(LANES // ROWS, nreq)
    cosk, sink, cosi, sini = tabs
    anyspec = pl.BlockSpec(memory_space=pl.ANY)
    tab = pl.BlockSpec((ROWS, LANES), lambda b, pt: (0, 0))

    grp3 = lambda w: pl.BlockSpec((group, ROWS, w), lambda i, pt: (i, 0, 0))
    sel, ki_new = pl.pallas_call(
        functools.partial(_dsa_select_body, topk, n_new, group),
        grid_spec=pltpu.PrefetchScalarGridSpec(
            num_scalar_prefetch=1,
            grid=(nreq // group,),
            in_specs=[grp3(IDX_HEADS * IDX_DIM), pl.BlockSpec((1, 1, group * LANES), lambda i, pt: (i, 0, 0)),
                      grp3(LANES), tab, tab, anyspec],
            out_specs=[pl.BlockSpec((1, past + LANES, LANES), lambda i, pt: (i, 0, 0)), grp3(IDX_DIM)],
            scratch_shapes=[pltpu.VMEM((2, group, IDX_DIM, past), f32), pltpu.SemaphoreType.DMA((2,))],
        ),
        out_shape=[jax.ShapeDtypeStruct((nreq // group, past + LANES, LANES), f32),
                   jax.ShapeDtypeStruct((nreq, ROWS, IDX_DIM), f32)],
        compiler_params=_cparams(("arbitrary",)),
        name="dsa_select",
    )(page_table, qi8, wrow.reshape(nreq // group, 1, group * LANES), kin8, cosi, sini, cache_kidx)

    req3 = lambda w: pl.BlockSpec((1, ROWS, w), lambda b, pt: (b, 0, 0))
    prow = cache_k.shape[1]
    o, k_new = pl.pallas_call(
        functools.partial(_dsa_sample_body, n_new, group),
        grid_spec=pltpu.PrefetchScalarGridSpec(
            num_scalar_prefetch=1,
            grid=(nreq,),
            in_specs=[req3(ATT_WIDTH), req3(KV_WIDTH), req3(KV_WIDTH),
                      pl.BlockSpec((1, past + LANES, LANES), lambda b, pt: (b // group, 0, 0)), tab, tab,
                      anyspec, anyspec],
            out_specs=[req3(ATT_WIDTH), req3(KV_WIDTH)],
            scratch_shapes=[pltpu.VMEM((2, n_pages * prow, HEAD_DIM), f32),
                            pltpu.VMEM((2, n_pages * prow, HEAD_DIM), f32), pltpu.SemaphoreType.DMA((2, 2))],
        ),
        out_shape=[jax.ShapeDtypeStruct((nreq, ROWS, ATT_WIDTH), f32),
                   jax.ShapeDtypeStruct((nreq, ROWS, KV_WIDTH), f32)],
        compiler_params=_cparams(("arbitrary",)),
        name="dsa_sample",
    )(page_table, q8, kn8, vn8, sel, cosk, sink, cache_k, cache_v)
    return o, k_new, ki_new


def _merge_body(orw_ref, oatt_ref, grw_ref, gatt_ref, prw_ref, patt_ref, o_ref):
    a = jnp.dot(orw_ref[...].astype(bf16), prw_ref[...], preferred_element_type=f32)
    b = jnp.dot(oatt_ref[...].astype(bf16), patt_ref[...], preferred_element_type=f32)
    o_ref[...] = (jax.nn.sigmoid(grw_ref[...]) * a + jax.nn.sigmoid(gatt_ref[...]) * b).astype(bf16)


def _merge(o_rw, o_att, c, p_rw, p_att):
    m = o_rw.shape[0]
    d = p_rw.shape[1]
    tm = min(512, m)
    tn = 1024
    nj = d // tn
    return pl.pallas_call(
        _merge_body,
        grid=(m // tm, nj),
        in_specs=[
            pl.BlockSpec((tm, RW_WIDTH), lambda i, j: (i, 0)),
            pl.BlockSpec((tm, ATT_WIDTH), lambda i, j: (i, 0)),
            pl.BlockSpec((tm, tn), lambda i, j: (i, C_GRW // tn + j)),
            pl.BlockSpec((tm, tn), lambda i, j: (i, C_GATT // tn + j)),
            pl.BlockSpec((RW_WIDTH, tn), lambda i, j: (0, j)),
            pl.BlockSpec((ATT_WIDTH, tn), lambda i, j: (0, j)),
        ],
        out_specs=pl.BlockSpec((tm, tn), lambda i, j: (i, j)),
        out_shape=jax.ShapeDtypeStruct((m, d), bf16),
        compiler_params=_cparams(("parallel", "arbitrary")),
        name="merge",
    )(o_rw, o_att, c, c, p_rw, p_att)


def _outproj_body(mg_ref, x_ref, w_ref, o_ref):
    o_ref[...] = x_ref[...] + jnp.dot(mg_ref[...], w_ref[...], preferred_element_type=f32)


def _outproj(merged, x2d, w_o):
    m, d = x2d.shape
    tm = min(512, m)
    tn = 1024
    return pl.pallas_call(
        _outproj_body,
        grid=(m // tm, d // tn),
        in_specs=[
            pl.BlockSpec((tm, d), lambda i, j: (i, 0)),
            pl.BlockSpec((tm, tn), lambda i, j: (i, j)),
            pl.BlockSpec((d, tn), lambda i, j: (0, j)),
        ],
        out_specs=pl.BlockSpec((tm, tn), lambda i, j: (i, j)),
        out_shape=jax.ShapeDtypeStruct((m, d), f32),
        compiler_params=_cparams(("parallel", "arbitrary")),
        name="out_proj",
    )(merged, x2d, w_o)


def _convglu_body(shift, tiles_per_seq, final_norm, h_ref, g2_ref, gf_ref, cprev_ref, wg_ref, wu_ref, cw_ref,
                  cb_ref, wd_ref, o_ref, tail_ref, n_ref, ext_ref, carry_ref, act_ref):
    i = pl.program_id(0)
    j = pl.program_id(1)
    nj = pl.num_programs(1) - 1
    tm = h_ref.shape[0]
    hist = 2 * shift
    base = ext_ref.shape[0] - tm

    def up_stage():
        n = n_ref[...]
        gate = jnp.dot(n, wg_ref[...], preferred_element_type=f32)
        up = jnp.dot(n, wu_ref[...], preferred_element_type=f32)
        ext_ref[base - hist:base, :] = jnp.where(i % tiles_per_seq == 0, cprev_ref[0], carry_ref[j])
        ext_ref[base:, :] = gate
        cw = cw_ref[...]
        c = (cb_ref[...] + ext_ref[base - hist:base - hist + tm, :] * cw[0:1, :]
             + ext_ref[base - shift:base - shift + tm, :] * cw[1:2, :] + gate * cw[2:3, :])
        tail = ext_ref[base + tm - hist:base + tm, :]
        tail_ref[0] = tail
        carry_ref[j] = tail
        act_ref[j % 2] = ((c * jax.nn.sigmoid(c)) * up).astype(bf16)

    def down_stage():
        o_ref[...] += jnp.dot(act_ref[(j + 1) % 2], wd_ref[...], preferred_element_type=f32)

    @pl.when(j == 0)
    def _():
        @pl.when(i == 0)
        def _():
            carry_ref[...] = jnp.zeros_like(carry_ref)

        h = h_ref[...]
        ms = jnp.mean(h * h, axis=-1, keepdims=True)
        n_ref[...] = (h * lax.rsqrt(ms + RMS_EPS) * g2_ref[...]).astype(bf16)
        o_ref[...] = jnp.zeros_like(o_ref)
        up_stage()

    @pl.when((j > 0) & (j < nj))
    def _():
        up_stage()
        down_stage()

    @pl.when(j == nj)
    def _():
        down_stage()
        out = h_ref[...] + o_ref[...]
        if final_norm:
            ms = jnp.mean(out * out, axis=-1, keepdims=True)
            out = out * lax.rsqrt(ms + RMS_EPS) * gf_ref[...]
        o_ref[...] = out


def _convglu(h2d, conv_prev, norm2, norm_f, w_up_b, conv_w, conv_b, w_down_b, nseq_groups, shift, final_norm):
    m, d = h2d.shape
    d_ff = w_down_b.shape[0]
    rows_per_group = m // nseq_groups
    tm = 1024 if rows_per_group % 1024 == 0 else min(512, rows_per_group)
    tf = 512
    nj = d_ff // tf
    tiles_per_seq = rows_per_group // tm
    hist = 2 * shift
    base = ((hist + 7) // 8) * 8
    up_j = lambda j: jnp.minimum(j, nj - 1)
    down_j = lambda j: jnp.maximum(j - 1, 0)
    once = pl.Buffered(1)
    out, tails = pl.pallas_call(
        functools.partial(_convglu_body, shift, tiles_per_seq, final_norm),
        grid=(m // tm, nj + 1),
        in_specs=[
            pl.BlockSpec((tm, d), lambda i, j: (i, 0), pipeline_mode=once),
            pl.BlockSpec((1, d), lambda i, j: (0, 0)),
            pl.BlockSpec((1, d), lambda i, j: (0, 0)),
            pl.BlockSpec((1, hist, tf), lambda i, j: (i // tiles_per_seq, 0, up_j(j))),
            pl.BlockSpec((d, tf), lambda i, j: (0, up_j(j))),
            pl.BlockSpec((d, tf), lambda i, j: (0, nj + up_j(j))),
            pl.BlockSpec((CONV_W, tf), lambda i, j: (0, up_j(j))),
            pl.BlockSpec((1, tf), lambda i, j: (0, up_j(j))),
            pl.BlockSpec((tf, d), lambda i, j: (down_j(j), 0)),
        ],
        out_specs=[
            pl.BlockSpec((tm, d), lambda i, j: (i, 0), pipeline_mode=once),
            pl.BlockSpec((1, hist, tf), lambda i, j: (i, 0, up_j(j))),
        ],
        out_shape=[jax.ShapeDtypeStruct((m, d), f32), jax.ShapeDtypeStruct((m // tm, hist, d_ff), f32)],
        scratch_shapes=[pltpu.VMEM((tm, d), bf16), pltpu.VMEM((base + tm, tf), f32),
                        pltpu.VMEM((nj, hist, tf), f32), pltpu.VMEM((2, tm, tf), bf16)],
        compiler_params=_cparams(("arbitrary", "arbitrary")),
        name="convglu",
    )(h2d, norm2, norm_f, conv_prev, w_up_b, w_up_b, conv_w, conv_b, w_down_b)
    return out, tails[tiles_per_seq - 1::tiles_per_seq]


def _pad_cols(a, width):
    return jnp.pad(a, [(0, 0)] * (a.ndim - 1) + [(0, width - a.shape[-1])])


def _to_layout(a):
    rw = 3 * RW_WIDTH
    o = {}
    o["r"], o["k"], o["v"] = a[..., 0:RW_WIDTH], a[..., RW_WIDTH:2 * RW_WIDTH], a[..., 2 * RW_WIDTH:rw]
    p = rw
    o["wl"] = a[..., p:p + D_DECAY_LORA]; p += D_DECAY_LORA
    o["al"] = a[..., p:p + D_AAA_LORA]; p += D_AAA_LORA
    o["gl"] = a[..., p:p + D_GATE_LORA]; p += D_GATE_LORA
    o["q"] = a[..., p:p + ATT_WIDTH]; p += ATT_WIDTH
    o["ak"] = a[..., p:p + KV_WIDTH]; p += KV_WIDTH
    o["av"] = a[..., p:p + KV_WIDTH]; p += KV_WIDTH
    o["qi"] = a[..., p:p + IDX_HEADS * IDX_DIM]; p += IDX_HEADS * IDX_DIM
    o["ki"] = a[..., p:p + IDX_DIM]; p += IDX_DIM
    o["wi"] = a[..., p:p + IDX_HEADS]; p += IDX_HEADS
    d = (a.shape[-1] - p) // 2
    o["grw"], o["gatt"] = a[..., p:p + d], a[..., p + d:p + 2 * d]
    return jnp.concatenate([
        o["r"], o["k"], o["v"], o["q"], o["grw"], o["gatt"], o["qi"],
        _pad_cols(o["wl"], LANES), _pad_cols(o["al"], LANES), o["gl"],
        _pad_cols(o["ki"], LANES), o["ak"], o["av"], _pad_cols(o["wi"], LANES)], axis=-1)


def _rw_cols_layout(a):
    rw = 3 * RW_WIDTH
    wl = a[..., rw:rw + D_DECAY_LORA]
    al = a[..., rw + D_DECAY_LORA:rw + D_DECAY_LORA + D_AAA_LORA]
    gl = a[..., rw + D_DECAY_LORA + D_AAA_LORA:]
    return a[..., :rw], jnp.concatenate([_pad_cols(wl, LANES), _pad_cols(al, LANES), gl], axis=-1)


def _rw_cols_from_layout(c):
    return jnp.concatenate([c[..., :3 * RW_WIDTH], c[..., C_LORA:C_LORA + D_DECAY_LORA],
                            c[..., C_LORA + LANES:C_LORA + LANES + D_AAA_LORA],
                            c[..., C_LORA + 2 * LANES:C_LORA + 3 * LANES]], axis=-1)


def _rope_tables(pos, rows):
    pos = jnp.pad(pos.astype(f32), (0, rows - pos.shape[0]))
    out = []
    for dim in (HEAD_DIM, IDX_DIM):
        half = dim // 2
        inv_freq = 1.0 / (ROPE_THETA ** (jnp.arange(half, dtype=f32) / half))
        ang = pos[:, None] * inv_freq[None, :]
        cos, sin = jnp.cos(ang), jnp.sin(ang)
        reps = LANES // dim
        out.append(jnp.tile(jnp.concatenate([cos, cos], axis=1), (1, reps)))
        out.append(jnp.tile(jnp.concatenate([-sin, sin], axis=1), (1, reps)))
    return tuple(out)


def _pair_unblock(s):
    n = s.shape[0]
    a = s[:, :, :RW_HEAD, :RW_HEAD]
    b = s[:, :, RW_HEAD:, RW_HEAD:]
    return jnp.stack([a, b], axis=2).reshape(n, RW_HEADS, RW_HEAD, RW_HEAD)


def kernel(x_prompt, x_sample, cache_k, cache_v, cache_kidx, state_wkv, state_shift, state_conv, page_table, norm1, w_in, rw_mu, rw_w0, rw_w2, rw_a0, rw_a2, rw_g2, rw_k_k, rw_k_a, rw_r_k, rw_lnx_w, rw_lnx_b, p_rw, p_att, w_o, norm2, w_up, conv_w, conv_b, w_down, norm_f):
    B, S, D = x_prompt.shape
    DB, DS, _ = x_sample.shape
    depth = w_in.shape[0]
    page = cache_k.shape[2]
    n_pages = page_table.shape[1]
    past_len = n_pages * page
    d_ff = w_down.shape[1]
    dt = x_prompt.dtype

    tabs_p = _rope_tables(jnp.arange(S), S)
    tabs_s = _rope_tables(past_len + jnp.arange(DS), ROWS)
    row2 = lambda a: a.reshape(1, -1)

    hp = x_prompt.reshape(B * S, D)
    hs = x_sample.reshape(DB * DS, D)
    outs = {k: [] for k in ("kp", "vp", "kip", "ks", "vs", "kis", "wkvp", "wkvs", "shp", "shs", "cvp", "cvs")}
    for l in range(depth):
        w_in_p = _to_layout(w_in[l]).astype(bf16)
        mu_rkv, mu_lora = _rw_cols_layout(rw_mu[l][None, :])
        params = dict(
            mu_rkv=mu_rkv, mu_lora=mu_lora, w0=row2(rw_w0[l]),
            w2=jnp.pad(rw_w2[l], ((0, LANES - D_DECAY_LORA), (0, 0))).astype(bf16), a0=row2(rw_a0[l]),
            a2=jnp.pad(rw_a2[l], ((0, LANES - D_AAA_LORA), (0, 0))).astype(bf16), g2=rw_g2[l].astype(bf16),
            k_k=row2(rw_k_k[l]), k_a=row2(rw_k_a[l]), r_k=row2(rw_r_k[l]), lnx_w=row2(rw_lnx_w[l]),
            lnx_b=row2(rw_lnx_b[l]))
        last = l == depth - 1

        c_p = _in_proj(hp, row2(norm1[l]), w_in_p)
        c_s = _in_proj(hs, row2(norm1[l]), w_in_p)

        sp_rkv = jnp.zeros((B, 1, RKV_W), dt)
        sp_lora = jnp.zeros((B, 1, LORA_W), dt)
        s0_p = jnp.zeros((B, RW_HEADS // 2, LANES, LANES), dt)
        o_rw_p, wkv_p = _rwkv_prompt(c_p, sp_rkv, sp_lora, s0_p, params, B, S)

        c_s3 = c_s.reshape(DB, DS, NP_COLS)
        ss_rkv, ss_lora = _rw_cols_layout(state_shift[l])
        prev_rkv = jnp.concatenate([ss_rkv[:, None, :], c_s3[:, :-1, :RKV_W]], axis=1).reshape(DB * DS, RKV_W)
        prev_lora = jnp.concatenate([ss_lora[:, None, :], c_s3[:, :-1, C_LORA:C_LORA + LORA_W]], axis=1)
        prev_lora = prev_lora.reshape(DB * DS, LORA_W)
        r_s, lw_s, k_s, v_s, a_s, b_s, g_s = _rwkv_prep(c_s, prev_rkv, prev_lora, params)
        y_s, wkv_s = _rwkv_seq((r_s, lw_s, k_s, v_s, a_s, b_s), state_wkv[l], DB, DS)
        o_rw_s = _rwkv_post(y_s, r_s, k_s, v_s, g_s, params)

        o_att_p, k_p, ki_p = _dsa_prompt(c_p, tabs_p, B, S)

        def rows8(lo, w):
            return jnp.pad(c_s3[:, :, lo:lo + w], ((0, 0), (0, ROWS - DS), (0, 0)))

        csel = (rows8(C_Q, ATT_WIDTH), rows8(C_QI, IDX_HEADS * IDX_DIM), rows8(C_AK, KV_WIDTH),
                rows8(C_AV, KV_WIDTH), rows8(C_KI, LANES))
        wi_s = jnp.pad(c_s3[:, :, C_WI:C_WI + IDX_HEADS], ((0, 0), (0, ROWS - DS), (0, 0)))
        wrow = jnp.transpose(wi_s, (0, 2, 1)).reshape(DB, 1, IDX_HEADS * ROWS)
        o_att_s8, k_s8, ki_s8 = _dsa_sample(
            csel, wrow, tabs_s, page_table, cache_k[l].reshape(-1, page * KV_HEADS, HEAD_DIM),
            cache_v[l].reshape(-1, page * KV_HEADS, HEAD_DIM), jnp.swapaxes(cache_kidx[l], 1, 2), DS)
        o_att_s = o_att_s8[:, :DS].reshape(DB * DS, ATT_WIDTH)

        p_rw_b, p_att_b, w_o_b = p_rw[l].astype(bf16), p_att[l].astype(bf16), w_o[l].astype(bf16)
        h_p = _outproj(_merge(o_rw_p, o_att_p, c_p, p_rw_b, p_att_b), hp, w_o_b)
        h_s = _outproj(_merge(o_rw_s, o_att_s, c_s, p_rw_b, p_att_b), hs, w_o_b)

        w_up_b, w_down_b = w_up[l].astype(bf16), w_down[l].astype(bf16)
        cv_args = (row2(norm2[l]), row2(norm_f), w_up_b, conv_w[l], row2(conv_b[l]), w_down_b)
        hp, tail_p = _convglu(h_p, jnp.zeros((B, CONV_W - 1, d_ff), dt), *cv_args, B, 1, last)
        h_s_tm = h_s.reshape(DB, DS, D).transpose(1, 0, 2).reshape(DS * DB, D)
        cprev_tm = state_conv[l].transpose(1, 0, 2).reshape(1, (CONV_W - 1) * DB, d_ff)
        hs_tm, tail_s = _convglu(h_s_tm, cprev_tm, *cv_args, 1, DB, last)
        hs = hs_tm.reshape(DS, DB, D).transpose(1, 0, 2).reshape(DB * DS, D)

        outs["kp"].append(k_p.reshape(B, S // page, page, KV_HEADS, HEAD_DIM))
        outs["vp"].append(c_p[:, C_AV:C_AV + KV_WIDTH].reshape(B, S // page, page, KV_HEADS, HEAD_DIM))
        outs["kip"].append(ki_p.reshape(B, S // page, page, IDX_DIM))
        outs["ks"].append(k_s8[:, :DS].reshape(DB, DS, KV_HEADS, HEAD_DIM))
        outs["vs"].append(c_s3[:, :, C_AV:C_AV + KV_WIDTH].reshape(DB, DS, KV_HEADS, HEAD_DIM))
        outs["kis"].append(ki_s8[:, :DS])
        outs["wkvp"].append(_pair_unblock(wkv_p))
        outs["wkvs"].append(wkv_s)
        outs["shp"].append(_rw_cols_from_layout(c_p.reshape(B, S, NP_COLS)[:, -1]))
        outs["shs"].append(_rw_cols_from_layout(c_s3[:, -1]))
        outs["cvp"].append(tail_p)
        outs["cvs"].append(tail_s.reshape(CONV_W - 1, DB, d_ff).transpose(1, 0, 2))

    y_prompt = hp.reshape(B, S, D)
    y_sample = hs.reshape(DB, DS, D)
    st = lambda k: jnp.stack(outs[k])
    return (y_prompt, y_sample, st("kp"), st("vp"), st("kip"), st("ks"), st("vs"), st("kis"), st("wkvp"),
            st("wkvs"), st("shp"), st("shs"), st("cvp"), st("cvs"))
```

```python
import functools
import math

import numpy as np
import jax
import jax.numpy as jnp
from jax import lax
from jax.experimental import pallas as pl
from jax.experimental.pallas import tpu as pltpu

f32 = jnp.float32
bf16 = jnp.bfloat16
i32 = jnp.int32

RW_HEADS = 16
RW_HEAD = 64
RW_WIDTH = RW_HEADS * RW_HEAD
D_DECAY_LORA = 96
D_AAA_LORA = 96
D_GATE_LORA = 128
GN_EPS = 64e-5
ATT_HEADS = 8
KV_HEADS = 2
HEAD_DIM = 128
ATT_WIDTH = ATT_HEADS * HEAD_DIM
KV_WIDTH = KV_HEADS * HEAD_DIM
IDX_HEADS = 16
IDX_DIM = 64
IDX_W_SCALE = (IDX_HEADS * IDX_DIM) ** -0.5
TOPK_MAX = 256
Q_BLOCK = 128
ROPE_THETA = 10000.0
CONV_W = 3
RMS_EPS = 1e-6

LANES = 128
NEG_BIG = -1e30
INT_MIN = -(2 ** 31)

C_R, C_K, C_V, C_Q, C_GRW, C_GATT, C_QI = 0, 1024, 2048, 3072, 4096, 6144, 8192
C_LORA = 9216
C_KI, C_AK, C_AV, C_WI = 9600, 9728, 9984, 10240
NP_COLS = 10368
LORA_W = 384
RKV_W = 3 * RW_WIDTH

VMEM_LIMIT = 56 * 1024 * 1024


def _cparams(sem):
    return pltpu.CompilerParams(dimension_semantics=sem, vmem_limit_bytes=VMEM_LIMIT)


def _bdot(a, b):
    return jnp.dot(a.astype(bf16), b.astype(bf16), preferred_element_type=f32)


def _bdot_nt(a, b):
    return lax.dot_general(a.astype(bf16), b.astype(bf16), (((1,), (1,)), ((), ())), preferred_element_type=f32)


def _bdot_tn(a, b):
    return lax.dot_general(a.astype(bf16), b.astype(bf16), (((0,), (0,)), ((), ())), preferred_element_type=f32)


def _split3(x):
    x1 = x.astype(bf16)
    r1 = x - x1.astype(f32)
    x2 = r1.astype(bf16)
    x3 = (r1 - x2.astype(f32)).astype(bf16)
    return x1, x2, x3


def _split2(x):
    x1 = x.astype(bf16)
    return x1, (x - x1.astype(f32)).astype(bf16)


def _headsum(x):
    ri = lax.broadcasted_iota(i32, (LANES, LANES), 0) // RW_HEAD
    ci = lax.broadcasted_iota(i32, (LANES, LANES), 1) // RW_HEAD
    bd = (ri == ci).astype(bf16)
    outs = []
    for i in range(x.shape[1] // LANES):
        hi, lo = _split2(x[:, i * LANES:(i + 1) * LANES])
        outs.append(jnp.dot(hi, bd, preferred_element_type=f32) + jnp.dot(lo, bd, preferred_element_type=f32))
    return jnp.concatenate(outs, axis=1)


def _softplus(x):
    return jnp.maximum(x, 0.0) + jnp.log(1.0 + jnp.exp(-jnp.abs(x)))


def _rope(x, cos, sin, half):
    w = x.shape[1]
    reps = w // LANES
    if reps > 1:
        cos = jnp.concatenate([cos] * reps, axis=1)
        sin = jnp.concatenate([sin] * reps, axis=1)
    if 2 * half == LANES and w == LANES:
        partner = pltpu.roll(x, half, axis=1)
    else:
        lane = lax.broadcasted_iota(i32, (1, w), 1)
        first = (lane % (2 * half)) < half
        partner = jnp.where(first, pltpu.roll(x, w - half, axis=1), pltpu.roll(x, half, axis=1))
    return x * cos + partner * sin


def _inproj_body(x_ref, g_ref, w_ref, o_ref, n_ref):
    @pl.when(pl.program_id(1) == 0)
    def _():
        x = x_ref[...]
        ms = jnp.mean(x * x, axis=-1, keepdims=True)
        n_ref[...] = (x * lax.rsqrt(ms + RMS_EPS) * g_ref[...]).astype(bf16)

    o_ref[...] = lax.dot_general(n_ref[...], w_ref[...], (((1,), (1,)), ((), ())), preferred_element_type=f32)


def _in_proj(x2d, gain, w_t):
    m, d = x2d.shape
    tm = 1024 if m % 1024 == 0 else min(512, m)
    tn = NP_COLS // 9
    return pl.pallas_call(
        _inproj_body,
        grid=(m // tm, NP_COLS // tn),
        in_specs=[
            pl.BlockSpec((tm, d), lambda i, j: (i, 0)),
            pl.BlockSpec((1, d), lambda i, j: (0, 0)),
            pl.BlockSpec((tn, d), lambda i, j: (j, 0)),
        ],
        out_specs=pl.BlockSpec((tm, tn), lambda i, j: (i, j)),
        out_shape=jax.ShapeDtypeStruct((m, NP_COLS), f32),
        scratch_shapes=[pltpu.VMEM((tm, d), bf16)],
        compiler_params=_cparams(("parallel", "arbitrary")),
        name="in_proj",
    )(x2d, gain, w_t)


RW_PARAM_NAMES = ("mu_rkv", "mu_lora", "w0", "w2", "a0", "a2", "g2", "k_k", "k_a", "r_k", "lnx_w", "lnx_b")


def _rw_prep(c_rkv, c_lora, p_rkv, p_lora, P):
    m = c_rkv + (p_rkv - c_rkv) * P["mu_rkv"]
    ml = c_lora + (p_lora - c_lora) * P["mu_lora"]
    r, k, v = m[:, :RW_WIDTH], m[:, RW_WIDTH:2 * RW_WIDTH], m[:, 2 * RW_WIDTH:]
    wl, al, gl = ml[:, :LANES], ml[:, LANES:2 * LANES], ml[:, 2 * LANES:]
    w_log = -_softplus(-(P["w0"] + _bdot(jnp.tanh(wl), P["w2"]))) - 0.5
    logw = -jnp.exp(w_log)
    asig = jax.nn.sigmoid(P["a0"] + _bdot(al, P["a2"]))
    g = _bdot(jax.nn.sigmoid(gl), P["g2"])
    kk = k * P["k_k"]
    kkn = kk * lax.rsqrt(jnp.maximum(_headsum(kk * kk), 1e-24))
    k2 = k * (1.0 + (asig - 1.0) * P["k_a"])
    return r, logw, k2, v, -kkn, kkn * asig, g


def _rw_post(y, r, k2, v, g, P):
    inv_n = 1.0 / RW_HEAD
    mean = _headsum(y) * inv_n
    d = y - mean
    var = _headsum(d * d) * inv_n
    yn = d * lax.rsqrt(var + GN_EPS) * P["lnx_w"] + P["lnx_b"]
    bonus = _headsum(r * k2 * P["r_k"]) * v
    return (yn + bonus) * g


def _chunk_step(at, rt, bt, kt, be, ke, v, wc, S, masks, nsteps):
    keep, eye2, colh, mA, bdmask = masks
    R = range(len(at))
    C = at[0].shape[0]
    X = [jnp.concatenate([at[p], rt[p]], axis=0) for p in R]
    scA = [jnp.where(keep, _bdot_nt(jnp.where(mA, X[p], 0.0), jnp.concatenate([bt[p], kt[p]], axis=0)), 0.0)
           for p in R]
    scB = [jnp.where(keep, _bdot_nt(jnp.where(mA, 0.0, X[p]), jnp.concatenate([kt[p], bt[p]], axis=0)), 0.0)
           for p in R]
    XS = [_bdot_nt(X[p], S[p]) for p in R]

    def bdiag(x):
        return jnp.concatenate([jnp.where(colh, x, 0.0), jnp.where(colh, 0.0, x)], axis=0)

    L = [jnp.where(colh, scA[p][:C], scB[p][:C]) for p in R]
    T = [eye2 + L[p] for p in R]
    if nsteps > 0:
        Pw = [_bdot(L[p], bdiag(L[p])) for p in R]
        for i in range(nsteps):
            if i < nsteps - 1:
                Z = [_bdot(Pw[p], jnp.concatenate([bdiag(T[p]), bdiag(Pw[p])], axis=1)) for p in R]
                T = [T[p] + Z[p][:, :2 * C] for p in R]
                Pw = [Z[p][:, 2 * C:] for p in R]
            else:
                T = [T[p] + _bdot(Pw[p], bdiag(T[p])) for p in R]
    vA = [jnp.where(mA, v[p], 0.0) for p in R]
    vB = [jnp.where(mA, 0.0, v[p]) for p in R]
    G = [XS[p][:C] + _bdot(jnp.where(colh, scB[p][:C], scA[p][:C]), jnp.concatenate([vB[p], vA[p]], axis=0))
         for p in R]
    U = [_bdot(T[p], jnp.concatenate([jnp.where(mA, G[p], 0.0), jnp.where(mA, 0.0, G[p])], axis=0)) for p in R]
    Y = [XS[p][C:] + _bdot(jnp.concatenate([scA[p][C:], scB[p][C:]], axis=1),
                           jnp.concatenate([jnp.where(mA, U[p], 0.0), vA[p], vB[p], jnp.where(mA, 0.0, U[p])], axis=0))
         for p in R]
    dS = [_bdot_tn(jnp.concatenate([U[p], v[p]], axis=0), jnp.concatenate([be[p], ke[p]], axis=0)) for p in R]
    S_new = [S[p] * wc[p] + jnp.where(bdmask, dS[p], 0.0) for p in R]
    return Y, S_new


def _chunk_masks(C):
    row = lax.broadcasted_iota(i32, (2 * C, 2 * C), 0)
    col = lax.broadcasted_iota(i32, (2 * C, 2 * C), 1)
    t = jnp.where(row >= C, row - C, row)
    s = jnp.where(col >= C, col - C, col)
    keep = (s < t) | ((row >= C) & (s == t))
    hrow = lax.broadcasted_iota(i32, (C, 2 * C), 0)
    hcol = lax.broadcasted_iota(i32, (C, 2 * C), 1)
    eye2 = (hrow == jnp.where(hcol >= C, hcol - C, hcol)).astype(f32)
    colh = hcol < C
    mA = lax.broadcasted_iota(i32, (1, LANES), 1) < RW_HEAD
    r2 = lax.broadcasted_iota(i32, (LANES, LANES), 0) // RW_HEAD
    c2 = lax.broadcasted_iota(i32, (LANES, LANES), 1) // RW_HEAD
    return keep, eye2, colh, mA, r2 == c2


def _rwkv_prompt_body(chunk, rkv_ref, lora_ref, sp_rkv_ref, sp_lora_ref, s0_ref, *rest):
    np_ = len(RW_PARAM_NAMES)
    P = {n: rest[i][...] for i, n in enumerate(RW_PARAM_NAMES)}
    o_ref, sout_ref = rest[np_], rest[np_ + 1]
    (S_ref, car_rkv, car_lora, at_s, rt_s, bt_s, kt_s, be_s, ke_s, v_s, cum_s, y_s) = rest[np_ + 2:]
    t = pl.program_id(1)
    tt = rkv_ref.shape[0]
    C = chunk

    @pl.when(t == 0)
    def _():
        S_ref[...] = s0_ref[0]
        car_rkv[...] = sp_rkv_ref[0]
        car_lora[...] = sp_lora_ref[0]

    c_rkv = rkv_ref[...]
    c_lora = lora_ref[...]
    first = lax.broadcasted_iota(i32, (tt, 1), 0) == 0
    p_rkv = jnp.where(first, car_rkv[...], pltpu.roll(c_rkv, 1, axis=0))
    p_lora = jnp.where(first, car_lora[...], pltpu.roll(c_lora, 1, axis=0))
    car_rkv[...] = c_rkv[tt - 1:tt, :]
    car_lora[...] = c_lora[tt - 1:tt, :]
    r, logw, k2, v, a, b, g = _rw_prep(c_rkv, c_lora, p_rkv, p_lora, P)

    ri = lax.broadcasted_iota(i32, (tt, tt), 0)
    ci = lax.broadcasted_iota(i32, (tt, tt), 1)
    same = ri // C == ci // C
    tril = (same & (ci <= ri)).astype(bf16)
    triu = (same & (ci > ri)).astype(bf16)
    pieces = _split3(logw)
    cum = sum(jnp.dot(tril, piece, preferred_element_type=f32) for piece in pieces)
    rev = sum(jnp.dot(triu, piece, preferred_element_type=f32) for piece in pieces)
    iw = jnp.exp(-cum)
    ew = jnp.exp(rev)
    at_s[...] = a * jnp.exp(cum - logw)
    rt_s[...] = r * jnp.exp(cum)
    bt_s[...] = b * iw
    kt_s[...] = k2 * iw
    be_s[...] = b * ew
    ke_s[...] = k2 * ew
    v_s[...] = v
    cum_s[...] = cum
    masks = _chunk_masks(C)
    nsteps = max(int(math.ceil(math.log2(C))) - 1, 0)
    pairs = range(RW_WIDTH // LANES)

    def chunk_body(ci_, carry):
        r0 = pl.multiple_of(ci_ * C, C)
        lanes = [slice(p * LANES, (p + 1) * LANES) for p in pairs]
        ld = lambda ref: [ref[pl.ds(r0, C), lanes[p]] for p in pairs]
        last8 = pl.multiple_of(r0 + C - 8, 8)
        wc = [jnp.exp(cum_s[pl.ds(last8, 8), lanes[p]][7:8]) for p in pairs]
        Y, S_new = _chunk_step(ld(at_s), ld(rt_s), ld(bt_s), ld(kt_s), ld(be_s), ld(ke_s), ld(v_s), wc,
                               [S_ref[p] for p in pairs], masks, nsteps)
        for p in pairs:
            y_s[pl.ds(r0, C), lanes[p]] = Y[p]
            S_ref[p] = S_new[p]
        return carry

    lax.fori_loop(0, tt // C, chunk_body, 0)
    o_ref[...] = _rw_post(y_s[...], r, k2, v, g, P)

    @pl.when(t == pl.num_programs(1) - 1)
    def _():
        sout_ref[0] = S_ref[...]


def _const_spec(shape):
    nd = len(shape)
    return pl.BlockSpec(shape, lambda *_: (0,) * nd)


def _rwkv_prompt(c, sp_rkv, sp_lora, s0, params, nb, seq):
    tt = min(256, seq)
    chunk = min(64, tt)
    nt = seq // tt
    npairs = RW_WIDTH // LANES
    in_specs = [
        pl.BlockSpec((tt, RKV_W), lambda b, t: (b * nt + t, 0)),
        pl.BlockSpec((tt, LORA_W), lambda b, t: (b * nt + t, C_LORA // LORA_W)),
        pl.BlockSpec((1, 1, RKV_W), lambda b, t: (b, 0, 0)),
        pl.BlockSpec((1, 1, LORA_W), lambda b, t: (b, 0, 0)),
        pl.BlockSpec((1, npairs, LANES, LANES), lambda b, t: (b, 0, 0, 0)),
    ] + [_const_spec(params[n].shape) for n in RW_PARAM_NAMES]
    out_specs = [
        pl.BlockSpec((tt, RW_WIDTH), lambda b, t: (b * nt + t, 0)),
        pl.BlockSpec((1, npairs, LANES, LANES), lambda b, t: (b, 0, 0, 0)),
    ]
    scratch = [pltpu.VMEM((npairs, LANES, LANES), f32), pltpu.VMEM((1, RKV_W), f32), pltpu.VMEM((1, LORA_W), f32)]
    scratch += [pltpu.VMEM((tt, RW_WIDTH), f32) for _ in range(9)]
    return pl.pallas_call(
        functools.partial(_rwkv_prompt_body, chunk),
        grid=(nb, nt),
        in_specs=in_specs,
        out_specs=out_specs,
        out_shape=[jax.ShapeDtypeStruct((nb * seq, RW_WIDTH), f32),
                   jax.ShapeDtypeStruct((nb, npairs, LANES, LANES), f32)],
        scratch_shapes=scratch,
        compiler_params=_cparams(("parallel", "arbitrary")),
        name="rwkv_prompt",
    )(c, c, sp_rkv, sp_lora, s0, *[params[n] for n in RW_PARAM_NAMES])


def _rwkv_prep_body(rkv_ref, lora_ref, prkv_ref, plora_ref, *rest):
    np_ = len(RW_PARAM_NAMES)
    P = {n: rest[i][...] for i, n in enumerate(RW_PARAM_NAMES)}
    outs = rest[np_:]
    vals = _rw_prep(rkv_ref[...], lora_ref[...], prkv_ref[...], plora_ref[...], P)
    for o, v in zip(outs, vals):
        o[...] = v


def _rwkv_prep(c, prev_rkv, prev_lora, params):
    m = c.shape[0]
    tm = min(512, m)
    in_specs = [
        pl.BlockSpec((tm, RKV_W), lambda i: (i, 0)),
        pl.BlockSpec((tm, LORA_W), lambda i: (i, C_LORA // LORA_W)),
        pl.BlockSpec((tm, RKV_W), lambda i: (i, 0)),
        pl.BlockSpec((tm, LORA_W), lambda i: (i, 0)),
    ] + [_const_spec(params[n].shape) for n in RW_PARAM_NAMES]
    return pl.pallas_call(
        _rwkv_prep_body,
        grid=(m // tm,),
        in_specs=in_specs,
        out_specs=[pl.BlockSpec((tm, RW_WIDTH), lambda i: (i, 0)) for _ in range(7)],
        out_shape=[jax.ShapeDtypeStruct((m, RW_WIDTH), f32) for _ in range(7)],
        compiler_params=_cparams(("parallel",)),
        name="rwkv_prep",
    )(c, c, prev_rkv, prev_lora, *[params[n] for n in RW_PARAM_NAMES])


def _rwkv_seq_body(r_ref, lw_ref, k_ref, v_ref, a_ref, b_ref, s_ref, y_ref, sout_ref):
    steps = r_ref.shape[0]
    w = [jnp.exp(lw_ref[t]) for t in range(steps)]
    for i in range(RW_HEAD):
        Si = s_ref[0, i]
        for t in range(steps):
            sa = jnp.sum(Si * a_ref[t], axis=0, keepdims=True)
            Si = Si * w[t] + sa * b_ref[t] + v_ref[t, i:i + 1, :] * k_ref[t]
            y_ref[t, i:i + 1, :] = jnp.sum(Si * r_ref[t], axis=0, keepdims=True)
        sout_ref[0, i] = Si


def _rwkv_seq(ops, s0, nreq, steps):
    ops_t = [o.reshape(nreq, steps, RW_WIDTH).transpose(1, 2, 0) for o in ops]
    vec = pl.BlockSpec((steps, RW_HEAD, nreq), lambda h: (0, h, 0))
    st = pl.BlockSpec((1, RW_HEAD, RW_HEAD, nreq), lambda h: (h, 0, 0, 0))
    y_t, s_t = pl.pallas_call(
        _rwkv_seq_body,
        grid=(RW_HEADS,),
        in_specs=[vec] * 6 + [st],
        out_specs=[vec, st],
        out_shape=[jax.ShapeDtypeStruct((steps, RW_WIDTH, nreq), f32),
                   jax.ShapeDtypeStruct((RW_HEADS, RW_HEAD, RW_HEAD, nreq), f32)],
        compiler_params=_cparams(("parallel",)),
        name="rwkv_seq",
    )(*ops_t, s0.transpose(1, 2, 3, 0))
    return y_t.transpose(2, 0, 1).reshape(nreq * steps, RW_WIDTH), s_t.transpose(3, 0, 1, 2)


def _rwkv_post_body(y_ref, r_ref, k_ref, v_ref, g_ref, *rest):
    np_ = len(RW_PARAM_NAMES)
    P = {n: rest[i][...] for i, n in enumerate(RW_PARAM_NAMES)}
    rest[np_][...] = _rw_post(y_ref[...], r_ref[...], k_ref[...], v_ref[...], g_ref[...], P)


def _rwkv_post(y, r, k2, v, g, params):
    m = y.shape[0]
    tm = min(512, m)
    return pl.pallas_call(
        _rwkv_post_body,
        grid=(m // tm,),
        in_specs=[pl.BlockSpec((tm, RW_WIDTH), lambda i: (i, 0)) for _ in range(5)]
        + [_const_spec(params[n].shape) for n in RW_PARAM_NAMES],
        out_specs=pl.BlockSpec((tm, RW_WIDTH), lambda i: (i, 0)),
        out_shape=jax.ShapeDtypeStruct((m, RW_WIDTH), f32),
        compiler_params=_cparams(("parallel",)),
        name="rwkv_post",
    )(y, r, k2, v, g, *[params[n] for n in RW_PARAM_NAMES])


def _sort_key(score):
    bits = pltpu.bitcast(score, i32)
    key = jnp.where(bits < 0, bits ^ jnp.int32(0x7FFFFFFF), bits)
    return jnp.where(score == 0.0, jnp.int32(0), key)


def _kth_largest(keys, k):
    def body(i, ts):
        bit = lax.shift_left(jnp.int32(1), jnp.int32(31) - i)
        cands = [t + bit for t in ts]
        cnts = [jnp.sum((key >= c).astype(f32), axis=1, keepdims=True) for key, c in zip(keys, cands)]
        return tuple(jnp.where(n >= k, c, t) for n, c, t in zip(cnts, cands, ts))

    init = tuple(jnp.full((key.shape[0], 1), INT_MIN, i32) for key in keys)
    return lax.fori_loop(0, 32, body, init, unroll=4)


def _row_groups(x, n):
    step = x.shape[0] // n
    return [x[g * step:(g + 1) * step] for g in range(n)]


def _dsa_prompt_body(topk, q_ref, qi_ref, wi_ref, k_ref, v_ref, ki_ref, cosk_ref, sink_ref, cosi_ref, sini_ref,
                     o_ref, kout_ref, kiout_ref, kb_ref, vb_ref, kib_ref, bias_ref):
    qb = pl.program_id(1)
    seq = k_ref.shape[0]
    nq = q_ref.shape[0]

    @pl.when(qb == 0)
    def _():
        cos, sin = cosk_ref[...], sink_ref[...]
        kr = jnp.concatenate([_rope(k_ref[:, h * HEAD_DIM:(h + 1) * HEAD_DIM], cos, sin, HEAD_DIM // 2)
                              for h in range(KV_HEADS)], axis=1)
        kout_ref[...] = kr
        kb_ref[...] = kr.astype(bf16)
        vb_ref[...] = v_ref[...].astype(bf16)
        kir = _rope(ki_ref[...], cosi_ref[...], sini_ref[...], IDX_DIM // 2)
        kiout_ref[...] = kir[:, :IDX_DIM]
        kib_ref[...] = (kir + pltpu.roll(kir, IDX_DIM, axis=1)).astype(bf16)

    r0 = pl.multiple_of(qb * nq, nq)
    cosq, sinq = cosk_ref[pl.ds(r0, nq), :], sink_ref[pl.ds(r0, nq), :]
    cosqi, sinqi = cosi_ref[pl.ds(r0, nq), :], sini_ref[pl.ds(r0, nq), :]
    q = _rope(q_ref[...], cosq, sinq, HEAD_DIM // 2).astype(bf16)
    qi = _rope(qi_ref[...], cosqi, sinqi, IDX_DIM // 2)
    wi = wi_ref[...] * IDX_W_SCALE
    lane = lax.broadcasted_iota(i32, (1, LANES), 1)
    scale = HEAD_DIM ** -0.5
    hpg = ATT_HEADS // KV_HEADS

    def process(ext):
        kib = kib_ref[:ext, :]
        sc = None
        for h in range(IDX_HEADS):
            pair = qi[:, (h // 2) * LANES:(h // 2 + 1) * LANES]
            mine = (lane < IDX_DIM) if h % 2 == 0 else (lane >= IDX_DIM)
            term = jnp.maximum(_bdot_nt(jnp.where(mine, pair, 0.0), kib), 0.0) * wi[:, h:h + 1]
            sc = term if sc is None else sc + term

        qpos = r0 + lax.broadcasted_iota(i32, (nq, 1), 0)
        kpos = lax.broadcasted_iota(i32, (1, ext), 1)
        causal = kpos <= qpos
        key = jnp.where(causal, _sort_key(sc), INT_MIN)
        thr = jnp.concatenate(_kth_largest(_row_groups(key, 4), float(topk)), axis=0)
        bias_ref[:, :ext] = jnp.where(causal & (key >= thr), 0.0, NEG_BIG)

        def qk(h):
            kg = kb_ref[:ext, (h // hpg) * HEAD_DIM:(h // hpg + 1) * HEAD_DIM]
            return _bdot_nt(q[:, h * HEAD_DIM:(h + 1) * HEAD_DIM], kg)

        s_next = qk(0)
        for h in range(ATT_HEADS):
            s = s_next * scale + bias_ref[:, :ext]
            if h + 1 < ATT_HEADS:
                s_next = qk(h + 1)
            m = jnp.max(s, axis=1, keepdims=True)
            p = jnp.exp(s - m)
            l = jnp.sum(p, axis=1, keepdims=True)
            vg = vb_ref[:ext, (h // hpg) * HEAD_DIM:(h // hpg + 1) * HEAD_DIM]
            o = jnp.dot(p.astype(bf16), vg, preferred_element_type=f32)
            o_ref[:, h * HEAD_DIM:(h + 1) * HEAD_DIM] = o / l

    nvar = next(v for v in (8, 4, 2, 1) if (seq // nq) % v == 0)
    per = seq // nq // nvar
    for var in range(nvar):
        pl.when(qb // per == var)(functools.partial(process, (var + 1) * per * nq))


def _dsa_prompt(c, tabs, nb, seq):
    nq = Q_BLOCK
    nblk = seq // nq
    topk = min(TOPK_MAX, seq // 4)
    cosk, sink, cosi, sini = tabs
    row = lambda b, j: b * nblk + j
    in_specs = [
        pl.BlockSpec((nq, ATT_WIDTH), lambda b, j: (row(b, j), C_Q // ATT_WIDTH)),
        pl.BlockSpec((nq, IDX_HEADS * IDX_DIM), lambda b, j: (row(b, j), C_QI // (IDX_HEADS * IDX_DIM))),
        pl.BlockSpec((nq, LANES), lambda b, j: (row(b, j), C_WI // LANES)),
        pl.BlockSpec((seq, KV_WIDTH), lambda b, j: (b, C_AK // KV_WIDTH)),
        pl.BlockSpec((seq, KV_WIDTH), lambda b, j: (b, C_AV // KV_WIDTH)),
        pl.BlockSpec((seq, LANES), lambda b, j: (b, C_KI // LANES)),
    ] + [_const_spec((seq, LANES)) for _ in range(4)]
    out_specs = [
        pl.BlockSpec((nq, ATT_WIDTH), lambda b, j: (row(b, j), 0)),
        pl.BlockSpec((seq, KV_WIDTH), lambda b, j: (b, 0)),
        pl.BlockSpec((seq, IDX_DIM), lambda b, j: (b, 0)),
    ]
    scratch = [pltpu.VMEM((seq, KV_WIDTH), bf16), pltpu.VMEM((seq, KV_WIDTH), bf16), pltpu.VMEM((seq, LANES), bf16),
               pltpu.VMEM((nq, seq), f32)]
    return pl.pallas_call(
        functools.partial(_dsa_prompt_body, topk),
        grid=(nb, nblk),
        in_specs=in_specs,
        out_specs=out_specs,
        out_shape=[jax.ShapeDtypeStruct((nb * seq, ATT_WIDTH), f32),
                   jax.ShapeDtypeStruct((nb * seq, KV_WIDTH), f32),
                   jax.ShapeDtypeStruct((nb * seq, IDX_DIM), f32)],
        scratch_shapes=scratch,
        compiler_params=_cparams(("parallel", "arbitrary")),
        name="dsa_prompt",
    )(c, c, c, c, c, c, cosk, sink, cosi, sini)


ROWS = 8


def _page_copies(pt_ref, req, n_pages, srcs_dsts_sems):
    out = []
    for p in range(n_pages):
        pg = pt_ref[req, p]
        for hbm, dst, sem in srcs_dsts_sems:
            out.append(pltpu.make_async_copy(hbm.at[pg], dst(p), sem))
    return out


def _dsa_select_body(topk, n_new, group, pt_ref, qi_ref, wrow_ref, kin_ref, cosi_ref, sini_ref, cki_hbm,
                     sel_ref, kiout_ref, kibuf, sems):
    i = pl.program_id(0)
    nsteps = pl.num_programs(0)
    n_pages = pt_ref.shape[1]
    page = cki_hbm.shape[2]
    past = n_pages * page
    G = group

    def copies(step, slot):
        out = []
        for j in range(G):
            out += _page_copies(pt_ref, step * G + j, n_pages,
                                [(cki_hbm, lambda p, j=j: kibuf.at[slot, j, :, pl.ds(p * page, page)],
                                  sems.at[slot])])
        return out

    slot = i % 2

    @pl.when(i == 0)
    def _():
        for cp in copies(0, 0):
            cp.start()

    @pl.when(i + 1 < nsteps)
    def _():
        for cp in copies(i + 1, 1 - slot):
            cp.start()

    cosi, sini = cosi_ref[...], sini_ref[...]
    li = lax.broadcasted_iota(i32, (LANES, LANES), 0)
    lo = lax.broadcasted_iota(i32, (LANES, LANES), 1)
    eye = (li == lo).astype(f32)
    qis, kins, wcols = [], [], []
    for j in range(G):
        qi = _rope(qi_ref[j], cosi, sini, IDX_DIM // 2)
        kin = _rope(kin_ref[j], cosi, sini, IDX_DIM // 2)[:, :IDX_DIM]
        kiout_ref[j] = kin
        kins.append(jnp.concatenate([kin, jnp.zeros((LANES - ROWS, IDX_DIM), f32)], axis=0))
        qis.append(jnp.concatenate([qi[:, h * IDX_DIM:(h + 1) * IDX_DIM] for h in range(IDX_HEADS)], axis=0))
        wrow = wrow_ref[0, :, j * LANES:(j + 1) * LANES] * IDX_W_SCALE
        wcols.append(jnp.sum(eye * wrow, axis=1, keepdims=True))

    for cp in copies(i, slot):
        cp.wait()

    rows = []
    for j in range(G):
        d = jnp.concatenate([_bdot(qis[j], kibuf[slot, j]), _bdot_nt(qis[j], kins[j])], axis=1)
        term = jnp.maximum(d, 0.0) * wcols[j]
        rows.append(sum(term[h * ROWS:(h + 1) * ROWS] for h in range(IDX_HEADS)))
    sc = jnp.concatenate(rows, axis=0)
    nrow = ROWS * G
    width = past + LANES
    t_row = lax.broadcasted_iota(i32, (nrow, width), 0) % ROWS
    col = lax.broadcasted_iota(i32, (nrow, width), 1)
    valid = (col < past) | ((col - past <= t_row) & (col - past < n_new))
    key = jnp.where(valid, _sort_key(sc), INT_MIN)
    thr = jnp.concatenate(_kth_largest(_row_groups(key, G), float(topk)), axis=0)
    chosen = (valid & (key >= thr)).astype(f32)
    if nrow < LANES:
        chosen = jnp.concatenate([chosen, jnp.zeros((LANES - nrow, width), f32)], axis=0)
    sel_ref[0] = chosen.T


def _dsa_sample_body(n_new, group, pt_ref, q_ref, kn_ref, vn_ref, sel_ref, cosk_ref, sink_ref, ck_hbm, cv_hbm,
                     o_ref, kout_ref, kbuf, vbuf, sems):
    b = pl.program_id(0)
    nreq = pl.num_programs(0)
    n_pages = pt_ref.shape[1]
    prow = ck_hbm.shape[1]
    past = n_pages * prow // KV_HEADS
    G = group

    def copies(req, slot):
        return _page_copies(pt_ref, req, n_pages,
                            [(ck_hbm, lambda p: kbuf.at[slot, pl.ds(p * prow, prow)], sems.at[0, slot]),
                             (cv_hbm, lambda p: vbuf.at[slot, pl.ds(p * prow, prow)], sems.at[1, slot])])

    slot = b % 2

    @pl.when(b == 0)
    def _():
        for cp in copies(0, 0):
            cp.start()

    @pl.when(b + 1 < nreq)
    def _():
        for cp in copies(b + 1, 1 - slot):
            cp.start()

    cosk, sink = cosk_ref[...], sink_ref[...]
    q = _rope(q_ref[0], cosk, sink, HEAD_DIM // 2)
    kn = jnp.concatenate([_rope(kn_ref[0][:, h * HEAD_DIM:(h + 1) * HEAD_DIM], cosk, sink, HEAD_DIM // 2)
                          for h in range(KV_HEADS)], axis=1)
    vn = vn_ref[0]
    kout_ref[0] = kn

    li = lax.broadcasted_iota(i32, (LANES, LANES), 0)
    lo = lax.broadcasted_iota(i32, (LANES, LANES), 1)
    route = ((b % G) * ROWS + lo % ROWS == li) & (lo % ROWS < n_new)
    chosen = jnp.dot(sel_ref[0].astype(bf16), route.astype(bf16), preferred_element_type=f32)
    bias = (chosen - 1.0) * (-NEG_BIG)
    bias_p, bias_n = bias[:past], bias[past:past + ROWS]

    eye = (li == lo).astype(f32)
    scale = HEAD_DIM ** -0.5
    hpg = ATT_HEADS // KV_HEADS
    zero_rows = lambda n: [jnp.zeros((n, HEAD_DIM), f32)] if n else []

    for cp in copies(b, slot):
        cp.wait()

    s_p = s_n = None
    for g in range(KV_HEADS):
        heads = [q[:, (g * hpg + r) * HEAD_DIM:(g * hpg + r + 1) * HEAD_DIM] for r in range(hpg)]
        qg = jnp.concatenate(zero_rows(g * hpg * ROWS) + heads + zero_rows(LANES - (g + 1) * hpg * ROWS), axis=0)
        kp = kbuf[slot, pl.ds(g, past, stride=KV_HEADS), :]
        part_p = _bdot_nt(kp, qg)
        part_n = _bdot_nt(kn[:, g * HEAD_DIM:(g + 1) * HEAD_DIM], qg)
        s_p = part_p if s_p is None else s_p + part_p
        s_n = part_n if s_n is None else s_n + part_n
    s_p = s_p * scale + bias_p
    s_n = s_n * scale + bias_n
    m = jnp.maximum(jnp.max(s_p, axis=0, keepdims=True), jnp.max(s_n, axis=0, keepdims=True))
    p_p = jnp.exp(s_p - m)
    p_n = jnp.exp(s_n - m)
    l = jnp.sum(p_p, axis=0, keepdims=True) + jnp.sum(p_n, axis=0, keepdims=True)
    l_col = jnp.sum(eye * l, axis=1, keepdims=True)
    for g in range(KV_HEADS):
        vp = vbuf[slot, pl.ds(g, past, stride=KV_HEADS), :]
        o = (_bdot_tn(p_p, vp) + _bdot_tn(p_n, vn[:, g * HEAD_DIM:(g + 1) * HEAD_DIM])) / l_col
        for r in range(hpg):
            h = g * hpg + r
            o_ref[0, :, h * HEAD_DIM:(h + 1) * HEAD_DIM] = o[h * ROWS:(h + 1) * ROWS]


def _dsa_sample(csel, wrow, tabs, page_table, cache_k, cache_v, cache_kidx, n_new):
    q8, qi8, kn8, vn8, kin8 = csel
    nreq = q8.shape[0]
    n_pages = page_table.shape[1]
    page = cache_kidx.shape[2]
    past = n_pages * page
    topk = min(TOPK_MAX, (past + n_new) // 4)
    group = min(LANES // ROWS, nreq)
    cosk, sink, cosi, sini = tabs
    anyspec = pl.BlockSpec(memory_space=pl.ANY)
    tab = pl.BlockSpec((ROWS, LANES), lambda b, pt: (0, 0))

    grp3 = lambda w: pl.BlockSpec((group, ROWS, w), lambda i, pt: (i, 0, 0))
    sel, ki_new = pl.pallas_call(
        functools.partial(_dsa_select_body, topk, n_new, group),
        grid_spec=pltpu.PrefetchScalarGridSpec(
            num_scalar_prefetch=1,
            grid=(nreq // group,),
            in_specs=[grp3(IDX_HEADS * IDX_DIM), pl.BlockSpec((1, 1, group * LANES), lambda i, pt: (i, 0, 0)),
                      grp3(LANES), tab, tab, anyspec],
            out_specs=[pl.BlockSpec((1, past + LANES, LANES), lambda i, pt: (i, 0, 0)), grp3(IDX_DIM)],
            scratch_shapes=[pltpu.VMEM((2, group, IDX_DIM, past), f32), pltpu.SemaphoreType.DMA((2,))],
        ),
        out_shape=[jax.ShapeDtypeStruct((nreq // group, past + LANES, LANES), f32),
                   jax.ShapeDtypeStruct((nreq, ROWS, IDX_DIM), f32)],
        compiler_params=_cparams(("arbitrary",)),
        name="dsa_select",
    )(page_table, qi8, wrow.reshape(nreq // group, 1, group * LANES), kin8, cosi, sini, cache_kidx)

    req3 = lambda w: pl.BlockSpec((1, ROWS, w), lambda b, pt: (b, 0, 0))
    prow = cache_k.shape[1]
    o, k_new = pl.pallas_call(
        functools.partial(_dsa_sample_body, n_new, group),
        grid_spec=pltpu.PrefetchScalarGridSpec(
            num_scalar_prefetch=1,
            grid=(nreq,),
            in_specs=[req3(ATT_WIDTH), req3(KV_WIDTH), req3(KV_WIDTH),
                      pl.BlockSpec((1, past + LANES, LANES), lambda b, pt: (b // group, 0, 0)), tab, tab,
                      anyspec, anyspec],
            out_specs=[req3(ATT_WIDTH), req3(KV_WIDTH)],
            scratch_shapes=[pltpu.VMEM((2, n_pages * prow, HEAD_DIM), f32),
                            pltpu.VMEM((2, n_pages * prow, HEAD_DIM), f32), pltpu.SemaphoreType.DMA((2, 2))],
        ),
        out_shape=[jax.ShapeDtypeStruct((nreq, ROWS, ATT_WIDTH), f32),
                   jax.ShapeDtypeStruct((nreq, ROWS, KV_WIDTH), f32)],
        compiler_params=_cparams(("arbitrary",)),
        name="dsa_sample",
    )(page_table, q8, kn8, vn8, sel, cosk, sink, cache_k, cache_v)
    return o, k_new, ki_new


def _merge_body(orw_ref, oatt_ref, grw_ref, gatt_ref, prw_ref, patt_ref, o_ref):
    a = jnp.dot(orw_ref[...].astype(bf16), prw_ref[...], preferred_element_type=f32)
    b = jnp.dot(oatt_ref[...].astype(bf16), patt_ref[...], preferred_element_type=f32)
    o_ref[...] = (jax.nn.sigmoid(grw_ref[...]) * a + jax.nn.sigmoid(gatt_ref[...]) * b).astype(bf16)


def _merge(o_rw, o_att, c, p_rw, p_att):
    m = o_rw.shape[0]
    d = p_rw.shape[1]
    tm = min(512, m)
    tn = 1024
    nj = d // tn
    return pl.pallas_call(
        _merge_body,
        grid=(m // tm, nj),
        in_specs=[
            pl.BlockSpec((tm, RW_WIDTH), lambda i, j: (i, 0)),
            pl.BlockSpec((tm, ATT_WIDTH), lambda i, j: (i, 0)),
            pl.BlockSpec((tm, tn), lambda i, j: (i, C_GRW // tn + j)),
            pl.BlockSpec((tm, tn), lambda i, j: (i, C_GATT // tn + j)),
            pl.BlockSpec((RW_WIDTH, tn), lambda i, j: (0, j)),
            pl.BlockSpec((ATT_WIDTH, tn), lambda i, j: (0, j)),
        ],
        out_specs=pl.BlockSpec((tm, tn), lambda i, j: (i, j)),
        out_shape=jax.ShapeDtypeStruct((m, d), bf16),
        compiler_params=_cparams(("parallel", "arbitrary")),
        name="merge",
    )(o_rw, o_att, c, c, p_rw, p_att)


def _outproj_body(mg_ref, x_ref, w_ref, o_ref):
    o_ref[...] = x_ref[...] + jnp.dot(mg_ref[...], w_ref[...], preferred_element_type=f32)


def _outproj(merged, x2d, w_o):
    m, d = x2d.shape
    tm = min(512, m)
    tn = 1024
    return pl.pallas_call(
        _outproj_body,
        grid=(m // tm, d // tn),
        in_specs=[
            pl.BlockSpec((tm, d), lambda i, j: (i, 0)),
            pl.BlockSpec((tm, tn), lambda i, j: (i, j)),
            pl.BlockSpec((d, tn), lambda i, j: (0, j)),
        ],
        out_specs=pl.BlockSpec((tm, tn), lambda i, j: (i, j)),
        out_shape=jax.ShapeDtypeStruct((m, d), f32),
        compiler_params=_cparams(("parallel", "arbitrary")),
        name="out_proj",
    )(merged, x2d, w_o)


def _convglu_body(shift, tiles_per_seq, final_norm, h_ref, g2_ref, gf_ref, cprev_ref, wg_ref, wu_ref, cw_ref,
                  cb_ref, wd_ref, o_ref, tail_ref, n_ref, ext_ref, carry_ref, act_ref):
    i = pl.program_id(0)
    j = pl.program_id(1)
    nj = pl.num_programs(1) - 1
    tm = h_ref.shape[0]
    hist = 2 * shift
    base = ext_ref.shape[0] - tm

    def up_stage():
        n = n_ref[...]
        gate = jnp.dot(n, wg_ref[...], preferred_element_type=f32)
        up = jnp.dot(n, wu_ref[...], preferred_element_type=f32)
        ext_ref[base - hist:base, :] = jnp.where(i % tiles_per_seq == 0, cprev_ref[0], carry_ref[j])
        ext_ref[base:, :] = gate
        cw = cw_ref[...]
        c = (cb_ref[...] + ext_ref[base - hist:base - hist + tm, :] * cw[0:1, :]
             + ext_ref[base - shift:base - shift + tm, :] * cw[1:2, :] + gate * cw[2:3, :])
        tail = ext_ref[base + tm - hist:base + tm, :]
        tail_ref[0] = tail
        carry_ref[j] = tail
        act_ref[j % 2] = ((c * jax.nn.sigmoid(c)) * up).astype(bf16)

    def down_stage():
        o_ref[...] += jnp.dot(act_ref[(j + 1) % 2], wd_ref[...], preferred_element_type=f32)

    @pl.when(j == 0)
    def _():
        @pl.when(i == 0)
        def _():
            carry_ref[...] = jnp.zeros_like(carry_ref)

        h = h_ref[...]
        ms = jnp.mean(h * h, axis=-1, keepdims=True)
        n_ref[...] = (h * lax.rsqrt(ms + RMS_EPS) * g2_ref[...]).astype(bf16)
        o_ref[...] = jnp.zeros_like(o_ref)
        up_stage()

    @pl.when((j > 0) & (j < nj))
    def _():
        up_stage()
        down_stage()

    @pl.when(j == nj)
    def _():
        down_stage()
        out = h_ref[...] + o_ref[...]
        if final_norm:
            ms = jnp.mean(out * out, axis=-1, keepdims=True)
            out = out * lax.rsqrt(ms + RMS_EPS) * gf_ref[...]
        o_ref[...] = out


def _convglu(h2d, conv_prev, norm2, norm_f, w_up_b, conv_w, conv_b, w_down_b, nseq_groups, shift, final_norm):
    m, d = h2d.shape
    d_ff = w_down_b.shape[0]
    rows_per_group = m // nseq_groups
    tm = 1024 if rows_per_group % 1024 == 0 else min(512, rows_per_group)
    tf = 512
    nj = d_ff // tf
    tiles_per_seq = rows_per_group // tm
    hist = 2 * shift
    base = ((hist + 7) // 8) * 8
    up_j = lambda j: jnp.minimum(j, nj - 1)
    down_j = lambda j: jnp.maximum(j - 1, 0)
    once = pl.Buffered(1)
    out, tails = pl.pallas_call(
        functools.partial(_convglu_body, shift, tiles_per_seq, final_norm),
        grid=(m // tm, nj + 1),
        in_specs=[
            pl.BlockSpec((tm, d), lambda i, j: (i, 0), pipeline_mode=once),
            pl.BlockSpec((1, d), lambda i, j: (0, 0)),
            pl.BlockSpec((1, d), lambda i, j: (0, 0)),
            pl.BlockSpec((1, hist, tf), lambda i, j: (i // tiles_per_seq, 0, up_j(j))),
            pl.BlockSpec((d, tf), lambda i, j: (0, up_j(j))),
            pl.BlockSpec((d, tf), lambda i, j: (0, nj + up_j(j))),
            pl.BlockSpec((CONV_W, tf), lambda i, j: (0, up_j(j))),
            pl.BlockSpec((1, tf), lambda i, j: (0, up_j(j))),
            pl.BlockSpec((tf, d), lambda i, j: (down_j(j), 0)),
        ],
        out_specs=[
            pl.BlockSpec((tm, d), lambda i, j: (i, 0), pipeline_mode=once),
            pl.BlockSpec((1, hist, tf), lambda i, j: (i, 0, up_j(j))),
        ],
        out_shape=[jax.ShapeDtypeStruct((m, d), f32), jax.ShapeDtypeStruct((m // tm, hist, d_ff), f32)],
        scratch_shapes=[pltpu.VMEM((tm, d), bf16), pltpu.VMEM((base + tm, tf), f32),
                        pltpu.VMEM((nj, hist, tf), f32), pltpu.VMEM((2, tm, tf), bf16)],
        compiler_params=_cparams(("arbitrary", "arbitrary")),
        name="convglu",
    )(h2d, norm2, norm_f, conv_prev, w_up_b, w_up_b, conv_w, conv_b, w_down_b)
    return out, tails[tiles_per_seq - 1::tiles_per_seq]


def _pad_cols(a, width):
    return jnp.pad(a, [(0, 0)] * (a.ndim - 1) + [(0, width - a.shape[-1])])


def _to_layout_rows(a):
    def pad_rows(x, n):
        return jnp.pad(x, ((0, n - x.shape[0]), (0, 0)))

    rw = 3 * RW_WIDTH
    o = {}
    o["r"], o["k"], o["v"] = a[0:RW_WIDTH], a[RW_WIDTH:2 * RW_WIDTH], a[2 * RW_WIDTH:rw]
    p = rw
    o["wl"] = a[p:p + D_DECAY_LORA]; p += D_DECAY_LORA
    o["al"] = a[p:p + D_AAA_LORA]; p += D_AAA_LORA
    o["gl"] = a[p:p + D_GATE_LORA]; p += D_GATE_LORA
    o["q"] = a[p:p + ATT_WIDTH]; p += ATT_WIDTH
    o["ak"] = a[p:p + KV_WIDTH]; p += KV_WIDTH
    o["av"] = a[p:p + KV_WIDTH]; p += KV_WIDTH
    o["qi"] = a[p:p + IDX_HEADS * IDX_DIM]; p += IDX_HEADS * IDX_DIM
    o["ki"] = a[p:p + IDX_DIM]; p += IDX_DIM
    o["wi"] = a[p:p + IDX_HEADS]; p += IDX_HEADS
    d = (a.shape[0] - p) // 2
    o["grw"], o["gatt"] = a[p:p + d], a[p + d:p + 2 * d]
    return jnp.concatenate([
        o["r"], o["k"], o["v"], o["q"], o["grw"], o["gatt"], o["qi"],
        pad_rows(o["wl"], LANES), pad_rows(o["al"], LANES), o["gl"],
        pad_rows(o["ki"], LANES), o["ak"], o["av"], pad_rows(o["wi"], LANES)], axis=0)


def _rw_cols_layout(a):
    rw = 3 * RW_WIDTH
    wl = a[..., rw:rw + D_DECAY_LORA]
    al = a[..., rw + D_DECAY_LORA:rw + D_DECAY_LORA + D_AAA_LORA]
    gl = a[..., rw + D_DECAY_LORA + D_AAA_LORA:]
    return a[..., :rw], jnp.concatenate([_pad_cols(wl, LANES), _pad_cols(al, LANES), gl], axis=-1)


def _rw_cols_from_layout(c):
    return jnp.concatenate([c[..., :3 * RW_WIDTH], c[..., C_LORA:C_LORA + D_DECAY_LORA],
                            c[..., C_LORA + LANES:C_LORA + LANES + D_AAA_LORA],
                            c[..., C_LORA + 2 * LANES:C_LORA + 3 * LANES]], axis=-1)


def _rope_tables(pos, rows):
    pos = jnp.pad(pos.astype(f32), (0, rows - pos.shape[0]))
    out = []
    for dim in (HEAD_DIM, IDX_DIM):
        half = dim // 2
        inv_freq = 1.0 / (ROPE_THETA ** (jnp.arange(half, dtype=f32) / half))
        ang = pos[:, None] * inv_freq[None, :]
        cos, sin = jnp.cos(ang), jnp.sin(ang)
        reps = LANES // dim
        out.append(jnp.tile(jnp.concatenate([cos, cos], axis=1), (1, reps)))
        out.append(jnp.tile(jnp.concatenate([-sin, sin], axis=1), (1, reps)))
    return tuple(out)


def _pair_unblock(s):
    n = s.shape[0]
    a = s[:, :, :RW_HEAD, :RW_HEAD]
    b = s[:, :, RW_HEAD:, RW_HEAD:]
    return jnp.stack([a, b], axis=2).reshape(n, RW_HEADS, RW_HEAD, RW_HEAD)


def kernel(x_prompt, x_sample, cache_k, cache_v, cache_kidx, state_wkv, state_shift, state_conv, page_table, norm1, w_in, rw_mu, rw_w0, rw_w2, rw_a0, rw_a2, rw_g2, rw_k_k, rw_k_a, rw_r_k, rw_lnx_w, rw_lnx_b, p_rw, p_att, w_o, norm2, w_up, conv_w, conv_b, w_down, norm_f):
    B, S, D = x_prompt.shape
    DB, DS, _ = x_sample.shape
    depth = w_in.shape[0]
    page = cache_k.shape[2]
    n_pages = page_table.shape[1]
    past_len = n_pages * page
    d_ff = w_down.shape[1]
    dt = x_prompt.dtype

    tabs_p = _rope_tables(jnp.arange(S), S)
    tabs_s = _rope_tables(past_len + jnp.arange(DS), ROWS)
    row2 = lambda a: a.reshape(1, -1)

    hp = x_prompt.reshape(B * S, D)
    hs = x_sample.reshape(DB * DS, D)
    outs = {k: [] for k in ("kp", "vp", "kip", "ks", "vs", "kis", "wkvp", "wkvs", "shp", "shs", "cvp", "cvs")}
    for l in range(depth):
        w_in_t = _to_layout_rows(w_in[l].T).astype(bf16)
        mu_rkv, mu_lora = _rw_cols_layout(rw_mu[l][None, :])
        params = dict(
            mu_rkv=mu_rkv, mu_lora=mu_lora, w0=row2(rw_w0[l]),
            w2=jnp.pad(rw_w2[l], ((0, LANES - D_DECAY_LORA), (0, 0))).astype(bf16), a0=row2(rw_a0[l]),
            a2=jnp.pad(rw_a2[l], ((0, LANES - D_AAA_LORA), (0, 0))).astype(bf16), g2=rw_g2[l].astype(bf16),
            k_k=row2(rw_k_k[l]), k_a=row2(rw_k_a[l]), r_k=row2(rw_r_k[l]), lnx_w=row2(rw_lnx_w[l]),
            lnx_b=row2(rw_lnx_b[l]))
        last = l == depth - 1

        c_p = _in_proj(hp, row2(norm1[l]), w_in_t)
        c_s = _in_proj(hs, row2(norm1[l]), w_in_t)

        sp_rkv = jnp.zeros((B, 1, RKV_W), dt)
        sp_lora = jnp.zeros((B, 1, LORA_W), dt)
        s0_p = jnp.zeros((B, RW_HEADS // 2, LANES, LANES), dt)
        o_rw_p, wkv_p = _rwkv_prompt(c_p, sp_rkv, sp_lora, s0_p, params, B, S)

        c_s3 = c_s.reshape(DB, DS, NP_COLS)
        ss_rkv, ss_lora = _rw_cols_layout(state_shift[l])
        prev_rkv = jnp.concatenate([ss_rkv[:, None, :], c_s3[:, :-1, :RKV_W]], axis=1).reshape(DB * DS, RKV_W)
        prev_lora = jnp.concatenate([ss_lora[:, None, :], c_s3[:, :-1, C_LORA:C_LORA + LORA_W]], axis=1)
        prev_lora = prev_lora.reshape(DB * DS, LORA_W)
        r_s, lw_s, k_s, v_s, a_s, b_s, g_s = _rwkv_prep(c_s, prev_rkv, prev_lora, params)
        y_s, wkv_s = _rwkv_seq((r_s, lw_s, k_s, v_s, a_s, b_s), state_wkv[l], DB, DS)
        o_rw_s = _rwkv_post(y_s, r_s, k_s, v_s, g_s, params)

        o_att_p, k_p, ki_p = _dsa_prompt(c_p, tabs_p, B, S)

        def rows8(lo, w):
            return jnp.pad(c_s3[:, :, lo:lo + w], ((0, 0), (0, ROWS - DS), (0, 0)))

        csel = (rows8(C_Q, ATT_WIDTH), rows8(C_QI, IDX_HEADS * IDX_DIM), rows8(C_AK, KV_WIDTH),
                rows8(C_AV, KV_WIDTH), rows8(C_KI, LANES))
        wi_s = jnp.pad(c_s3[:, :, C_WI:C_WI + IDX_HEADS], ((0, 0), (0, ROWS - DS), (0, 0)))
        wrow = jnp.transpose(wi_s, (0, 2, 1)).reshape(DB, 1, IDX_HEADS * ROWS)
        o_att_s8, k_s8, ki_s8 = _dsa_sample(
            csel, wrow, tabs_s, page_table, cache_k[l].reshape(-1, page * KV_HEADS, HEAD_DIM),
            cache_v[l].reshape(-1, page * KV_HEADS, HEAD_DIM), jnp.swapaxes(cache_kidx[l], 1, 2), DS)
        o_att_s = o_att_s8[:, :DS].reshape(DB * DS, ATT_WIDTH)

        p_rw_b, p_att_b, w_o_b = p_rw[l].astype(bf16), p_att[l].astype(bf16), w_o[l].astype(bf16)
        h_p = _outproj(_merge(o_rw_p, o_att_p, c_p, p_rw_b, p_att_b), hp, w_o_b)
        h_s = _outproj(_merge(o_rw_s, o_att_s, c_s, p_rw_b, p_att_b), hs, w_o_b)

        w_up_b, w_down_b = w_up[l].astype(bf16), w_down[l].astype(bf16)
        cv_args = (row2(norm2[l]), row2(norm_f), w_up_b, conv_w[l], row2(conv_b[l]), w_down_b)
        hp, tail_p = _convglu(h_p, jnp.zeros((B, CONV_W - 1, d_ff), dt), *cv_args, B, 1, last)
        h_s_tm = h_s.reshape(DB, DS, D).transpose(1, 0, 2).reshape(DS * DB, D)
        cprev_tm = state_conv[l].transpose(1, 0, 2).reshape(1, (CONV_W - 1) * DB, d_ff)
        hs_tm, tail_s = _convglu(h_s_tm, cprev_tm, *cv_args, 1, DB, last)
        hs = hs_tm.reshape(DS, DB, D).transpose(1, 0, 2).reshape(DB * DS, D)

        outs["kp"].append(k_p.reshape(B, S // page, page, KV_HEADS, HEAD_DIM))
        outs["vp"].append(c_p[:, C_AV:C_AV + KV_WIDTH].reshape(B, S // page, page, KV_HEADS, HEAD_DIM))
        outs["kip"].append(ki_p.reshape(B, S // page, page, IDX_DIM))
        outs["ks"].append(k_s8[:, :DS].reshape(DB, DS, KV_HEADS, HEAD_DIM))
        outs["vs"].append(c_s3[:, :, C_AV:C_AV + KV_WIDTH].reshape(DB, DS, KV_HEADS, HEAD_DIM))
        outs["kis"].append(ki_s8[:, :DS])
        outs["wkvp"].append(_pair_unblock(wkv_p))
        outs["wkvs"].append(wkv_s)
        outs["shp"].append(_rw_cols_from_layout(c_p.reshape(B, S, NP_COLS)[:, -1]))
        outs["shs"].append(_rw_cols_from_layout(c_s3[:, -1]))
        outs["cvp"].append(tail_p)
        outs["cvs"].append(tail_s.reshape(CONV_W - 1, DB, d_ff).transpose(1, 0, 2))

    y_prompt = hp.reshape(B, S, D)
    y_sample = hs.reshape(DB, DS, D)
    st = lambda k: jnp.stack(outs[k])
    return (y_prompt, y_sample, st("kp"), st("vp"), st("kip"), st("ks"), st("vs"), st("kis"), st("wkvp"),
            st("wkvs"), st("shp"), st("shs"), st("cvp"), st("cvs"))
```

```python
import functools
import math

import numpy as np
import jax
import jax.numpy as jnp
from jax import lax
from jax.experimental import pallas as pl
from jax.experimental.pallas import tpu as pltpu

f32 = jnp.float32
bf16 = jnp.bfloat16
i32 = jnp.int32

RW_HEADS = 16
RW_HEAD = 64
RW_WIDTH = RW_HEADS * RW_HEAD
D_DECAY_LORA = 96
D_AAA_LORA = 96
D_GATE_LORA = 128
GN_EPS = 64e-5
ATT_HEADS = 8
KV_HEADS = 2
HEAD_DIM = 128
ATT_WIDTH = ATT_HEADS * HEAD_DIM
KV_WIDTH = KV_HEADS * HEAD_DIM
IDX_HEADS = 16
IDX_DIM = 64
IDX_W_SCALE = (IDX_HEADS * IDX_DIM) ** -0.5
TOPK_MAX = 256
Q_BLOCK = 128
ROPE_THETA = 10000.0
CONV_W = 3
RMS_EPS = 1e-6

LANES = 128
NEG_BIG = -1e30
INT_MIN = -(2 ** 31)

C_R, C_K, C_V, C_Q, C_GRW, C_GATT, C_QI = 0, 1024, 2048, 3072, 4096, 6144, 8192
C_LORA = 9216
C_KI, C_AK, C_AV, C_WI = 9600, 9728, 9984, 10240
NP_COLS = 10368
LORA_W = 384
RKV_W = 3 * RW_WIDTH

VMEM_LIMIT = 56 * 1024 * 1024


def _cparams(sem):
    return pltpu.CompilerParams(dimension_semantics=sem, vmem_limit_bytes=VMEM_LIMIT)


def _bdot(a, b):
    return jnp.dot(a.astype(bf16), b.astype(bf16), preferred_element_type=f32)


def _bdot_nt(a, b):
    return lax.dot_general(a.astype(bf16), b.astype(bf16), (((1,), (1,)), ((), ())), preferred_element_type=f32)


def _bdot_tn(a, b):
    return lax.dot_general(a.astype(bf16), b.astype(bf16), (((0,), (0,)), ((), ())), preferred_element_type=f32)


def _split3(x):
    x1 = x.astype(bf16)
    r1 = x - x1.astype(f32)
    x2 = r1.astype(bf16)
    x3 = (r1 - x2.astype(f32)).astype(bf16)
    return x1, x2, x3


def _split2(x):
    x1 = x.astype(bf16)
    return x1, (x - x1.astype(f32)).astype(bf16)


def _headsum(x):
    ri = lax.broadcasted_iota(i32, (LANES, LANES), 0) // RW_HEAD
    ci = lax.broadcasted_iota(i32, (LANES, LANES), 1) // RW_HEAD
    bd = (ri == ci).astype(bf16)
    outs = []
    for i in range(x.shape[1] // LANES):
        hi, lo = _split2(x[:, i * LANES:(i + 1) * LANES])
        outs.append(jnp.dot(hi, bd, preferred_element_type=f32) + jnp.dot(lo, bd, preferred_element_type=f32))
    return jnp.concatenate(outs, axis=1)


def _softplus(x):
    return jnp.maximum(x, 0.0) + jnp.log(1.0 + jnp.exp(-jnp.abs(x)))


def _rope(x, cos, sin, half):
    w = x.shape[1]
    reps = w // LANES
    if reps > 1:
        cos = jnp.concatenate([cos] * reps, axis=1)
        sin = jnp.concatenate([sin] * reps, axis=1)
    if 2 * half == LANES and w == LANES:
        partner = pltpu.roll(x, half, axis=1)
    else:
        lane = lax.broadcasted_iota(i32, (1, w), 1)
        first = (lane % (2 * half)) < half
        partner = jnp.where(first, pltpu.roll(x, w - half, axis=1), pltpu.roll(x, half, axis=1))
    return x * cos + partner * sin


def _inproj_body(x_ref, g_ref, w_ref, o_ref, n_ref):
    @pl.when(pl.program_id(1) == 0)
    def _():
        x = x_ref[...]
        ms = jnp.mean(x * x, axis=-1, keepdims=True)
        n_ref[...] = (x * lax.rsqrt(ms + RMS_EPS) * g_ref[...]).astype(bf16)

    o_ref[...] = lax.dot_general(n_ref[...], w_ref[...], (((1,), (1,)), ((), ())), preferred_element_type=f32)


def _in_proj(x2d, gain, w_t):
    m, d = x2d.shape
    tm = 1024 if m % 1024 == 0 else min(512, m)
    tn = NP_COLS // 9
    return pl.pallas_call(
        _inproj_body,
        grid=(m // tm, NP_COLS // tn),
        in_specs=[
            pl.BlockSpec((tm, d), lambda i, j: (i, 0)),
            pl.BlockSpec((1, d), lambda i, j: (0, 0)),
            pl.BlockSpec((tn, d), lambda i, j: (j, 0)),
        ],
        out_specs=pl.BlockSpec((tm, tn), lambda i, j: (i, j)),
        out_shape=jax.ShapeDtypeStruct((m, NP_COLS), f32),
        scratch_shapes=[pltpu.VMEM((tm, d), bf16)],
        compiler_params=_cparams(("parallel", "arbitrary")),
        name="in_proj",
    )(x2d, gain, w_t)


RW_PARAM_NAMES = ("mu_rkv", "mu_lora", "w0", "w2", "a0", "a2", "g2", "k_k", "k_a", "r_k", "lnx_w", "lnx_b")


def _rw_prep(c_rkv, c_lora, p_rkv, p_lora, P):
    m = c_rkv + (p_rkv - c_rkv) * P["mu_rkv"]
    ml = c_lora + (p_lora - c_lora) * P["mu_lora"]
    r, k, v = m[:, :RW_WIDTH], m[:, RW_WIDTH:2 * RW_WIDTH], m[:, 2 * RW_WIDTH:]
    wl, al, gl = ml[:, :LANES], ml[:, LANES:2 * LANES], ml[:, 2 * LANES:]
    w_log = -_softplus(-(P["w0"] + _bdot(jnp.tanh(wl), P["w2"]))) - 0.5
    logw = -jnp.exp(w_log)
    asig = jax.nn.sigmoid(P["a0"] + _bdot(al, P["a2"]))
    g = _bdot(jax.nn.sigmoid(gl), P["g2"])
    kk = k * P["k_k"]
    kkn = kk * lax.rsqrt(jnp.maximum(_headsum(kk * kk), 1e-24))
    k2 = k * (1.0 + (asig - 1.0) * P["k_a"])
    return r, logw, k2, v, -kkn, kkn * asig, g


def _rw_post(y, r, k2, v, g, P):
    inv_n = 1.0 / RW_HEAD
    mean = _headsum(y) * inv_n
    d = y - mean
    var = _headsum(d * d) * inv_n
    yn = d * lax.rsqrt(var + GN_EPS) * P["lnx_w"] + P["lnx_b"]
    bonus = _headsum(r * k2 * P["r_k"]) * v
    return (yn + bonus) * g


def _chunk_pre(at, rt, bt, kt, v, masks, nsteps):
    keep, eye2, colh, mA, _ = masks
    R = range(len(at))
    C = at[0].shape[0]
    X = [jnp.concatenate([at[p], rt[p]], axis=0) for p in R]
    scA = [jnp.where(keep, _bdot_nt(jnp.where(mA, X[p], 0.0), jnp.concatenate([bt[p], kt[p]], axis=0)), 0.0)
           for p in R]
    scB = [jnp.where(keep, _bdot_nt(jnp.where(mA, 0.0, X[p]), jnp.concatenate([kt[p], bt[p]], axis=0)), 0.0)
           for p in R]

    def bdiag(x):
        return jnp.concatenate([jnp.where(colh, x, 0.0), jnp.where(colh, 0.0, x)], axis=0)

    L = [jnp.where(colh, scA[p][:C], scB[p][:C]) for p in R]
    T = [eye2 + L[p] for p in R]
    if nsteps > 0:
        Pw = [_bdot(L[p], bdiag(L[p])) for p in R]
        for i in range(nsteps):
            if i < nsteps - 1:
                Z = [_bdot(Pw[p], jnp.concatenate([bdiag(T[p]), bdiag(Pw[p])], axis=1)) for p in R]
                T = [T[p] + Z[p][:, :2 * C] for p in R]
                Pw = [Z[p][:, 2 * C:] for p in R]
            else:
                T = [T[p] + _bdot(Pw[p], bdiag(T[p])) for p in R]
    lkv = [_bdot(jnp.where(colh, scB[p][:C], scA[p][:C]),
                 jnp.concatenate([jnp.where(mA, 0.0, v[p]), jnp.where(mA, v[p], 0.0)], axis=0)) for p in R]
    readout = [jnp.concatenate([scA[p][C:], scB[p][C:]], axis=1) for p in R]
    return T, lkv, readout


def _chunk_seq(at, rt, be, ke, v, wc, T, lkv, readout, S, masks):
    _, _, _, mA, bdmask = masks
    R = range(len(at))
    C = at[0].shape[0]
    XS = [_bdot_nt(jnp.concatenate([at[p], rt[p]], axis=0), S[p]) for p in R]
    G = [XS[p][:C] + lkv[p] for p in R]
    U = [_bdot(T[p], jnp.concatenate([jnp.where(mA, G[p], 0.0), jnp.where(mA, 0.0, G[p])], axis=0)) for p in R]
    Y = [XS[p][C:] + _bdot(readout[p], jnp.concatenate([jnp.where(mA, U[p], 0.0), jnp.where(mA, v[p], 0.0),
                                                        jnp.where(mA, 0.0, v[p]), jnp.where(mA, 0.0, U[p])], axis=0))
         for p in R]
    dS = [_bdot_tn(jnp.concatenate([U[p], v[p]], axis=0), jnp.concatenate([be[p], ke[p]], axis=0)) for p in R]
    S_new = [S[p] * wc[p] + jnp.where(bdmask, dS[p], 0.0) for p in R]
    return Y, S_new


def _chunk_masks(C):
    row = lax.broadcasted_iota(i32, (2 * C, 2 * C), 0)
    col = lax.broadcasted_iota(i32, (2 * C, 2 * C), 1)
    t = jnp.where(row >= C, row - C, row)
    s = jnp.where(col >= C, col - C, col)
    keep = (s < t) | ((row >= C) & (s == t))
    hrow = lax.broadcasted_iota(i32, (C, 2 * C), 0)
    hcol = lax.broadcasted_iota(i32, (C, 2 * C), 1)
    eye2 = (hrow == jnp.where(hcol >= C, hcol - C, hcol)).astype(f32)
    colh = hcol < C
    mA = lax.broadcasted_iota(i32, (1, LANES), 1) < RW_HEAD
    r2 = lax.broadcasted_iota(i32, (LANES, LANES), 0) // RW_HEAD
    c2 = lax.broadcasted_iota(i32, (LANES, LANES), 1) // RW_HEAD
    return keep, eye2, colh, mA, r2 == c2


def _rwkv_prompt_body(chunk, rkv_ref, lora_ref, sp_rkv_ref, sp_lora_ref, s0_ref, *rest):
    np_ = len(RW_PARAM_NAMES)
    P = {n: rest[i][...] for i, n in enumerate(RW_PARAM_NAMES)}
    o_ref, sout_ref = rest[np_], rest[np_ + 1]
    (S_ref, car_rkv, car_lora, at_s, rt_s, bt_s, kt_s, be_s, ke_s, v_s, cum_s, y_s, t_s, lkv_s, ro_s) = rest[np_ + 2:]
    t = pl.program_id(1)
    tt = rkv_ref.shape[0]
    C = chunk

    @pl.when(t == 0)
    def _():
        S_ref[...] = s0_ref[0]
        car_rkv[...] = sp_rkv_ref[0]
        car_lora[...] = sp_lora_ref[0]

    c_rkv = rkv_ref[...]
    c_lora = lora_ref[...]
    first = lax.broadcasted_iota(i32, (tt, 1), 0) == 0
    p_rkv = jnp.where(first, car_rkv[...], pltpu.roll(c_rkv, 1, axis=0))
    p_lora = jnp.where(first, car_lora[...], pltpu.roll(c_lora, 1, axis=0))
    car_rkv[...] = c_rkv[tt - 1:tt, :]
    car_lora[...] = c_lora[tt - 1:tt, :]
    r, logw, k2, v, a, b, g = _rw_prep(c_rkv, c_lora, p_rkv, p_lora, P)

    ri = lax.broadcasted_iota(i32, (tt, tt), 0)
    ci = lax.broadcasted_iota(i32, (tt, tt), 1)
    same = ri // C == ci // C
    tril = (same & (ci <= ri)).astype(bf16)
    triu = (same & (ci > ri)).astype(bf16)
    pieces = _split3(logw)
    cum = sum(jnp.dot(tril, piece, preferred_element_type=f32) for piece in pieces)
    rev = sum(jnp.dot(triu, piece, preferred_element_type=f32) for piece in pieces)
    iw = jnp.exp(-cum)
    ew = jnp.exp(rev)
    at_s[...] = a * jnp.exp(cum - logw)
    rt_s[...] = r * jnp.exp(cum)
    bt_s[...] = b * iw
    kt_s[...] = k2 * iw
    be_s[...] = b * ew
    ke_s[...] = k2 * ew
    v_s[...] = v
    cum_s[...] = cum
    masks = _chunk_masks(C)
    nsteps = max(int(math.ceil(math.log2(C))) - 1, 0)
    pairs = range(RW_WIDTH // LANES)

    lanes = [slice(p * LANES, (p + 1) * LANES) for p in pairs]
    wide = [slice(p * 4 * C, (p + 1) * 4 * C) for p in pairs]

    def ld(ref, r0, cols=lanes):
        return [ref[pl.ds(r0, C), cols[p]] for p in pairs]

    npre = next(u for u in (4, 2, 1) if (tt // C) % u == 0)

    def pre_body(gi, carry):
        r0s = [pl.multiple_of((gi * npre + u) * C, C) for u in range(npre)]
        cat = lambda ref: sum((ld(ref, r0) for r0 in r0s), [])
        T, lkv, readout = _chunk_pre(cat(at_s), cat(rt_s), cat(bt_s), cat(kt_s), cat(v_s), masks, nsteps)
        for u, r0 in enumerate(r0s):
            for p in pairs:
                q = u * len(pairs) + p
                t_s[pl.ds(r0, C), lanes[p]] = T[q]
                lkv_s[pl.ds(r0, C), lanes[p]] = lkv[q]
                ro_s[pl.ds(r0, C), wide[p]] = readout[q]
        return carry

    def seq_body(ci_, carry):
        r0 = pl.multiple_of(ci_ * C, C)
        last8 = pl.multiple_of(r0 + C - 8, 8)
        wc = [jnp.exp(cum_s[pl.ds(last8, 8), lanes[p]][7:8]) for p in pairs]
        Y, S_new = _chunk_seq(ld(at_s, r0), ld(rt_s, r0), ld(be_s, r0), ld(ke_s, r0), ld(v_s, r0), wc, ld(t_s, r0),
                              ld(lkv_s, r0), ld(ro_s, r0, wide), [S_ref[p] for p in pairs], masks)
        for p in pairs:
            y_s[pl.ds(r0, C), lanes[p]] = Y[p]
            S_ref[p] = S_new[p]
        return carry

    lax.fori_loop(0, tt // C // npre, pre_body, 0)
    lax.fori_loop(0, tt // C, seq_body, 0)
    o_ref[...] = _rw_post(y_s[...], r, k2, v, g, P)

    @pl.when(t == pl.num_programs(1) - 1)
    def _():
        sout_ref[0] = S_ref[...]


def _const_spec(shape):
    nd = len(shape)
    return pl.BlockSpec(shape, lambda *_: (0,) * nd)


def _rwkv_prompt(c, sp_rkv, sp_lora, s0, params, nb, seq):
    tt = min(256, seq)
    chunk = min(64, tt)
    nt = seq // tt
    npairs = RW_WIDTH // LANES
    in_specs = [
        pl.BlockSpec((tt, RKV_W), lambda b, t: (b * nt + t, 0)),
        pl.BlockSpec((tt, LORA_W), lambda b, t: (b * nt + t, C_LORA // LORA_W)),
        pl.BlockSpec((1, 1, RKV_W), lambda b, t: (b, 0, 0)),
        pl.BlockSpec((1, 1, LORA_W), lambda b, t: (b, 0, 0)),
        pl.BlockSpec((1, npairs, LANES, LANES), lambda b, t: (b, 0, 0, 0)),
    ] + [_const_spec(params[n].shape) for n in RW_PARAM_NAMES]
    out_specs = [
        pl.BlockSpec((tt, RW_WIDTH), lambda b, t: (b * nt + t, 0)),
        pl.BlockSpec((1, npairs, LANES, LANES), lambda b, t: (b, 0, 0, 0)),
    ]
    scratch = [pltpu.VMEM((npairs, LANES, LANES), f32), pltpu.VMEM((1, RKV_W), f32), pltpu.VMEM((1, LORA_W), f32)]
    assert 2 * chunk == LANES, "a head pair's chunk matrices fill one 128-lane tile"
    scratch += [pltpu.VMEM((tt, RW_WIDTH), f32) for _ in range(11)] + [pltpu.VMEM((tt, 2 * RW_WIDTH), f32)]
    return pl.pallas_call(
        functools.partial(_rwkv_prompt_body, chunk),
        grid=(nb, nt),
        in_specs=in_specs,
        out_specs=out_specs,
        out_shape=[jax.ShapeDtypeStruct((nb * seq, RW_WIDTH), f32),
                   jax.ShapeDtypeStruct((nb, npairs, LANES, LANES), f32)],
        scratch_shapes=scratch,
        compiler_params=_cparams(("parallel", "arbitrary")),
        name="rwkv_prompt",
    )(c, c, sp_rkv, sp_lora, s0, *[params[n] for n in RW_PARAM_NAMES])


def _rwkv_prep_body(rkv_ref, lora_ref, prkv_ref, plora_ref, *rest):
    np_ = len(RW_PARAM_NAMES)
    P = {n: rest[i][...] for i, n in enumerate(RW_PARAM_NAMES)}
    outs = rest[np_:]
    vals = _rw_prep(rkv_ref[...], lora_ref[...], prkv_ref[...], plora_ref[...], P)
    for o, v in zip(outs, vals):
        o[...] = v


def _rwkv_prep(c, prev_rkv, prev_lora, params):
    m = c.shape[0]
    tm = min(512, m)
    in_specs = [
        pl.BlockSpec((tm, RKV_W), lambda i: (i, 0)),
        pl.BlockSpec((tm, LORA_W), lambda i: (i, C_LORA // LORA_W)),
        pl.BlockSpec((tm, RKV_W), lambda i: (i, 0)),
        pl.BlockSpec((tm, LORA_W), lambda i: (i, 0)),
    ] + [_const_spec(params[n].shape) for n in RW_PARAM_NAMES]
    return pl.pallas_call(
        _rwkv_prep_body,
        grid=(m // tm,),
        in_specs=in_specs,
        out_specs=[pl.BlockSpec((tm, RW_WIDTH), lambda i: (i, 0)) for _ in range(7)],
        out_shape=[jax.ShapeDtypeStruct((m, RW_WIDTH), f32) for _ in range(7)],
        compiler_params=_cparams(("parallel",)),
        name="rwkv_prep",
    )(c, c, prev_rkv, prev_lora, *[params[n] for n in RW_PARAM_NAMES])


def _rwkv_seq_body(r_ref, lw_ref, k_ref, v_ref, a_ref, b_ref, s_ref, y_ref, sout_ref):
    steps = r_ref.shape[0]
    w = [jnp.exp(lw_ref[t]) for t in range(steps)]
    for i in range(RW_HEAD):
        Si = s_ref[0, i]
        for t in range(steps):
            sa = jnp.sum(Si * a_ref[t], axis=0, keepdims=True)
            Si = Si * w[t] + sa * b_ref[t] + v_ref[t, i:i + 1, :] * k_ref[t]
            y_ref[t, i:i + 1, :] = jnp.sum(Si * r_ref[t], axis=0, keepdims=True)
        sout_ref[0, i] = Si


def _rwkv_seq(ops, s0, nreq, steps):
    ops_t = [o.reshape(nreq, steps, RW_WIDTH).transpose(1, 2, 0) for o in ops]
    vec = pl.BlockSpec((steps, RW_HEAD, nreq), lambda h: (0, h, 0))
    st = pl.BlockSpec((1, RW_HEAD, RW_HEAD, nreq), lambda h: (h, 0, 0, 0))
    y_t, s_t = pl.pallas_call(
        _rwkv_seq_body,
        grid=(RW_HEADS,),
        in_specs=[vec] * 6 + [st],
        out_specs=[vec, st],
        out_shape=[jax.ShapeDtypeStruct((steps, RW_WIDTH, nreq), f32),
                   jax.ShapeDtypeStruct((RW_HEADS, RW_HEAD, RW_HEAD, nreq), f32)],
        compiler_params=_cparams(("parallel",)),
        name="rwkv_seq",
    )(*ops_t, s0.transpose(1, 2, 3, 0))
    return y_t.transpose(2, 0, 1).reshape(nreq * steps, RW_WIDTH), s_t.transpose(3, 0, 1, 2)


def _rwkv_post_body(y_ref, r_ref, k_ref, v_ref, g_ref, *rest):
    np_ = len(RW_PARAM_NAMES)
    P = {n: rest[i][...] for i, n in enumerate(RW_PARAM_NAMES)}
    rest[np_][...] = _rw_post(y_ref[...], r_ref[...], k_ref[...], v_ref[...], g_ref[...], P)


def _rwkv_post(y, r, k2, v, g, params):
    m = y.shape[0]
    tm = min(512, m)
    return pl.pallas_call(
        _rwkv_post_body,
        grid=(m // tm,),
        in_specs=[pl.BlockSpec((tm, RW_WIDTH), lambda i: (i, 0)) for _ in range(5)]
        + [_const_spec(params[n].shape) for n in RW_PARAM_NAMES],
        out_specs=pl.BlockSpec((tm, RW_WIDTH), lambda i: (i, 0)),
        out_shape=jax.ShapeDtypeStruct((m, RW_WIDTH), f32),
        compiler_params=_cparams(("parallel",)),
        name="rwkv_post",
    )(y, r, k2, v, g, *[params[n] for n in RW_PARAM_NAMES])


def _sort_key(score):
    bits = pltpu.bitcast(score, i32)
    key = jnp.where(bits < 0, bits ^ jnp.int32(0x7FFFFFFF), bits)
    return jnp.where(score == 0.0, jnp.int32(0), key)


def _kth_largest(keys, k):
    def body(i, ts):
        bit = lax.shift_left(jnp.int32(1), jnp.int32(31) - i)
        cands = [t + bit for t in ts]
        cnts = [jnp.sum((key >= c).astype(f32), axis=1, keepdims=True) for key, c in zip(keys, cands)]
        return tuple(jnp.where(n >= k, c, t) for n, c, t in zip(cnts, cands, ts))

    init = tuple(jnp.full((key.shape[0], 1), INT_MIN, i32) for key in keys)
    return lax.fori_loop(0, 32, body, init, unroll=4)


def _row_groups(x, n):
    step = x.shape[0] // n
    return [x[g * step:(g + 1) * step] for g in range(n)]


def _dsa_prompt_body(topk, q_ref, qi_ref, wi_ref, k_ref, v_ref, ki_ref, cosk_ref, sink_ref, cosi_ref, sini_ref,
                     o_ref, kout_ref, kiout_ref, kb_ref, vb_ref, kib_ref, bias_ref):
    qb = pl.program_id(1)
    seq = k_ref.shape[0]
    nq = q_ref.shape[0]

    @pl.when(qb == 0)
    def _():
        cos, sin = cosk_ref[...], sink_ref[...]
        kr = jnp.concatenate([_rope(k_ref[:, h * HEAD_DIM:(h + 1) * HEAD_DIM], cos, sin, HEAD_DIM // 2)
                              for h in range(KV_HEADS)], axis=1)
        kout_ref[...] = kr
        kb_ref[...] = kr.astype(bf16)
        vb_ref[...] = v_ref[...].astype(bf16)
        kir = _rope(ki_ref[...], cosi_ref[...], sini_ref[...], IDX_DIM // 2)
        kiout_ref[...] = kir[:, :IDX_DIM]
        kib_ref[...] = (kir + pltpu.roll(kir, IDX_DIM, axis=1)).astype(bf16)

    r0 = pl.multiple_of(qb * nq, nq)
    cosq, sinq = cosk_ref[pl.ds(r0, nq), :], sink_ref[pl.ds(r0, nq), :]
    cosqi, sinqi = cosi_ref[pl.ds(r0, nq), :], sini_ref[pl.ds(r0, nq), :]
    q = _rope(q_ref[...], cosq, sinq, HEAD_DIM // 2).astype(bf16)
    qi = _rope(qi_ref[...], cosqi, sinqi, IDX_DIM // 2)
    wi = wi_ref[...] * IDX_W_SCALE
    lane = lax.broadcasted_iota(i32, (1, LANES), 1)
    scale = HEAD_DIM ** -0.5 * math.log2(math.e)
    hpg = ATT_HEADS // KV_HEADS

    def process(ext):
        kib = kib_ref[:ext, :]
        sc = None
        for h in range(IDX_HEADS):
            pair = qi[:, (h // 2) * LANES:(h // 2 + 1) * LANES]
            mine = (lane < IDX_DIM) if h % 2 == 0 else (lane >= IDX_DIM)
            term = jnp.maximum(_bdot_nt(jnp.where(mine, pair, 0.0), kib), 0.0) * wi[:, h:h + 1]
            sc = term if sc is None else sc + term

        qpos = r0 + lax.broadcasted_iota(i32, (nq, 1), 0)
        kpos = lax.broadcasted_iota(i32, (1, ext), 1)
        causal = kpos <= qpos
        key = jnp.where(causal, _sort_key(sc), INT_MIN)
        thr = jnp.concatenate(_kth_largest(_row_groups(key, 4), float(topk)), axis=0)
        bias_ref[:, :ext] = jnp.where(causal & (key >= thr), 0.0, NEG_BIG)

        def qk(h):
            kg = kb_ref[:ext, (h // hpg) * HEAD_DIM:(h // hpg + 1) * HEAD_DIM]
            return _bdot_nt(q[:, h * HEAD_DIM:(h + 1) * HEAD_DIM], kg)

        s_next = qk(0)
        for h in range(ATT_HEADS):
            s = s_next * scale + bias_ref[:, :ext]
            if h + 1 < ATT_HEADS:
                s_next = qk(h + 1)
            m = jnp.max(s, axis=1, keepdims=True)
            p = jnp.exp2(s - m)
            l = jnp.sum(p, axis=1, keepdims=True)
            vg = vb_ref[:ext, (h // hpg) * HEAD_DIM:(h // hpg + 1) * HEAD_DIM]
            o = jnp.dot(p.astype(bf16), vg, preferred_element_type=f32)
            o_ref[:, h * HEAD_DIM:(h + 1) * HEAD_DIM] = o / l

    nvar = next(v for v in (8, 4, 2, 1) if (seq // nq) % v == 0)
    per = seq // nq // nvar
    for var in range(nvar):
        pl.when(qb // per == var)(functools.partial(process, (var + 1) * per * nq))


def _dsa_prompt(c, tabs, nb, seq):
    nq = Q_BLOCK
    nblk = seq // nq
    topk = min(TOPK_MAX, seq // 4)
    cosk, sink, cosi, sini = tabs
    row = lambda b, j: b * nblk + j
    in_specs = [
        pl.BlockSpec((nq, ATT_WIDTH), lambda b, j: (row(b, j), C_Q // ATT_WIDTH)),
        pl.BlockSpec((nq, IDX_HEADS * IDX_DIM), lambda b, j: (row(b, j), C_QI // (IDX_HEADS * IDX_DIM))),
        pl.BlockSpec((nq, LANES), lambda b, j: (row(b, j), C_WI // LANES)),
        pl.BlockSpec((seq, KV_WIDTH), lambda b, j: (b, C_AK // KV_WIDTH)),
        pl.BlockSpec((seq, KV_WIDTH), lambda b, j: (b, C_AV // KV_WIDTH)),
        pl.BlockSpec((seq, LANES), lambda b, j: (b, C_KI // LANES)),
    ] + [_const_spec((seq, LANES)) for _ in range(4)]
    out_specs = [
        pl.BlockSpec((nq, ATT_WIDTH), lambda b, j: (row(b, j), 0)),
        pl.BlockSpec((seq, KV_WIDTH), lambda b, j: (b, 0)),
        pl.BlockSpec((seq, IDX_DIM), lambda b, j: (b, 0)),
    ]
    scratch = [pltpu.VMEM((seq, KV_WIDTH), bf16), pltpu.VMEM((seq, KV_WIDTH), bf16), pltpu.VMEM((seq, LANES), bf16),
               pltpu.VMEM((nq, seq), f32)]
    return pl.pallas_call(
        functools.partial(_dsa_prompt_body, topk),
        grid=(nb, nblk),
        in_specs=in_specs,
        out_specs=out_specs,
        out_shape=[jax.ShapeDtypeStruct((nb * seq, ATT_WIDTH), f32),
                   jax.ShapeDtypeStruct((nb * seq, KV_WIDTH), f32),
                   jax.ShapeDtypeStruct((nb * seq, IDX_DIM), f32)],
        scratch_shapes=scratch,
        compiler_params=_cparams(("parallel", "arbitrary")),
        name="dsa_prompt",
    )(c, c, c, c, c, c, cosk, sink, cosi, sini)


ROWS = 8


def _page_copies(pt_ref, req, n_pages, srcs_dsts_sems):
    out = []
    for p in range(n_pages):
        pg = pt_ref[req, p]
        for hbm, dst, sem in srcs_dsts_sems:
            out.append(pltpu.make_async_copy(hbm.at[pg], dst(p), sem))
    return out


def _dsa_select_body(topk, n_new, group, pt_ref, qi_ref, wrow_ref, kin_ref, cosi_ref, sini_ref, cki_hbm,
                     sel_ref, kiout_ref, kibuf, sems):
    i = pl.program_id(0)
    nsteps = pl.num_programs(0)
    n_pages = pt_ref.shape[1]
    page = cki_hbm.shape[2]
    past = n_pages * page
    G = group

    def copies(step, slot):
        out = []
        for j in range(G):
            out += _page_copies(pt_ref, step * G + j, n_pages,
                                [(cki_hbm, lambda p, j=j: kibuf.at[slot, j, :, pl.ds(p * page, page)],
                                  sems.at[slot])])
        return out

    slot = i % 2

    @pl.when(i == 0)
    def _():
        for cp in copies(0, 0):
            cp.start()

    @pl.when(i + 1 < nsteps)
    def _():
        for cp in copies(i + 1, 1 - slot):
            cp.start()

    cosi, sini = cosi_ref[...], sini_ref[...]
    li = lax.broadcasted_iota(i32, (LANES, LANES), 0)
    lo = lax.broadcasted_iota(i32, (LANES, LANES), 1)
    eye = (li == lo).astype(f32)
    qis, kins, wcols = [], [], []
    for j in range(G):
        qi = _rope(qi_ref[j], cosi, sini, IDX_DIM // 2)
        kin = _rope(kin_ref[j], cosi, sini, IDX_DIM // 2)[:, :IDX_DIM]
        kiout_ref[j] = kin
        kins.append(jnp.concatenate([kin, jnp.zeros((LANES - ROWS, IDX_DIM), f32)], axis=0))
        qis.append(jnp.concatenate([qi[:, h * IDX_DIM:(h + 1) * IDX_DIM] for h in range(IDX_HEADS)], axis=0))
        wrow = wrow_ref[0, :, j * LANES:(j + 1) * LANES] * IDX_W_SCALE
        wcols.append(jnp.sum(eye * wrow, axis=1, keepdims=True))

    for cp in copies(i, slot):
        cp.wait()

    rows = []
    for j in range(G):
        d = jnp.concatenate([_bdot(qis[j], kibuf[slot, j]), _bdot_nt(qis[j], kins[j])], axis=1)
        term = jnp.maximum(d, 0.0) * wcols[j]
        rows.append(sum(term[h * ROWS:(h + 1) * ROWS] for h in range(IDX_HEADS)))
    sc = jnp.concatenate(rows, axis=0)
    nrow = ROWS * G
    width = past + LANES
    t_row = lax.broadcasted_iota(i32, (nrow, width), 0) % ROWS
    col = lax.broadcasted_iota(i32, (nrow, width), 1)
    valid = (col < past) | ((col - past <= t_row) & (col - past < n_new))
    key = jnp.where(valid, _sort_key(sc), INT_MIN)
    thr = jnp.concatenate(_kth_largest(_row_groups(key, G), float(topk)), axis=0)
    chosen = (valid & (key >= thr)).astype(f32)
    if nrow < LANES:
        chosen = jnp.concatenate([chosen, jnp.zeros((LANES - nrow, width), f32)], axis=0)
    sel_ref[0] = chosen.T


def _dsa_sample_body(n_new, group, pt_ref, q_ref, kn_ref, vn_ref, sel_ref, cosk_ref, sink_ref, ck_hbm, cv_hbm,
                     o_ref, kout_ref, kbuf, vbuf, sems):
    b = pl.program_id(0)
    nreq = pl.num_programs(0)
    n_pages = pt_ref.shape[1]
    prow = ck_hbm.shape[1]
    past = n_pages * prow // KV_HEADS
    G = group

    def copies(req, slot):
        return _page_copies(pt_ref, req, n_pages,
                            [(ck_hbm, lambda p: kbuf.at[slot, pl.ds(p * prow, prow)], sems.at[0, slot]),
                             (cv_hbm, lambda p: vbuf.at[slot, pl.ds(p * prow, prow)], sems.at[1, slot])])

    slot = b % 2

    @pl.when(b == 0)
    def _():
        for cp in copies(0, 0):
            cp.start()

    @pl.when(b + 1 < nreq)
    def _():
        for cp in copies(b + 1, 1 - slot):
            cp.start()

    cosk, sink = cosk_ref[...], sink_ref[...]
    q = _rope(q_ref[0], cosk, sink, HEAD_DIM // 2)
    kn = jnp.concatenate([_rope(kn_ref[0][:, h * HEAD_DIM:(h + 1) * HEAD_DIM], cosk, sink, HEAD_DIM // 2)
                          for h in range(KV_HEADS)], axis=1)
    vn = vn_ref[0]
    kout_ref[0] = kn

    li = lax.broadcasted_iota(i32, (LANES, LANES), 0)
    lo = lax.broadcasted_iota(i32, (LANES, LANES), 1)
    route = ((b % G) * ROWS + lo % ROWS == li) & (lo % ROWS < n_new)
    chosen = jnp.dot(sel_ref[0].astype(bf16), route.astype(bf16), preferred_element_type=f32)
    bias = (chosen - 1.0) * (-NEG_BIG)
    bias_p, bias_n = bias[:past], bias[past:past + ROWS]

    eye = (li == lo).astype(f32)
    scale = HEAD_DIM ** -0.5 * math.log2(math.e)
    hpg = ATT_HEADS // KV_HEADS
    zero_rows = lambda n: [jnp.zeros((n, HEAD_DIM), f32)] if n else []

    for cp in copies(b, slot):
        cp.wait()

    s_p = s_n = None
    for g in range(KV_HEADS):
        heads = [q[:, (g * hpg + r) * HEAD_DIM:(g * hpg + r + 1) * HEAD_DIM] for r in range(hpg)]
        qg = jnp.concatenate(zero_rows(g * hpg * ROWS) + heads + zero_rows(LANES - (g + 1) * hpg * ROWS), axis=0)
        kp = kbuf[slot, pl.ds(g, past, stride=KV_HEADS), :]
        part_p = _bdot_nt(kp, qg)
        part_n = _bdot_nt(kn[:, g * HEAD_DIM:(g + 1) * HEAD_DIM], qg)
        s_p = part_p if s_p is None else s_p + part_p
        s_n = part_n if s_n is None else s_n + part_n
    s_p = s_p * scale + bias_p
    s_n = s_n * scale + bias_n
    m = jnp.maximum(jnp.max(s_p, axis=0, keepdims=True), jnp.max(s_n, axis=0, keepdims=True))
    p_p = jnp.exp2(s_p - m)
    p_n = jnp.exp2(s_n - m)
    l = jnp.sum(p_p, axis=0, keepdims=True) + jnp.sum(p_n, axis=0, keepdims=True)
    l_col = jnp.sum(eye * l, axis=1, keepdims=True)
    for g in range(KV_HEADS):
        vp = vbuf[slot, pl.ds(g, past, stride=KV_HEADS), :]
        o = (_bdot_tn(p_p, vp) + _bdot_tn(p_n, vn[:, g * HEAD_DIM:(g + 1) * HEAD_DIM])) / l_col
        for r in range(hpg):
            h = g * hpg + r
            o_ref[0, :, h * HEAD_DIM:(h + 1) * HEAD_DIM] = o[h * ROWS:(h + 1) * ROWS]


def _dsa_sample(csel, wrow, tabs, page_table, cache_k, cache_v, cache_kidx, n_new):
    q8, qi8, kn8, vn8, kin8 = csel
    nreq = q8.shape[0]
    n_pages = page_table.shape[1]
    page = cache_kidx.shape[2]
    past = n_pages * page
    topk = min(TOPK_MAX, (past + n_new) // 4)
    group = min(LANES // ROWS, nreq)
    cosk, sink, cosi, sini = tabs
    anyspec = pl.BlockSpec(memory_space=pl.ANY)
    tab = pl.BlockSpec((ROWS, LANES), lambda b, pt: (0, 0))

    grp3 = lambda w: pl.BlockSpec((group, ROWS, w), lambda i, pt: (i, 0, 0))
    sel, ki_new = pl.pallas_call(
        functools.partial(_dsa_select_body, topk, n_new, group),
        grid_spec=pltpu.PrefetchScalarGridSpec(
            num_scalar_prefetch=1,
            grid=(nreq // group,),
            in_specs=[grp3(IDX_HEADS * IDX_DIM), pl.BlockSpec((1, 1, group * LANES), lambda i, pt: (i, 0, 0)),
                      grp3(LANES), tab, tab, anyspec],
            out_specs=[pl.BlockSpec((1, past + LANES, LANES), lambda i, pt: (i, 0, 0)), grp3(IDX_DIM)],
            scratch_shapes=[pltpu.VMEM((2, group, IDX_DIM, past), f32), pltpu.SemaphoreType.DMA((2,))],
        ),
        out_shape=[jax.ShapeDtypeStruct((nreq // group, past + LANES, LANES), f32),
                   jax.ShapeDtypeStruct((nreq, ROWS, IDX_DIM), f32)],
        compiler_params=_cparams(("arbitrary",)),
        name="dsa_select",
    )(page_table, qi8, wrow.reshape(nreq // group, 1, group * LANES), kin8, cosi, sini, cache_kidx)

    req3 = lambda w: pl.BlockSpec((1, ROWS, w), lambda b, pt: (b, 0, 0))
    prow = cache_k.shape[1]
    o, k_new = pl.pallas_call(
        functools.partial(_dsa_sample_body, n_new, group),
        grid_spec=pltpu.PrefetchScalarGridSpec(
            num_scalar_prefetch=1,
            grid=(nreq,),
            in_specs=[req3(ATT_WIDTH), req3(KV_WIDTH), req3(KV_WIDTH),
                      pl.BlockSpec((1, past + LANES, LANES), lambda b, pt: (b // group, 0, 0)), tab, tab,
                      anyspec, anyspec],
            out_specs=[req3(ATT_WIDTH), req3(KV_WIDTH)],
            scratch_shapes=[pltpu.VMEM((2, n_pages * prow, HEAD_DIM), f32),
                            pltpu.VMEM((2, n_pages * prow, HEAD_DIM), f32), pltpu.SemaphoreType.DMA((2, 2))],
        ),
        out_shape=[jax.ShapeDtypeStruct((nreq, ROWS, ATT_WIDTH), f32),
                   jax.ShapeDtypeStruct((nreq, ROWS, KV_WIDTH), f32)],
        compiler_params=_cparams(("arbitrary",)),
        name="dsa_sample",
    )(page_table, q8, kn8, vn8, sel, cosk, sink, cache_k, cache_v)
    return o, k_new, ki_new


def _merge_body(orw_ref, oatt_ref, grw_ref, gatt_ref, prw_ref, patt_ref, o_ref):
    a = jnp.dot(orw_ref[...].astype(bf16), prw_ref[...], preferred_element_type=f32)
    b = jnp.dot(oatt_ref[...].astype(bf16), patt_ref[...], preferred_element_type=f32)
    o_ref[...] = (jax.nn.sigmoid(grw_ref[...]) * a + jax.nn.sigmoid(gatt_ref[...]) * b).astype(bf16)


def _merge(o_rw, o_att, c, p_rw, p_att):
    m = o_rw.shape[0]
    d = p_rw.shape[1]
    tm = 1024 if m % 1024 == 0 else min(512, m)
    tn = 1024
    nj = d // tn
    return pl.pallas_call(
        _merge_body,
        grid=(m // tm, nj),
        in_specs=[
            pl.BlockSpec((tm, RW_WIDTH), lambda i, j: (i, 0)),
            pl.BlockSpec((tm, ATT_WIDTH), lambda i, j: (i, 0)),
            pl.BlockSpec((tm, tn), lambda i, j: (i, C_GRW // tn + j)),
            pl.BlockSpec((tm, tn), lambda i, j: (i, C_GATT // tn + j)),
            pl.BlockSpec((RW_WIDTH, tn), lambda i, j: (0, j)),
            pl.BlockSpec((ATT_WIDTH, tn), lambda i, j: (0, j)),
        ],
        out_specs=pl.BlockSpec((tm, tn), lambda i, j: (i, j)),
        out_shape=jax.ShapeDtypeStruct((m, d), bf16),
        compiler_params=_cparams(("parallel", "arbitrary")),
        name="merge",
    )(o_rw, o_att, c, c, p_rw, p_att)


def _outproj_body(mg_ref, x_ref, w_ref, o_ref):
    o_ref[...] = x_ref[...] + jnp.dot(mg_ref[...], w_ref[...], preferred_element_type=f32)


def _outproj(merged, x2d, w_o):
    m, d = x2d.shape
    tm = 1024 if m % 1024 == 0 else min(512, m)
    tn = 1024
    return pl.pallas_call(
        _outproj_body,
        grid=(m // tm, d // tn),
        in_specs=[
            pl.BlockSpec((tm, d), lambda i, j: (i, 0)),
            pl.BlockSpec((tm, tn), lambda i, j: (i, j)),
            pl.BlockSpec((d, tn), lambda i, j: (0, j)),
        ],
        out_specs=pl.BlockSpec((tm, tn), lambda i, j: (i, j)),
        out_shape=jax.ShapeDtypeStruct((m, d), f32),
        compiler_params=_cparams(("parallel", "arbitrary")),
        name="out_proj",
    )(merged, x2d, w_o)


def _convglu_body(shift, tiles_per_seq, final_norm, h_ref, g2_ref, gf_ref, cprev_ref, wg_ref, wu_ref, cw_ref,
                  cb_ref, wd_ref, o_ref, tail_ref, n_ref, ext_ref, carry_ref, act_ref):
    i = pl.program_id(0)
    j = pl.program_id(1)
    nj = pl.num_programs(1) - 1
    tm = h_ref.shape[0]
    hist = 2 * shift
    base = ext_ref.shape[0] - tm

    def up_stage():
        n = n_ref[...]
        gate = jnp.dot(n, wg_ref[...], preferred_element_type=f32)
        up = jnp.dot(n, wu_ref[...], preferred_element_type=f32)
        ext_ref[base - hist:base, :] = jnp.where(i % tiles_per_seq == 0, cprev_ref[0], carry_ref[j])
        ext_ref[base:, :] = gate
        cw = cw_ref[...]
        c = (cb_ref[...] + ext_ref[base - hist:base - hist + tm, :] * cw[0:1, :]
             + ext_ref[base - shift:base - shift + tm, :] * cw[1:2, :] + gate * cw[2:3, :])
        tail = ext_ref[base + tm - hist:base + tm, :]
        tail_ref[0] = tail
        carry_ref[j] = tail
        act_ref[j % 2] = ((c * jax.nn.sigmoid(c)) * up).astype(bf16)

    def down_stage():
        o_ref[...] += jnp.dot(act_ref[(j + 1) % 2], wd_ref[...], preferred_element_type=f32)

    @pl.when(j == 0)
    def _():
        @pl.when(i == 0)
        def _():
            carry_ref[...] = jnp.zeros_like(carry_ref)

        h = h_ref[...]
        ms = jnp.mean(h * h, axis=-1, keepdims=True)
        n_ref[...] = (h * lax.rsqrt(ms + RMS_EPS) * g2_ref[...]).astype(bf16)
        o_ref[...] = jnp.zeros_like(o_ref)
        up_stage()

    @pl.when((j > 0) & (j < nj))
    def _():
        up_stage()
        down_stage()

    @pl.when(j == nj)
    def _():
        down_stage()
        out = h_ref[...] + o_ref[...]
        if final_norm:
            ms = jnp.mean(out * out, axis=-1, keepdims=True)
            out = out * lax.rsqrt(ms + RMS_EPS) * gf_ref[...]
        o_ref[...] = out


def _convglu(h2d, conv_prev, norm2, norm_f, w_up_b, conv_w, conv_b, w_down_b, nseq_groups, shift, final_norm):
    m, d = h2d.shape
    d_ff = w_down_b.shape[0]
    rows_per_group = m // nseq_groups
    tm = 1024 if rows_per_group % 1024 == 0 else min(512, rows_per_group)
    tf = 512
    nj = d_ff // tf
    tiles_per_seq = rows_per_group // tm
    hist = 2 * shift
    base = ((hist + 7) // 8) * 8
    up_j = lambda j: jnp.minimum(j, nj - 1)
    down_j = lambda j: jnp.maximum(j - 1, 0)
    once = pl.Buffered(1)
    out, tails = pl.pallas_call(
        functools.partial(_convglu_body, shift, tiles_per_seq, final_norm),
        grid=(m // tm, nj + 1),
        in_specs=[
            pl.BlockSpec((tm, d), lambda i, j: (i, 0)),
            pl.BlockSpec((1, d), lambda i, j: (0, 0)),
            pl.BlockSpec((1, d), lambda i, j: (0, 0)),
            pl.BlockSpec((1, hist, tf), lambda i, j: (i // tiles_per_seq, 0, up_j(j))),
            pl.BlockSpec((d, tf), lambda i, j: (0, up_j(j))),
            pl.BlockSpec((d, tf), lambda i, j: (0, nj + up_j(j))),
            pl.BlockSpec((CONV_W, tf), lambda i, j: (0, up_j(j))),
            pl.BlockSpec((1, tf), lambda i, j: (0, up_j(j))),
            pl.BlockSpec((tf, d), lambda i, j: (down_j(j), 0)),
        ],
        out_specs=[
            pl.BlockSpec((tm, d), lambda i, j: (i, 0), pipeline_mode=once),
            pl.BlockSpec((1, hist, tf), lambda i, j: (i, 0, up_j(j))),
        ],
        out_shape=[jax.ShapeDtypeStruct((m, d), f32), jax.ShapeDtypeStruct((m // tm, hist, d_ff), f32)],
        scratch_shapes=[pltpu.VMEM((tm, d), bf16), pltpu.VMEM((base + tm, tf), f32),
                        pltpu.VMEM((nj, hist, tf), f32), pltpu.VMEM((2, tm, tf), bf16)],
        compiler_params=_cparams(("arbitrary", "arbitrary")),
        name="convglu",
    )(h2d, norm2, norm_f, conv_prev, w_up_b, w_up_b, conv_w, conv_b, w_down_b)
    return out, tails[tiles_per_seq - 1::tiles_per_seq]


def _pad_cols(a, width):
    return jnp.pad(a, [(0, 0)] * (a.ndim - 1) + [(0, width - a.shape[-1])])


def _to_layout_rows(a):
    def pad_rows(x, n):
        return jnp.pad(x, ((0, n - x.shape[0]), (0, 0)))

    rw = 3 * RW_WIDTH
    o = {}
    o["r"], o["k"], o["v"] = a[0:RW_WIDTH], a[RW_WIDTH:2 * RW_WIDTH], a[2 * RW_WIDTH:rw]
    p = rw
    o["wl"] = a[p:p + D_DECAY_LORA]; p += D_DECAY_LORA
    o["al"] = a[p:p + D_AAA_LORA]; p += D_AAA_LORA
    o["gl"] = a[p:p + D_GATE_LORA]; p += D_GATE_LORA
    o["q"] = a[p:p + ATT_WIDTH]; p += ATT_WIDTH
    o["ak"] = a[p:p + KV_WIDTH]; p += KV_WIDTH
    o["av"] = a[p:p + KV_WIDTH]; p += KV_WIDTH
    o["qi"] = a[p:p + IDX_HEADS * IDX_DIM]; p += IDX_HEADS * IDX_DIM
    o["ki"] = a[p:p + IDX_DIM]; p += IDX_DIM
    o["wi"] = a[p:p + IDX_HEADS]; p += IDX_HEADS
    d = (a.shape[0] - p) // 2
    o["grw"], o["gatt"] = a[p:p + d], a[p + d:p + 2 * d]
    return jnp.concatenate([
        o["r"], o["k"], o["v"], o["q"], o["grw"], o["gatt"], o["qi"],
        pad_rows(o["wl"], LANES), pad_rows(o["al"], LANES), o["gl"],
        pad_rows(o["ki"], LANES), o["ak"], o["av"], pad_rows(o["wi"], LANES)], axis=0)


def _rw_cols_layout(a):
    rw = 3 * RW_WIDTH
    wl = a[..., rw:rw + D_DECAY_LORA]
    al = a[..., rw + D_DECAY_LORA:rw + D_DECAY_LORA + D_AAA_LORA]
    gl = a[..., rw + D_DECAY_LORA + D_AAA_LORA:]
    return a[..., :rw], jnp.concatenate([_pad_cols(wl, LANES), _pad_cols(al, LANES), gl], axis=-1)


def _rw_cols_from_layout(c):
    return jnp.concatenate([c[..., :3 * RW_WIDTH], c[..., C_LORA:C_LORA + D_DECAY_LORA],
                            c[..., C_LORA + LANES:C_LORA + LANES + D_AAA_LORA],
                            c[..., C_LORA + 2 * LANES:C_LORA + 3 * LANES]], axis=-1)


def _rope_tables(pos, rows):
    pos = jnp.pad(pos.astype(f32), (0, rows - pos.shape[0]))
    out = []
    for dim in (HEAD_DIM, IDX_DIM):
        half = dim // 2
        inv_freq = 1.0 / (ROPE_THETA ** (jnp.arange(half, dtype=f32) / half))
        ang = pos[:, None] * inv_freq[None, :]
        cos, sin = jnp.cos(ang), jnp.sin(ang)
        reps = LANES // dim
        out.append(jnp.tile(jnp.concatenate([cos, cos], axis=1), (1, reps)))
        out.append(jnp.tile(jnp.concatenate([-sin, sin], axis=1), (1, reps)))
    return tuple(out)


def _pair_unblock(s):
    n = s.shape[0]
    a = s[:, :, :RW_HEAD, :RW_HEAD]
    b = s[:, :, RW_HEAD:, RW_HEAD:]
    return jnp.stack([a, b], axis=2).reshape(n, RW_HEADS, RW_HEAD, RW_HEAD)


def kernel(x_prompt, x_sample, cache_k, cache_v, cache_kidx, state_wkv, state_shift, state_conv, page_table, norm1, w_in, rw_mu, rw_w0, rw_w2, rw_a0, rw_a2, rw_g2, rw_k_k, rw_k_a, rw_r_k, rw_lnx_w, rw_lnx_b, p_rw, p_att, w_o, norm2, w_up, conv_w, conv_b, w_down, norm_f):
    B, S, D = x_prompt.shape
    DB, DS, _ = x_sample.shape
    depth = w_in.shape[0]
    page = cache_k.shape[2]
    n_pages = page_table.shape[1]
    past_len = n_pages * page
    d_ff = w_down.shape[1]
    dt = x_prompt.dtype

    tabs_p = _rope_tables(jnp.arange(S), S)
    tabs_s = _rope_tables(past_len + jnp.arange(DS), ROWS)
    row2 = lambda a: a.reshape(1, -1)

    hp = x_prompt.reshape(B * S, D)
    hs = x_sample.reshape(DB * DS, D)
    outs = {k: [] for k in ("kp", "vp", "kip", "ks", "vs", "kis", "wkvp", "wkvs", "shp", "shs", "cvp", "cvs")}
    for l in range(depth):
        w_in_t = _to_layout_rows(w_in[l].T).astype(bf16)
        mu_rkv, mu_lora = _rw_cols_layout(rw_mu[l][None, :])
        params = dict(
            mu_rkv=mu_rkv, mu_lora=mu_lora, w0=row2(rw_w0[l]),
            w2=jnp.pad(rw_w2[l], ((0, LANES - D_DECAY_LORA), (0, 0))).astype(bf16), a0=row2(rw_a0[l]),
            a2=jnp.pad(rw_a2[l], ((0, LANES - D_AAA_LORA), (0, 0))).astype(bf16), g2=rw_g2[l].astype(bf16),
            k_k=row2(rw_k_k[l]), k_a=row2(rw_k_a[l]), r_k=row2(rw_r_k[l]), lnx_w=row2(rw_lnx_w[l]),
            lnx_b=row2(rw_lnx_b[l]))
        last = l == depth - 1

        c_p = _in_proj(hp, row2(norm1[l]), w_in_t)
        c_s = _in_proj(hs, row2(norm1[l]), w_in_t)

        sp_rkv = jnp.zeros((B, 1, RKV_W), dt)
        sp_lora = jnp.zeros((B, 1, LORA_W), dt)
        s0_p = jnp.zeros((B, RW_HEADS // 2, LANES, LANES), dt)
        o_rw_p, wkv_p = _rwkv_prompt(c_p, sp_rkv, sp_lora, s0_p, params, B, S)

        c_s3 = c_s.reshape(DB, DS, NP_COLS)
        ss_rkv, ss_lora = _rw_cols_layout(state_shift[l])
        prev_rkv = jnp.concatenate([ss_rkv[:, None, :], c_s3[:, :-1, :RKV_W]], axis=1).reshape(DB * DS, RKV_W)
        prev_lora = jnp.concatenate([ss_lora[:, None, :], c_s3[:, :-1, C_LORA:C_LORA + LORA_W]], axis=1)
        prev_lora = prev_lora.reshape(DB * DS, LORA_W)
        r_s, lw_s, k_s, v_s, a_s, b_s, g_s = _rwkv_prep(c_s, prev_rkv, prev_lora, params)
        y_s, wkv_s = _rwkv_seq((r_s, lw_s, k_s, v_s, a_s, b_s), state_wkv[l], DB, DS)
        o_rw_s = _rwkv_post(y_s, r_s, k_s, v_s, g_s, params)

        o_att_p, k_p, ki_p = _dsa_prompt(c_p, tabs_p, B, S)

        def rows8(lo, w):
            return jnp.pad(c_s3[:, :, lo:lo + w], ((0, 0), (0, ROWS - DS), (0, 0)))

        csel = (rows8(C_Q, ATT_WIDTH), rows8(C_QI, IDX_HEADS * IDX_DIM), rows8(C_AK, KV_WIDTH),
                rows8(C_AV, KV_WIDTH), rows8(C_KI, LANES))
        wi_s = jnp.pad(c_s3[:, :, C_WI:C_WI + IDX_HEADS], ((0, 0), (0, ROWS - DS), (0, 0)))
        wrow = jnp.transpose(wi_s, (0, 2, 1)).reshape(DB, 1, IDX_HEADS * ROWS)
        o_att_s8, k_s8, ki_s8 = _dsa_sample(
            csel, wrow, tabs_s, page_table, cache_k[l].reshape(-1, page * KV_HEADS, HEAD_DIM),
            cache_v[l].reshape(-1, page * KV_HEADS, HEAD_DIM), jnp.swapaxes(cache_kidx[l], 1, 2), DS)
        o_att_s = o_att_s8[:, :DS].reshape(DB * DS, ATT_WIDTH)

        p_rw_b, p_att_b, w_o_b = p_rw[l].astype(bf16), p_att[l].astype(bf16), w_o[l].astype(bf16)
        h_p = _outproj(_merge(o_rw_p, o_att_p, c_p, p_rw_b, p_att_b), hp, w_o_b)
        h_s = _outproj(_merge(o_rw_s, o_att_s, c_s, p_rw_b, p_att_b), hs, w_o_b)

        w_up_b, w_down_b = w_up[l].astype(bf16), w_down[l].astype(bf16)
        cv_args = (row2(norm2[l]), row2(norm_f), w_up_b, conv_w[l], row2(conv_b[l]), w_down_b)
        hp, tail_p = _convglu(h_p, jnp.zeros((B, CONV_W - 1, d_ff), dt), *cv_args, B, 1, last)
        h_s_tm = h_s.reshape(DB, DS, D).transpose(1, 0, 2).reshape(DS * DB, D)
        cprev_tm = state_conv[l].transpose(1, 0, 2).reshape(1, (CONV_W - 1) * DB, d_ff)
        hs_tm, tail_s = _convglu(h_s_tm, cprev_tm, *cv_args, 1, DB, last)
        hs = hs_tm.reshape(DS, DB, D).transpose(1, 0, 2).reshape(DB * DS, D)

        outs["kp"].append(k_p.reshape(B, S // page, page, KV_HEADS, HEAD_DIM))
        outs["vp"].append(c_p[:, C_AV:C_AV + KV_WIDTH].reshape(B, S // page, page, KV_HEADS, HEAD_DIM))
        outs["kip"].append(ki_p.reshape(B, S // page, page, IDX_DIM))
        outs["ks"].append(k_s8[:, :DS].reshape(DB, DS, KV_HEADS, HEAD_DIM))
        outs["vs"].append(c_s3[:, :, C_AV:C_AV + KV_WIDTH].reshape(DB, DS, KV_HEADS, HEAD_DIM))
        outs["kis"].append(ki_s8[:, :DS])
        outs["wkvp"].append(_pair_unblock(wkv_p))
        outs["wkvs"].append(wkv_s)
        outs["shp"].append(_rw_cols_from_layout(c_p.reshape(B, S, NP_COLS)[:, -1]))
        outs["shs"].append(_rw_cols_from_layout(c_s3[:, -1]))
        outs["cvp"].append(tail_p)
        outs["cvs"].append(tail_s.reshape(CONV_W - 1, DB, d_ff).transpose(1, 0, 2))

    y_prompt = hp.reshape(B, S, D)
    y_sample = hs.reshape(DB, DS, D)
    st = lambda k: jnp.stack(outs[k])
    return (y_prompt, y_sample, st("kp"), st("vp"), st("kip"), st("ks"), st("vs"), st("kis"), st("wkvp"),
            st("wkvs"), st("shp"), st("shs"), st("cvp"), st("cvs"))
```

```python
import functools
import math

import numpy as np
import jax
import jax.numpy as jnp
from jax import lax
from jax.experimental import pallas as pl
from jax.experimental.pallas import tpu as pltpu

f32 = jnp.float32
bf16 = jnp.bfloat16
i32 = jnp.int32

RW_HEADS = 16
RW_HEAD = 64
RW_WIDTH = RW_HEADS * RW_HEAD
D_DECAY_LORA = 96
D_AAA_LORA = 96
D_GATE_LORA = 128
GN_EPS = 64e-5
ATT_HEADS = 8
KV_HEADS = 2
HEAD_DIM = 128
ATT_WIDTH = ATT_HEADS * HEAD_DIM
KV_WIDTH = KV_HEADS * HEAD_DIM
IDX_HEADS = 16
IDX_DIM = 64
IDX_W_SCALE = (IDX_HEADS * IDX_DIM) ** -0.5
TOPK_MAX = 256
Q_BLOCK = 128
ROPE_THETA = 10000.0
CONV_W = 3
RMS_EPS = 1e-6

LANES = 128
NEG_BIG = -1e30
INT_MIN = -(2 ** 31)

C_R, C_K, C_V, C_Q, C_GRW, C_GATT, C_QI = 0, 1024, 2048, 3072, 4096, 6144, 8192
C_LORA = 9216
C_KI, C_AK, C_AV, C_WI = 9600, 9728, 9984, 10240
NP_COLS = 10368
LORA_W = 384
RKV_W = 3 * RW_WIDTH

VMEM_LIMIT = 56 * 1024 * 1024
CONVGLU_VMEM_LIMIT = 61 * 1024 * 1024


def _cparams(sem, vmem_limit=VMEM_LIMIT):
    return pltpu.CompilerParams(dimension_semantics=sem, vmem_limit_bytes=vmem_limit)


def _bdot(a, b):
    return jnp.dot(a.astype(bf16), b.astype(bf16), preferred_element_type=f32)


def _bdot_nt(a, b):
    return lax.dot_general(a.astype(bf16), b.astype(bf16), (((1,), (1,)), ((), ())), preferred_element_type=f32)


def _bdot_tn(a, b):
    return lax.dot_general(a.astype(bf16), b.astype(bf16), (((0,), (0,)), ((), ())), preferred_element_type=f32)


def _split3(x):
    x1 = x.astype(bf16)
    r1 = x - x1.astype(f32)
    x2 = r1.astype(bf16)
    x3 = (r1 - x2.astype(f32)).astype(bf16)
    return x1, x2, x3


def _split2(x):
    x1 = x.astype(bf16)
    return x1, (x - x1.astype(f32)).astype(bf16)


def _headsum(x):
    ri = lax.broadcasted_iota(i32, (LANES, LANES), 0) // RW_HEAD
    ci = lax.broadcasted_iota(i32, (LANES, LANES), 1) // RW_HEAD
    bd = (ri == ci).astype(bf16)
    outs = []
    for i in range(x.shape[1] // LANES):
        hi, lo = _split2(x[:, i * LANES:(i + 1) * LANES])
        outs.append(jnp.dot(hi, bd, preferred_element_type=f32) + jnp.dot(lo, bd, preferred_element_type=f32))
    return jnp.concatenate(outs, axis=1)


def _softplus(x):
    return jnp.maximum(x, 0.0) + jnp.log(1.0 + jnp.exp(-jnp.abs(x)))


def _rope(x, cos, sin, half):
    w = x.shape[1]
    reps = w // LANES
    if reps > 1:
        cos = jnp.concatenate([cos] * reps, axis=1)
        sin = jnp.concatenate([sin] * reps, axis=1)
    if 2 * half == LANES and w == LANES:
        partner = pltpu.roll(x, half, axis=1)
    else:
        lane = lax.broadcasted_iota(i32, (1, w), 1)
        first = (lane % (2 * half)) < half
        partner = jnp.where(first, pltpu.roll(x, w - half, axis=1), pltpu.roll(x, half, axis=1))
    return x * cos + partner * sin


def _inproj_body(x_ref, g_ref, w_ref, o_ref, n_ref):
    @pl.when(pl.program_id(1) == 0)
    def _():
        x = x_ref[...]
        ms = jnp.mean(x * x, axis=-1, keepdims=True)
        n_ref[...] = (x * lax.rsqrt(ms + RMS_EPS) * g_ref[...]).astype(bf16)

    o_ref[...] = lax.dot_general(n_ref[...], w_ref[...], (((1,), (1,)), ((), ())), preferred_element_type=f32)


def _in_proj(x2d, gain, w_t):
    m, d = x2d.shape
    tm = 1024 if m % 1024 == 0 else min(512, m)
    tn = NP_COLS // 9
    return pl.pallas_call(
        _inproj_body,
        grid=(m // tm, NP_COLS // tn),
        in_specs=[
            pl.BlockSpec((tm, d), lambda i, j: (i, 0)),
            pl.BlockSpec((1, d), lambda i, j: (0, 0)),
            pl.BlockSpec((tn, d), lambda i, j: (j, 0)),
        ],
        out_specs=pl.BlockSpec((tm, tn), lambda i, j: (i, j)),
        out_shape=jax.ShapeDtypeStruct((m, NP_COLS), f32),
        scratch_shapes=[pltpu.VMEM((tm, d), bf16)],
        compiler_params=_cparams(("parallel", "arbitrary")),
        name="in_proj",
    )(x2d, gain, w_t)


RW_PARAM_NAMES = ("mu_rkv", "mu_lora", "w0", "w2", "a0", "a2", "g2", "k_k", "k_a", "r_k", "lnx_w", "lnx_b")


def _rw_prep(c_rkv, c_lora, p_rkv, p_lora, P):
    m = c_rkv + (p_rkv - c_rkv) * P["mu_rkv"]
    ml = c_lora + (p_lora - c_lora) * P["mu_lora"]
    r, k, v = m[:, :RW_WIDTH], m[:, RW_WIDTH:2 * RW_WIDTH], m[:, 2 * RW_WIDTH:]
    wl, al, gl = ml[:, :LANES], ml[:, LANES:2 * LANES], ml[:, 2 * LANES:]
    w_log = -_softplus(-(P["w0"] + _bdot(jnp.tanh(wl), P["w2"]))) - 0.5
    logw = -jnp.exp(w_log)
    asig = jax.nn.sigmoid(P["a0"] + _bdot(al, P["a2"]))
    g = _bdot(jax.nn.sigmoid(gl), P["g2"])
    kk = k * P["k_k"]
    kkn = kk * lax.rsqrt(jnp.maximum(_headsum(kk * kk), 1e-24))
    k2 = k * (1.0 + (asig - 1.0) * P["k_a"])
    return r, logw, k2, v, -kkn, kkn * asig, g


def _rw_post(y, r, k2, v, g, P):
    inv_n = 1.0 / RW_HEAD
    mean = _headsum(y) * inv_n
    d = y - mean
    var = _headsum(d * d) * inv_n
    yn = d * lax.rsqrt(var + GN_EPS) * P["lnx_w"] + P["lnx_b"]
    bonus = _headsum(r * k2 * P["r_k"]) * v
    return (yn + bonus) * g


def _chunk_pre(at, rt, bt, kt, v, masks, nsteps):
    keep, eye2, colh, mA, _ = masks
    R = range(len(at))
    C = at[0].shape[0]
    X = [jnp.concatenate([at[p], rt[p]], axis=0) for p in R]
    scA = [jnp.where(keep, _bdot_nt(jnp.where(mA, X[p], 0.0), jnp.concatenate([bt[p], kt[p]], axis=0)), 0.0)
           for p in R]
    scB = [jnp.where(keep, _bdot_nt(jnp.where(mA, 0.0, X[p]), jnp.concatenate([kt[p], bt[p]], axis=0)), 0.0)
           for p in R]

    def bdiag(x):
        return jnp.concatenate([jnp.where(colh, x, 0.0), jnp.where(colh, 0.0, x)], axis=0)

    L = [jnp.where(colh, scA[p][:C], scB[p][:C]) for p in R]
    T = [eye2 + L[p] for p in R]
    if nsteps > 0:
        Pw = [_bdot(L[p], bdiag(L[p])) for p in R]
        for i in range(nsteps):
            if i < nsteps - 1:
                Z = [_bdot(Pw[p], jnp.concatenate([bdiag(T[p]), bdiag(Pw[p])], axis=1)) for p in R]
                T = [T[p] + Z[p][:, :2 * C] for p in R]
                Pw = [Z[p][:, 2 * C:] for p in R]
            else:
                T = [T[p] + _bdot(Pw[p], bdiag(T[p])) for p in R]
    lkv = [_bdot(jnp.where(colh, scB[p][:C], scA[p][:C]),
                 jnp.concatenate([jnp.where(mA, 0.0, v[p]), jnp.where(mA, v[p], 0.0)], axis=0)) for p in R]
    readout = [jnp.concatenate([scA[p][C:], scB[p][C:]], axis=1) for p in R]
    return T, lkv, readout


def _chunk_seq(at, rt, be, ke, v, wc, T, lkv, readout, S, masks):
    _, _, _, mA, bdmask = masks
    R = range(len(at))
    C = at[0].shape[0]
    XS = [_bdot_nt(jnp.concatenate([at[p], rt[p]], axis=0), S[p]) for p in R]
    G = [XS[p][:C] + lkv[p] for p in R]
    U = [_bdot(T[p], jnp.concatenate([jnp.where(mA, G[p], 0.0), jnp.where(mA, 0.0, G[p])], axis=0)) for p in R]
    Y = [XS[p][C:] + _bdot(readout[p], jnp.concatenate([jnp.where(mA, U[p], 0.0), jnp.where(mA, v[p], 0.0),
                                                        jnp.where(mA, 0.0, v[p]), jnp.where(mA, 0.0, U[p])], axis=0))
         for p in R]
    dS = [_bdot_tn(jnp.concatenate([U[p], v[p]], axis=0), jnp.concatenate([be[p], ke[p]], axis=0)) for p in R]
    S_new = [S[p] * wc[p] + jnp.where(bdmask, dS[p], 0.0) for p in R]
    return Y, S_new


def _chunk_masks(C):
    row = lax.broadcasted_iota(i32, (2 * C, 2 * C), 0)
    col = lax.broadcasted_iota(i32, (2 * C, 2 * C), 1)
    t = jnp.where(row >= C, row - C, row)
    s = jnp.where(col >= C, col - C, col)
    keep = (s < t) | ((row >= C) & (s == t))
    hrow = lax.broadcasted_iota(i32, (C, 2 * C), 0)
    hcol = lax.broadcasted_iota(i32, (C, 2 * C), 1)
    eye2 = (hrow == jnp.where(hcol >= C, hcol - C, hcol)).astype(f32)
    colh = hcol < C
    mA = lax.broadcasted_iota(i32, (1, LANES), 1) < RW_HEAD
    r2 = lax.broadcasted_iota(i32, (LANES, LANES), 0) // RW_HEAD
    c2 = lax.broadcasted_iota(i32, (LANES, LANES), 1) // RW_HEAD
    return keep, eye2, colh, mA, r2 == c2


def _rwkv_prompt_body(chunk, rkv_ref, lora_ref, sp_rkv_ref, sp_lora_ref, s0_ref, *rest):
    np_ = len(RW_PARAM_NAMES)
    P = {n: rest[i][...] for i, n in enumerate(RW_PARAM_NAMES)}
    o_ref, sout_ref = rest[np_], rest[np_ + 1]
    (S_ref, car_rkv, car_lora, at_s, rt_s, bt_s, kt_s, be_s, ke_s, v_s, cum_s, y_s, t_s, lkv_s, ro_s) = rest[np_ + 2:]
    t = pl.program_id(1)
    tt = rkv_ref.shape[0]
    C = chunk

    @pl.when(t == 0)
    def _():
        S_ref[...] = s0_ref[0]
        car_rkv[...] = sp_rkv_ref[0]
        car_lora[...] = sp_lora_ref[0]

    c_rkv = rkv_ref[...]
    c_lora = lora_ref[...]
    first = lax.broadcasted_iota(i32, (tt, 1), 0) == 0
    p_rkv = jnp.where(first, car_rkv[...], pltpu.roll(c_rkv, 1, axis=0))
    p_lora = jnp.where(first, car_lora[...], pltpu.roll(c_lora, 1, axis=0))
    car_rkv[...] = c_rkv[tt - 1:tt, :]
    car_lora[...] = c_lora[tt - 1:tt, :]
    r, logw, k2, v, a, b, g = _rw_prep(c_rkv, c_lora, p_rkv, p_lora, P)

    ri = lax.broadcasted_iota(i32, (tt, tt), 0)
    ci = lax.broadcasted_iota(i32, (tt, tt), 1)
    same = ri // C == ci // C
    tril = (same & (ci <= ri)).astype(bf16)
    triu = (same & (ci > ri)).astype(bf16)
    pieces = _split3(logw)
    cum = sum(jnp.dot(tril, piece, preferred_element_type=f32) for piece in pieces)
    rev = sum(jnp.dot(triu, piece, preferred_element_type=f32) for piece in pieces)
    iw = jnp.exp(-cum)
    ew = jnp.exp(rev)
    at_s[...] = a * jnp.exp(cum - logw)
    rt_s[...] = r * jnp.exp(cum)
    bt_s[...] = b * iw
    kt_s[...] = k2 * iw
    be_s[...] = b * ew
    ke_s[...] = k2 * ew
    v_s[...] = v
    cum_s[...] = cum
    masks = _chunk_masks(C)
    nsteps = max(int(math.ceil(math.log2(C))) - 1, 0)
    pairs = range(RW_WIDTH // LANES)

    lanes = [slice(p * LANES, (p + 1) * LANES) for p in pairs]
    wide = [slice(p * 4 * C, (p + 1) * 4 * C) for p in pairs]

    def ld(ref, r0, cols=lanes):
        return [ref[pl.ds(r0, C), cols[p]] for p in pairs]

    npre = next(u for u in (4, 2, 1) if (tt // C) % u == 0)

    def pre_body(gi, carry):
        r0s = [pl.multiple_of((gi * npre + u) * C, C) for u in range(npre)]
        cat = lambda ref: sum((ld(ref, r0) for r0 in r0s), [])
        T, lkv, readout = _chunk_pre(cat(at_s), cat(rt_s), cat(bt_s), cat(kt_s), cat(v_s), masks, nsteps)
        for u, r0 in enumerate(r0s):
            for p in pairs:
                q = u * len(pairs) + p
                t_s[pl.ds(r0, C), lanes[p]] = T[q]
                lkv_s[pl.ds(r0, C), lanes[p]] = lkv[q]
                ro_s[pl.ds(r0, C), wide[p]] = readout[q]
        return carry

    def seq_body(ci_, carry):
        r0 = pl.multiple_of(ci_ * C, C)
        last8 = pl.multiple_of(r0 + C - 8, 8)
        wc = [jnp.exp(cum_s[pl.ds(last8, 8), lanes[p]][7:8]) for p in pairs]
        Y, S_new = _chunk_seq(ld(at_s, r0), ld(rt_s, r0), ld(be_s, r0), ld(ke_s, r0), ld(v_s, r0), wc, ld(t_s, r0),
                              ld(lkv_s, r0), ld(ro_s, r0, wide), [S_ref[p] for p in pairs], masks)
        for p in pairs:
            y_s[pl.ds(r0, C), lanes[p]] = Y[p]
            S_ref[p] = S_new[p]
        return carry

    lax.fori_loop(0, tt // C // npre, pre_body, 0)
    lax.fori_loop(0, tt // C, seq_body, 0)
    o_ref[...] = _rw_post(y_s[...], r, k2, v, g, P)

    @pl.when(t == pl.num_programs(1) - 1)
    def _():
        sout_ref[0] = S_ref[...]


def _const_spec(shape):
    nd = len(shape)
    return pl.BlockSpec(shape, lambda *_: (0,) * nd)


def _rwkv_prompt(c, sp_rkv, sp_lora, s0, params, nb, seq):
    tt = min(256, seq)
    chunk = min(64, tt)
    nt = seq // tt
    npairs = RW_WIDTH // LANES
    in_specs = [
        pl.BlockSpec((tt, RKV_W), lambda b, t: (b * nt + t, 0)),
        pl.BlockSpec((tt, LORA_W), lambda b, t: (b * nt + t, C_LORA // LORA_W)),
        pl.BlockSpec((1, 1, RKV_W), lambda b, t: (b, 0, 0)),
        pl.BlockSpec((1, 1, LORA_W), lambda b, t: (b, 0, 0)),
        pl.BlockSpec((1, npairs, LANES, LANES), lambda b, t: (b, 0, 0, 0)),
    ] + [_const_spec(params[n].shape) for n in RW_PARAM_NAMES]
    out_specs = [
        pl.BlockSpec((tt, RW_WIDTH), lambda b, t: (b * nt + t, 0)),
        pl.BlockSpec((1, npairs, LANES, LANES), lambda b, t: (b, 0, 0, 0)),
    ]
    scratch = [pltpu.VMEM((npairs, LANES, LANES), f32), pltpu.VMEM((1, RKV_W), f32), pltpu.VMEM((1, LORA_W), f32)]
    assert 2 * chunk == LANES, "a head pair's chunk matrices fill one 128-lane tile"
    scratch += [pltpu.VMEM((tt, RW_WIDTH), f32) for _ in range(11)] + [pltpu.VMEM((tt, 2 * RW_WIDTH), f32)]
    return pl.pallas_call(
        functools.partial(_rwkv_prompt_body, chunk),
        grid=(nb, nt),
        in_specs=in_specs,
        out_specs=out_specs,
        out_shape=[jax.ShapeDtypeStruct((nb * seq, RW_WIDTH), f32),
                   jax.ShapeDtypeStruct((nb, npairs, LANES, LANES), f32)],
        scratch_shapes=scratch,
        compiler_params=_cparams(("parallel", "arbitrary")),
        name="rwkv_prompt",
    )(c, c, sp_rkv, sp_lora, s0, *[params[n] for n in RW_PARAM_NAMES])


def _rwkv_prep_body(rkv_ref, lora_ref, prkv_ref, plora_ref, *rest):
    np_ = len(RW_PARAM_NAMES)
    P = {n: rest[i][...] for i, n in enumerate(RW_PARAM_NAMES)}
    outs = rest[np_:]
    vals = _rw_prep(rkv_ref[...], lora_ref[...], prkv_ref[...], plora_ref[...], P)
    for o, v in zip(outs, vals):
        o[...] = v


def _rwkv_prep(c, prev_rkv, prev_lora, params):
    m = c.shape[0]
    tm = min(512, m)
    in_specs = [
        pl.BlockSpec((tm, RKV_W), lambda i: (i, 0)),
        pl.BlockSpec((tm, LORA_W), lambda i: (i, C_LORA // LORA_W)),
        pl.BlockSpec((tm, RKV_W), lambda i: (i, 0)),
        pl.BlockSpec((tm, LORA_W), lambda i: (i, 0)),
    ] + [_const_spec(params[n].shape) for n in RW_PARAM_NAMES]
    return pl.pallas_call(
        _rwkv_prep_body,
        grid=(m // tm,),
        in_specs=in_specs,
        out_specs=[pl.BlockSpec((tm, RW_WIDTH), lambda i: (i, 0)) for _ in range(7)],
        out_shape=[jax.ShapeDtypeStruct((m, RW_WIDTH), f32) for _ in range(7)],
        compiler_params=_cparams(("parallel",)),
        name="rwkv_prep",
    )(c, c, prev_rkv, prev_lora, *[params[n] for n in RW_PARAM_NAMES])


def _rwkv_seq_body(r_ref, lw_ref, k_ref, v_ref, a_ref, b_ref, s_ref, y_ref, sout_ref):
    steps = r_ref.shape[0]
    w = [jnp.exp(lw_ref[t]) for t in range(steps)]
    for i in range(RW_HEAD):
        Si = s_ref[0, i]
        for t in range(steps):
            sa = jnp.sum(Si * a_ref[t], axis=0, keepdims=True)
            Si = Si * w[t] + sa * b_ref[t] + v_ref[t, i:i + 1, :] * k_ref[t]
            y_ref[t, i:i + 1, :] = jnp.sum(Si * r_ref[t], axis=0, keepdims=True)
        sout_ref[0, i] = Si


def _rwkv_seq(ops, s0, nreq, steps):
    ops_t = [o.reshape(nreq, steps, RW_WIDTH).transpose(1, 2, 0) for o in ops]
    vec = pl.BlockSpec((steps, RW_HEAD, nreq), lambda h: (0, h, 0))
    st = pl.BlockSpec((1, RW_HEAD, RW_HEAD, nreq), lambda h: (h, 0, 0, 0))
    y_t, s_t = pl.pallas_call(
        _rwkv_seq_body,
        grid=(RW_HEADS,),
        in_specs=[vec] * 6 + [st],
        out_specs=[vec, st],
        out_shape=[jax.ShapeDtypeStruct((steps, RW_WIDTH, nreq), f32),
                   jax.ShapeDtypeStruct((RW_HEADS, RW_HEAD, RW_HEAD, nreq), f32)],
        compiler_params=_cparams(("parallel",)),
        name="rwkv_seq",
    )(*ops_t, s0.transpose(1, 2, 3, 0))
    return y_t.transpose(2, 0, 1).reshape(nreq * steps, RW_WIDTH), s_t.transpose(3, 0, 1, 2)


def _rwkv_post_body(y_ref, r_ref, k_ref, v_ref, g_ref, *rest):
    np_ = len(RW_PARAM_NAMES)
    P = {n: rest[i][...] for i, n in enumerate(RW_PARAM_NAMES)}
    rest[np_][...] = _rw_post(y_ref[...], r_ref[...], k_ref[...], v_ref[...], g_ref[...], P)


def _rwkv_post(y, r, k2, v, g, params):
    m = y.shape[0]
    tm = min(512, m)
    return pl.pallas_call(
        _rwkv_post_body,
        grid=(m // tm,),
        in_specs=[pl.BlockSpec((tm, RW_WIDTH), lambda i: (i, 0)) for _ in range(5)]
        + [_const_spec(params[n].shape) for n in RW_PARAM_NAMES],
        out_specs=pl.BlockSpec((tm, RW_WIDTH), lambda i: (i, 0)),
        out_shape=jax.ShapeDtypeStruct((m, RW_WIDTH), f32),
        compiler_params=_cparams(("parallel",)),
        name="rwkv_post",
    )(y, r, k2, v, g, *[params[n] for n in RW_PARAM_NAMES])


def _sort_key(score):
    bits = pltpu.bitcast(score, i32)
    key = jnp.where(bits < 0, bits ^ jnp.int32(0x7FFFFFFF), bits)
    return jnp.where(score == 0.0, jnp.int32(0), key)


def _kth_largest(keys, k):
    def body(i, ts):
        bit = lax.shift_left(jnp.int32(1), jnp.int32(31) - i)
        cands = [t + bit for t in ts]
        cnts = [jnp.sum((key >= c).astype(f32), axis=1, keepdims=True) for key, c in zip(keys, cands)]
        return tuple(jnp.where(n >= k, c, t) for n, c, t in zip(cnts, cands, ts))

    init = tuple(jnp.full((key.shape[0], 1), INT_MIN, i32) for key in keys)
    return lax.fori_loop(0, 32, body, init, unroll=4)


def _row_groups(x, n):
    step = x.shape[0] // n
    return [x[g * step:(g + 1) * step] for g in range(n)]


def _dsa_prompt_body(topk, q_ref, qi_ref, wi_ref, k_ref, v_ref, ki_ref, cosk_ref, sink_ref, cosi_ref, sini_ref,
                     o_ref, kout_ref, vout_ref, kiout_ref, kb_ref, vb_ref, kib_ref, bias_ref):
    qb = pl.program_id(1)
    seq = k_ref.shape[0]
    nq = q_ref.shape[0]

    @pl.when(qb == 0)
    def _():
        cos, sin = cosk_ref[...], sink_ref[...]
        v = v_ref[...]
        kr = jnp.concatenate([_rope(k_ref[:, h * HEAD_DIM:(h + 1) * HEAD_DIM], cos, sin, HEAD_DIM // 2)
                              for h in range(KV_HEADS)], axis=1)
        for h in range(KV_HEADS):
            kout_ref[pl.ds(h, seq, stride=KV_HEADS), :] = kr[:, h * HEAD_DIM:(h + 1) * HEAD_DIM]
            vout_ref[pl.ds(h, seq, stride=KV_HEADS), :] = v[:, h * HEAD_DIM:(h + 1) * HEAD_DIM]
        kb_ref[...] = kr.astype(bf16)
        vb_ref[...] = v.astype(bf16)
        kir = _rope(ki_ref[...], cosi_ref[...], sini_ref[...], IDX_DIM // 2)
        kiout_ref[...] = kir[:, :IDX_DIM]
        kib_ref[...] = (kir + pltpu.roll(kir, IDX_DIM, axis=1)).astype(bf16)

    r0 = pl.multiple_of(qb * nq, nq)
    cosq, sinq = cosk_ref[pl.ds(r0, nq), :], sink_ref[pl.ds(r0, nq), :]
    cosqi, sinqi = cosi_ref[pl.ds(r0, nq), :], sini_ref[pl.ds(r0, nq), :]
    q = _rope(q_ref[...], cosq, sinq, HEAD_DIM // 2).astype(bf16)
    qi = _rope(qi_ref[...], cosqi, sinqi, IDX_DIM // 2)
    wi = wi_ref[...] * IDX_W_SCALE
    lane = lax.broadcasted_iota(i32, (1, LANES), 1)
    scale = HEAD_DIM ** -0.5 * math.log2(math.e)
    hpg = ATT_HEADS // KV_HEADS

    def process(ext):
        kib = kib_ref[:ext, :]
        sc = None
        for h in range(IDX_HEADS):
            pair = qi[:, (h // 2) * LANES:(h // 2 + 1) * LANES]
            mine = (lane < IDX_DIM) if h % 2 == 0 else (lane >= IDX_DIM)
            term = jnp.maximum(_bdot_nt(jnp.where(mine, pair, 0.0), kib), 0.0) * wi[:, h:h + 1]
            sc = term if sc is None else sc + term

        qpos = r0 + lax.broadcasted_iota(i32, (nq, 1), 0)
        kpos = lax.broadcasted_iota(i32, (1, ext), 1)
        causal = kpos <= qpos
        key = jnp.where(causal, _sort_key(sc), INT_MIN)
        thr = jnp.concatenate(_kth_largest(_row_groups(key, 4), float(topk)), axis=0)
        bias_ref[:, :ext] = jnp.where(causal & (key >= thr), 0.0, NEG_BIG)

        def qk(h):
            kg = kb_ref[:ext, (h // hpg) * HEAD_DIM:(h // hpg + 1) * HEAD_DIM]
            return _bdot_nt(q[:, h * HEAD_DIM:(h + 1) * HEAD_DIM], kg)

        s_next = qk(0)
        for h in range(ATT_HEADS):
            s = s_next * scale + bias_ref[:, :ext]
            if h + 1 < ATT_HEADS:
                s_next = qk(h + 1)
            m = jnp.max(s, axis=1, keepdims=True)
            p = jnp.exp2(s - m)
            l = jnp.sum(p, axis=1, keepdims=True)
            vg = vb_ref[:ext, (h // hpg) * HEAD_DIM:(h // hpg + 1) * HEAD_DIM]
            o = jnp.dot(p.astype(bf16), vg, preferred_element_type=f32)
            o_ref[:, h * HEAD_DIM:(h + 1) * HEAD_DIM] = o / l

    nvar = next(v for v in (8, 4, 2, 1) if (seq // nq) % v == 0)
    per = seq // nq // nvar
    for var in range(nvar):
        pl.when(qb // per == var)(functools.partial(process, (var + 1) * per * nq))


def _dsa_prompt(c, tabs, nb, seq):
    nq = Q_BLOCK
    nblk = seq // nq
    topk = min(TOPK_MAX, seq // 4)
    cosk, sink, cosi, sini = tabs
    row = lambda b, j: b * nblk + j
    in_specs = [
        pl.BlockSpec((nq, ATT_WIDTH), lambda b, j: (row(b, j), C_Q // ATT_WIDTH)),
        pl.BlockSpec((nq, IDX_HEADS * IDX_DIM), lambda b, j: (row(b, j), C_QI // (IDX_HEADS * IDX_DIM))),
        pl.BlockSpec((nq, LANES), lambda b, j: (row(b, j), C_WI // LANES)),
        pl.BlockSpec((seq, KV_WIDTH), lambda b, j: (b, C_AK // KV_WIDTH)),
        pl.BlockSpec((seq, KV_WIDTH), lambda b, j: (b, C_AV // KV_WIDTH)),
        pl.BlockSpec((seq, LANES), lambda b, j: (b, C_KI // LANES)),
    ] + [_const_spec((seq, LANES)) for _ in range(4)]
    out_specs = [
        pl.BlockSpec((nq, ATT_WIDTH), lambda b, j: (row(b, j), 0)),
        pl.BlockSpec((seq * KV_HEADS, HEAD_DIM), lambda b, j: (b, 0)),
        pl.BlockSpec((seq * KV_HEADS, HEAD_DIM), lambda b, j: (b, 0)),
        pl.BlockSpec((seq, IDX_DIM), lambda b, j: (b, 0)),
    ]
    scratch = [pltpu.VMEM((seq, KV_WIDTH), bf16), pltpu.VMEM((seq, KV_WIDTH), bf16), pltpu.VMEM((seq, LANES), bf16),
               pltpu.VMEM((nq, seq), f32)]
    return pl.pallas_call(
        functools.partial(_dsa_prompt_body, topk),
        grid=(nb, nblk),
        in_specs=in_specs,
        out_specs=out_specs,
        out_shape=[jax.ShapeDtypeStruct((nb * seq, ATT_WIDTH), f32),
                   jax.ShapeDtypeStruct((nb * seq * KV_HEADS, HEAD_DIM), f32),
                   jax.ShapeDtypeStruct((nb * seq * KV_HEADS, HEAD_DIM), f32),
                   jax.ShapeDtypeStruct((nb * seq, IDX_DIM), f32)],
        scratch_shapes=scratch,
        compiler_params=_cparams(("parallel", "arbitrary")),
        name="dsa_prompt",
    )(c, c, c, c, c, c, cosk, sink, cosi, sini)


ROWS = 8


def _page_copies(pt_ref, req, n_pages, srcs_dsts_sems):
    out = []
    for p in range(n_pages):
        pg = pt_ref[req, p]
        for hbm, dst, sem in srcs_dsts_sems:
            out.append(pltpu.make_async_copy(hbm.at[pg], dst(p), sem))
    return out


def _dsa_select_body(topk, n_new, group, pt_ref, qi_ref, wrow_ref, kin_ref, cosi_ref, sini_ref, cki_hbm,
                     sel_ref, kiout_ref, kibuf, sems):
    i = pl.program_id(0)
    nsteps = pl.num_programs(0)
    n_pages = pt_ref.shape[1]
    page = cki_hbm.shape[2]
    past = n_pages * page
    G = group

    def copies(step, slot):
        out = []
        for j in range(G):
            out += _page_copies(pt_ref, step * G + j, n_pages,
                                [(cki_hbm, lambda p, j=j: kibuf.at[slot, j, :, pl.ds(p * page, page)],
                                  sems.at[slot])])
        return out

    slot = i % 2

    @pl.when(i == 0)
    def _():
        for cp in copies(0, 0):
            cp.start()

    @pl.when(i + 1 < nsteps)
    def _():
        for cp in copies(i + 1, 1 - slot):
            cp.start()

    cosi, sini = cosi_ref[...], sini_ref[...]
    li = lax.broadcasted_iota(i32, (LANES, LANES), 0)
    lo = lax.broadcasted_iota(i32, (LANES, LANES), 1)
    eye = (li == lo).astype(f32)
    qis, kins, wcols = [], [], []
    for j in range(G):
        qi = _rope(qi_ref[j], cosi, sini, IDX_DIM // 2)
        kin = _rope(kin_ref[j], cosi, sini, IDX_DIM // 2)[:, :IDX_DIM]
        kiout_ref[j] = kin
        kins.append(jnp.concatenate([kin, jnp.zeros((LANES - ROWS, IDX_DIM), f32)], axis=0))
        qis.append(jnp.concatenate([qi[:, h * IDX_DIM:(h + 1) * IDX_DIM] for h in range(IDX_HEADS)], axis=0))
        wrow = wrow_ref[0, :, j * LANES:(j + 1) * LANES] * IDX_W_SCALE
        wcols.append(jnp.sum(eye * wrow, axis=1, keepdims=True))

    for cp in copies(i, slot):
        cp.wait()

    rows = []
    for j in range(G):
        d = jnp.concatenate([_bdot(qis[j], kibuf[slot, j]), _bdot_nt(qis[j], kins[j])], axis=1)
        term = jnp.maximum(d, 0.0) * wcols[j]
        rows.append(sum(term[h * ROWS:(h + 1) * ROWS] for h in range(IDX_HEADS)))
    sc = jnp.concatenate(rows, axis=0)
    nrow = ROWS * G
    width = past + LANES
    t_row = lax.broadcasted_iota(i32, (nrow, width), 0) % ROWS
    col = lax.broadcasted_iota(i32, (nrow, width), 1)
    valid = (col < past) | ((col - past <= t_row) & (col - past < n_new))
    key = jnp.where(valid, _sort_key(sc), INT_MIN)
    thr = jnp.concatenate(_kth_largest(_row_groups(key, G), float(topk)), axis=0)
    chosen = (valid & (key >= thr)).astype(f32)
    if nrow < LANES:
        chosen = jnp.concatenate([chosen, jnp.zeros((LANES - nrow, width), f32)], axis=0)
    sel_ref[0] = chosen.T


def _dsa_sample_body(n_new, group, pt_ref, q_ref, kn_ref, vn_ref, sel_ref, cosk_ref, sink_ref, ck_hbm, cv_hbm,
                     o_ref, kout_ref, kbuf, vbuf, sems):
    b = pl.program_id(0)
    nreq = pl.num_programs(0)
    n_pages = pt_ref.shape[1]
    prow = ck_hbm.shape[1]
    past = n_pages * prow // KV_HEADS
    G = group

    def copies(req, slot):
        return _page_copies(pt_ref, req, n_pages,
                            [(ck_hbm, lambda p: kbuf.at[slot, pl.ds(p * prow, prow)], sems.at[0, slot]),
                             (cv_hbm, lambda p: vbuf.at[slot, pl.ds(p * prow, prow)], sems.at[1, slot])])

    slot = b % 2

    @pl.when(b == 0)
    def _():
        for cp in copies(0, 0):
            cp.start()

    @pl.when(b + 1 < nreq)
    def _():
        for cp in copies(b + 1, 1 - slot):
            cp.start()

    cosk, sink = cosk_ref[...], sink_ref[...]
    q = _rope(q_ref[0], cosk, sink, HEAD_DIM // 2)
    kn = jnp.concatenate([_rope(kn_ref[0][:, h * HEAD_DIM:(h + 1) * HEAD_DIM], cosk, sink, HEAD_DIM // 2)
                          for h in range(KV_HEADS)], axis=1)
    vn = vn_ref[0]
    kout_ref[0] = kn

    li = lax.broadcasted_iota(i32, (LANES, LANES), 0)
    lo = lax.broadcasted_iota(i32, (LANES, LANES), 1)
    route = ((b % G) * ROWS + lo % ROWS == li) & (lo % ROWS < n_new)
    chosen = jnp.dot(sel_ref[0].astype(bf16), route.astype(bf16), preferred_element_type=f32)
    bias = (chosen - 1.0) * (-NEG_BIG)
    bias_p, bias_n = bias[:past], bias[past:past + ROWS]

    eye = (li == lo).astype(f32)
    scale = HEAD_DIM ** -0.5 * math.log2(math.e)
    hpg = ATT_HEADS // KV_HEADS
    zero_rows = lambda n: [jnp.zeros((n, HEAD_DIM), f32)] if n else []

    for cp in copies(b, slot):
        cp.wait()

    s_p = s_n = None
    for g in range(KV_HEADS):
        heads = [q[:, (g * hpg + r) * HEAD_DIM:(g * hpg + r + 1) * HEAD_DIM] for r in range(hpg)]
        qg = jnp.concatenate(zero_rows(g * hpg * ROWS) + heads + zero_rows(LANES - (g + 1) * hpg * ROWS), axis=0)
        kp = kbuf[slot, pl.ds(g, past, stride=KV_HEADS), :]
        part_p = _bdot_nt(kp, qg)
        part_n = _bdot_nt(kn[:, g * HEAD_DIM:(g + 1) * HEAD_DIM], qg)
        s_p = part_p if s_p is None else s_p + part_p
        s_n = part_n if s_n is None else s_n + part_n
    s_p = s_p * scale + bias_p
    s_n = s_n * scale + bias_n
    m = jnp.maximum(jnp.max(s_p, axis=0, keepdims=True), jnp.max(s_n, axis=0, keepdims=True))
    p_p = jnp.exp2(s_p - m)
    p_n = jnp.exp2(s_n - m)
    l = jnp.sum(p_p, axis=0, keepdims=True) + jnp.sum(p_n, axis=0, keepdims=True)
    l_col = jnp.sum(eye * l, axis=1, keepdims=True)
    for g in range(KV_HEADS):
        vp = vbuf[slot, pl.ds(g, past, stride=KV_HEADS), :]
        o = (_bdot_tn(p_p, vp) + _bdot_tn(p_n, vn[:, g * HEAD_DIM:(g + 1) * HEAD_DIM])) / l_col
        for r in range(hpg):
            h = g * hpg + r
            o_ref[0, :, h * HEAD_DIM:(h + 1) * HEAD_DIM] = o[h * ROWS:(h + 1) * ROWS]


def _dsa_sample(csel, wrow, tabs, page_table, cache_k, cache_v, cache_kidx, n_new):
    q8, qi8, kn8, vn8, kin8 = csel
    nreq = q8.shape[0]
    n_pages = page_table.shape[1]
    page = cache_kidx.shape[2]
    past = n_pages * page
    topk = min(TOPK_MAX, (past + n_new) // 4)
    group = min(LANES // ROWS, nreq)
    cosk, sink, cosi, sini = tabs
    anyspec = pl.BlockSpec(memory_space=pl.ANY)
    tab = pl.BlockSpec((ROWS, LANES), lambda b, pt: (0, 0))

    grp3 = lambda w: pl.BlockSpec((group, ROWS, w), lambda i, pt: (i, 0, 0))
    sel, ki_new = pl.pallas_call(
        functools.partial(_dsa_select_body, topk, n_new, group),
        grid_spec=pltpu.PrefetchScalarGridSpec(
            num_scalar_prefetch=1,
            grid=(nreq // group,),
            in_specs=[grp3(IDX_HEADS * IDX_DIM), pl.BlockSpec((1, 1, group * LANES), lambda i, pt: (i, 0, 0)),
                      grp3(LANES), tab, tab, anyspec],
            out_specs=[pl.BlockSpec((1, past + LANES, LANES), lambda i, pt: (i, 0, 0)), grp3(IDX_DIM)],
            scratch_shapes=[pltpu.VMEM((2, group, IDX_DIM, past), f32), pltpu.SemaphoreType.DMA((2,))],
        ),
        out_shape=[jax.ShapeDtypeStruct((nreq // group, past + LANES, LANES), f32),
                   jax.ShapeDtypeStruct((nreq, ROWS, IDX_DIM), f32)],
        compiler_params=_cparams(("arbitrary",)),
        name="dsa_select",
    )(page_table, qi8, wrow.reshape(nreq // group, 1, group * LANES), kin8, cosi, sini, cache_kidx)

    req3 = lambda w: pl.BlockSpec((1, ROWS, w), lambda b, pt: (b, 0, 0))
    prow = cache_k.shape[1]
    o, k_new = pl.pallas_call(
        functools.partial(_dsa_sample_body, n_new, group),
        grid_spec=pltpu.PrefetchScalarGridSpec(
            num_scalar_prefetch=1,
            grid=(nreq,),
            in_specs=[req3(ATT_WIDTH), req3(KV_WIDTH), req3(KV_WIDTH),
                      pl.BlockSpec((1, past + LANES, LANES), lambda b, pt: (b // group, 0, 0)), tab, tab,
                      anyspec, anyspec],
            out_specs=[req3(ATT_WIDTH), req3(KV_WIDTH)],
            scratch_shapes=[pltpu.VMEM((2, n_pages * prow, HEAD_DIM), f32),
                            pltpu.VMEM((2, n_pages * prow, HEAD_DIM), f32), pltpu.SemaphoreType.DMA((2, 2))],
        ),
        out_shape=[jax.ShapeDtypeStruct((nreq, ROWS, ATT_WIDTH), f32),
                   jax.ShapeDtypeStruct((nreq, ROWS, KV_WIDTH), f32)],
        compiler_params=_cparams(("arbitrary",)),
        name="dsa_sample",
    )(page_table, q8, kn8, vn8, sel, cosk, sink, cache_k, cache_v)
    return o, k_new, ki_new


def _merge_body(orw_ref, oatt_ref, grw_ref, gatt_ref, prw_ref, patt_ref, o_ref):
    a = jnp.dot(orw_ref[...].astype(bf16), prw_ref[...], preferred_element_type=f32)
    b = jnp.dot(oatt_ref[...].astype(bf16), patt_ref[...], preferred_element_type=f32)
    o_ref[...] = (jax.nn.sigmoid(grw_ref[...]) * a + jax.nn.sigmoid(gatt_ref[...]) * b).astype(bf16)


def _merge(o_rw, o_att, c, p_rw, p_att):
    m = o_rw.shape[0]
    d = p_rw.shape[1]
    tm = 1024 if m % 1024 == 0 else min(512, m)
    tn = 1024
    nj = d // tn
    return pl.pallas_call(
        _merge_body,
        grid=(m // tm, nj),
        in_specs=[
            pl.BlockSpec((tm, RW_WIDTH), lambda i, j: (i, 0)),
            pl.BlockSpec((tm, ATT_WIDTH), lambda i, j: (i, 0)),
            pl.BlockSpec((tm, tn), lambda i, j: (i, C_GRW // tn + j)),
            pl.BlockSpec((tm, tn), lambda i, j: (i, C_GATT // tn + j)),
            pl.BlockSpec((RW_WIDTH, tn), lambda i, j: (0, j)),
            pl.BlockSpec((ATT_WIDTH, tn), lambda i, j: (0, j)),
        ],
        out_specs=pl.BlockSpec((tm, tn), lambda i, j: (i, j)),
        out_shape=jax.ShapeDtypeStruct((m, d), bf16),
        compiler_params=_cparams(("parallel", "arbitrary")),
        name="merge",
    )(o_rw, o_att, c, c, p_rw, p_att)


def _outproj_body(mg_ref, x_ref, w_ref, o_ref):
    o_ref[...] = x_ref[...] + jnp.dot(mg_ref[...], w_ref[...], preferred_element_type=f32)


def _outproj(merged, x2d, w_o):
    m, d = x2d.shape
    tm = 1024 if m % 1024 == 0 else min(512, m)
    tn = 1024
    return pl.pallas_call(
        _outproj_body,
        grid=(m // tm, d // tn),
        in_specs=[
            pl.BlockSpec((tm, d), lambda i, j: (i, 0)),
            pl.BlockSpec((tm, tn), lambda i, j: (i, j)),
            pl.BlockSpec((d, tn), lambda i, j: (0, j)),
        ],
        out_specs=pl.BlockSpec((tm, tn), lambda i, j: (i, j)),
        out_shape=jax.ShapeDtypeStruct((m, d), f32),
        compiler_params=_cparams(("parallel", "arbitrary")),
        name="out_proj",
    )(merged, x2d, w_o)


def _convglu_body(shift, tiles_per_seq, final_norm, h_ref, g2_ref, gf_ref, cprev_ref, wg_ref, wu_ref, cw_ref,
                  cb_ref, wd_ref, o_ref, tail_ref, n_ref, ext_ref, carry_ref, act_ref):
    i = pl.program_id(0)
    j = pl.program_id(1)
    nj = pl.num_programs(1) - 1
    tm = h_ref.shape[0]
    hist = 2 * shift
    base = ext_ref.shape[0] - tm

    def up_stage():
        n = n_ref[...]
        gate = jnp.dot(n, wg_ref[...], preferred_element_type=f32)
        up = jnp.dot(n, wu_ref[...], preferred_element_type=f32)
        ext_ref[base - hist:base, :] = jnp.where(i % tiles_per_seq == 0, cprev_ref[0], carry_ref[j])
        ext_ref[base:, :] = gate
        cw = cw_ref[...]
        c = (cb_ref[...] + ext_ref[base - hist:base - hist + tm, :] * cw[0:1, :]
             + ext_ref[base - shift:base - shift + tm, :] * cw[1:2, :] + gate * cw[2:3, :])
        tail = ext_ref[base + tm - hist:base + tm, :]
        tail_ref[0] = tail
        carry_ref[j] = tail
        act_ref[j % 2] = ((c * jax.nn.sigmoid(c)) * up).astype(bf16)

    def down_stage():
        o_ref[...] += jnp.dot(act_ref[(j + 1) % 2], wd_ref[...], preferred_element_type=f32)

    @pl.when(j == 0)
    def _():
        @pl.when(i == 0)
        def _():
            carry_ref[...] = jnp.zeros_like(carry_ref)

        h = h_ref[...]
        ms = jnp.mean(h * h, axis=-1, keepdims=True)
        n_ref[...] = (h * lax.rsqrt(ms + RMS_EPS) * g2_ref[...]).astype(bf16)
        o_ref[...] = jnp.zeros_like(o_ref)
        up_stage()

    @pl.when((j > 0) & (j < nj))
    def _():
        up_stage()
        down_stage()

    @pl.when(j == nj)
    def _():
        down_stage()
        out = h_ref[...] + o_ref[...]
        if final_norm:
            ms = jnp.mean(out * out, axis=-1, keepdims=True)
            out = out * lax.rsqrt(ms + RMS_EPS) * gf_ref[...]
        o_ref[...] = out


def _convglu(h2d, conv_prev, norm2, norm_f, w_up_b, conv_w, conv_b, w_down_b, nseq_groups, shift, final_norm):
    m, d = h2d.shape
    d_ff = w_down_b.shape[0]
    rows_per_group = m // nseq_groups
    tm = 1024 if rows_per_group % 1024 == 0 else min(512, rows_per_group)
    tf = 512
    nj = d_ff // tf
    tiles_per_seq = rows_per_group // tm
    hist = 2 * shift
    base = ((hist + 7) // 8) * 8
    up_j = lambda j: jnp.minimum(j, nj - 1)
    down_j = lambda j: jnp.maximum(j - 1, 0)
    out, tails = pl.pallas_call(
        functools.partial(_convglu_body, shift, tiles_per_seq, final_norm),
        grid=(m // tm, nj + 1),
        in_specs=[
            pl.BlockSpec((tm, d), lambda i, j: (i, 0)),
            pl.BlockSpec((1, d), lambda i, j: (0, 0)),
            pl.BlockSpec((1, d), lambda i, j: (0, 0)),
            pl.BlockSpec((1, hist, tf), lambda i, j: (i // tiles_per_seq, 0, up_j(j))),
            pl.BlockSpec((d, tf), lambda i, j: (0, up_j(j))),
            pl.BlockSpec((d, tf), lambda i, j: (0, nj + up_j(j))),
            pl.BlockSpec((CONV_W, tf), lambda i, j: (0, up_j(j))),
            pl.BlockSpec((1, tf), lambda i, j: (0, up_j(j))),
            pl.BlockSpec((tf, d), lambda i, j: (down_j(j), 0)),
        ],
        out_specs=[
            pl.BlockSpec((tm, d), lambda i, j: (i, 0)),
            pl.BlockSpec((1, hist, tf), lambda i, j: (i, 0, up_j(j))),
        ],
        out_shape=[jax.ShapeDtypeStruct((m, d), f32), jax.ShapeDtypeStruct((m // tm, hist, d_ff), f32)],
        scratch_shapes=[pltpu.VMEM((tm, d), bf16), pltpu.VMEM((base + tm, tf), f32),
                        pltpu.VMEM((nj, hist, tf), f32), pltpu.VMEM((2, tm, tf), bf16)],
        compiler_params=_cparams(("arbitrary", "arbitrary"), CONVGLU_VMEM_LIMIT),
        name="convglu",
    )(h2d, norm2, norm_f, conv_prev, w_up_b, w_up_b, conv_w, conv_b, w_down_b)
    return out, tails[tiles_per_seq - 1::tiles_per_seq]


def _pad_cols(a, width):
    return jnp.pad(a, [(0, 0)] * (a.ndim - 1) + [(0, width - a.shape[-1])])


def _to_layout_rows(a):
    def pad_rows(x, n):
        return jnp.pad(x, ((0, n - x.shape[0]), (0, 0)))

    rw = 3 * RW_WIDTH
    o = {}
    o["r"], o["k"], o["v"] = a[0:RW_WIDTH], a[RW_WIDTH:2 * RW_WIDTH], a[2 * RW_WIDTH:rw]
    p = rw
    o["wl"] = a[p:p + D_DECAY_LORA]; p += D_DECAY_LORA
    o["al"] = a[p:p + D_AAA_LORA]; p += D_AAA_LORA
    o["gl"] = a[p:p + D_GATE_LORA]; p += D_GATE_LORA
    o["q"] = a[p:p + ATT_WIDTH]; p += ATT_WIDTH
    o["ak"] = a[p:p + KV_WIDTH]; p += KV_WIDTH
    o["av"] = a[p:p + KV_WIDTH]; p += KV_WIDTH
    o["qi"] = a[p:p + IDX_HEADS * IDX_DIM]; p += IDX_HEADS * IDX_DIM
    o["ki"] = a[p:p + IDX_DIM]; p += IDX_DIM
    o["wi"] = a[p:p + IDX_HEADS]; p += IDX_HEADS
    d = (a.shape[0] - p) // 2
    o["grw"], o["gatt"] = a[p:p + d], a[p + d:p + 2 * d]
    return jnp.concatenate([
        o["r"], o["k"], o["v"], o["q"], o["grw"], o["gatt"], o["qi"],
        pad_rows(o["wl"], LANES), pad_rows(o["al"], LANES), o["gl"],
        pad_rows(o["ki"], LANES), o["ak"], o["av"], pad_rows(o["wi"], LANES)], axis=0)


def _rw_cols_layout(a):
    rw = 3 * RW_WIDTH
    wl = a[..., rw:rw + D_DECAY_LORA]
    al = a[..., rw + D_DECAY_LORA:rw + D_DECAY_LORA + D_AAA_LORA]
    gl = a[..., rw + D_DECAY_LORA + D_AAA_LORA:]
    return a[..., :rw], jnp.concatenate([_pad_cols(wl, LANES), _pad_cols(al, LANES), gl], axis=-1)


def _rw_cols_from_layout(c):
    return jnp.concatenate([c[..., :3 * RW_WIDTH], c[..., C_LORA:C_LORA + D_DECAY_LORA],
                            c[..., C_LORA + LANES:C_LORA + LANES + D_AAA_LORA],
                            c[..., C_LORA + 2 * LANES:C_LORA + 3 * LANES]], axis=-1)


def _rope_tables(pos, rows):
    pos = jnp.pad(pos.astype(f32), (0, rows - pos.shape[0]))
    out = []
    for dim in (HEAD_DIM, IDX_DIM):
        half = dim // 2
        inv_freq = 1.0 / (ROPE_THETA ** (jnp.arange(half, dtype=f32) / half))
        ang = pos[:, None] * inv_freq[None, :]
        cos, sin = jnp.cos(ang), jnp.sin(ang)
        reps = LANES // dim
        out.append(jnp.tile(jnp.concatenate([cos, cos], axis=1), (1, reps)))
        out.append(jnp.tile(jnp.concatenate([-sin, sin], axis=1), (1, reps)))
    return tuple(out)


def _pair_unblock(s):
    n = s.shape[0]
    a = s[:, :, :RW_HEAD, :RW_HEAD]
    b = s[:, :, RW_HEAD:, RW_HEAD:]
    return jnp.stack([a, b], axis=2).reshape(n, RW_HEADS, RW_HEAD, RW_HEAD)


def kernel(x_prompt, x_sample, cache_k, cache_v, cache_kidx, state_wkv, state_shift, state_conv, page_table, norm1, w_in, rw_mu, rw_w0, rw_w2, rw_a0, rw_a2, rw_g2, rw_k_k, rw_k_a, rw_r_k, rw_lnx_w, rw_lnx_b, p_rw, p_att, w_o, norm2, w_up, conv_w, conv_b, w_down, norm_f):
    B, S, D = x_prompt.shape
    DB, DS, _ = x_sample.shape
    depth = w_in.shape[0]
    page = cache_k.shape[2]
    n_pages = page_table.shape[1]
    past_len = n_pages * page
    d_ff = w_down.shape[1]
    dt = x_prompt.dtype

    tabs_p = _rope_tables(jnp.arange(S), S)
    tabs_s = _rope_tables(past_len + jnp.arange(DS), ROWS)
    row2 = lambda a: a.reshape(1, -1)

    hp = x_prompt.reshape(B * S, D)
    hs = x_sample.reshape(DB * DS, D)
    outs = {k: [] for k in ("kp", "vp", "kip", "ks", "vs", "kis", "wkvp", "wkvs", "shp", "shs", "cvp", "cvs")}
    for l in range(depth):
        w_in_t = _to_layout_rows(w_in[l].T).astype(bf16)
        mu_rkv, mu_lora = _rw_cols_layout(rw_mu[l][None, :])
        params = dict(
            mu_rkv=mu_rkv, mu_lora=mu_lora, w0=row2(rw_w0[l]),
            w2=jnp.pad(rw_w2[l], ((0, LANES - D_DECAY_LORA), (0, 0))).astype(bf16), a0=row2(rw_a0[l]),
            a2=jnp.pad(rw_a2[l], ((0, LANES - D_AAA_LORA), (0, 0))).astype(bf16), g2=rw_g2[l].astype(bf16),
            k_k=row2(rw_k_k[l]), k_a=row2(rw_k_a[l]), r_k=row2(rw_r_k[l]), lnx_w=row2(rw_lnx_w[l]),
            lnx_b=row2(rw_lnx_b[l]))
        last = l == depth - 1

        c_p = _in_proj(hp, row2(norm1[l]), w_in_t)
        c_s = _in_proj(hs, row2(norm1[l]), w_in_t)

        sp_rkv = jnp.zeros((B, 1, RKV_W), dt)
        sp_lora = jnp.zeros((B, 1, LORA_W), dt)
        s0_p = jnp.zeros((B, RW_HEADS // 2, LANES, LANES), dt)
        o_rw_p, wkv_p = _rwkv_prompt(c_p, sp_rkv, sp_lora, s0_p, params, B, S)

        c_s3 = c_s.reshape(DB, DS, NP_COLS)
        ss_rkv, ss_lora = _rw_cols_layout(state_shift[l])
        prev_rkv = jnp.concatenate([ss_rkv[:, None, :], c_s3[:, :-1, :RKV_W]], axis=1).reshape(DB * DS, RKV_W)
        prev_lora = jnp.concatenate([ss_lora[:, None, :], c_s3[:, :-1, C_LORA:C_LORA + LORA_W]], axis=1)
        prev_lora = prev_lora.reshape(DB * DS, LORA_W)
        r_s, lw_s, k_s, v_s, a_s, b_s, g_s = _rwkv_prep(c_s, prev_rkv, prev_lora, params)
        y_s, wkv_s = _rwkv_seq((r_s, lw_s, k_s, v_s, a_s, b_s), state_wkv[l], DB, DS)
        o_rw_s = _rwkv_post(y_s, r_s, k_s, v_s, g_s, params)

        o_att_p, k_p, v_p, ki_p = _dsa_prompt(c_p, tabs_p, B, S)

        def rows8(lo, w):
            return jnp.pad(c_s3[:, :, lo:lo + w], ((0, 0), (0, ROWS - DS), (0, 0)))

        csel = (rows8(C_Q, ATT_WIDTH), rows8(C_QI, IDX_HEADS * IDX_DIM), rows8(C_AK, KV_WIDTH),
                rows8(C_AV, KV_WIDTH), rows8(C_KI, LANES))
        wi_s = jnp.pad(c_s3[:, :, C_WI:C_WI + IDX_HEADS], ((0, 0), (0, ROWS - DS), (0, 0)))
        wrow = jnp.transpose(wi_s, (0, 2, 1)).reshape(DB, 1, IDX_HEADS * ROWS)
        o_att_s8, k_s8, ki_s8 = _dsa_sample(
            csel, wrow, tabs_s, page_table, cache_k[l].reshape(-1, page * KV_HEADS, HEAD_DIM),
            cache_v[l].reshape(-1, page * KV_HEADS, HEAD_DIM), jnp.swapaxes(cache_kidx[l], 1, 2), DS)
        o_att_s = o_att_s8[:, :DS].reshape(DB * DS, ATT_WIDTH)

        p_rw_b, p_att_b, w_o_b = p_rw[l].astype(bf16), p_att[l].astype(bf16), w_o[l].astype(bf16)
        h_p = _outproj(_merge(o_rw_p, o_att_p, c_p, p_rw_b, p_att_b), hp, w_o_b)
        h_s = _outproj(_merge(o_rw_s, o_att_s, c_s, p_rw_b, p_att_b), hs, w_o_b)

        w_up_b, w_down_b = w_up[l].astype(bf16), w_down[l].astype(bf16)
        cv_args = (row2(norm2[l]), row2(norm_f), w_up_b, conv_w[l], row2(conv_b[l]), w_down_b)
        hp, tail_p = _convglu(h_p, jnp.zeros((B, CONV_W - 1, d_ff), dt), *cv_args, B, 1, last)
        h_s_tm = h_s.reshape(DB, DS, D).transpose(1, 0, 2).reshape(DS * DB, D)
        cprev_tm = state_conv[l].transpose(1, 0, 2).reshape(1, (CONV_W - 1) * DB, d_ff)
        hs_tm, tail_s = _convglu(h_s_tm, cprev_tm, *cv_args, 1, DB, last)
        hs = hs_tm.reshape(DS, DB, D).transpose(1, 0, 2).reshape(DB * DS, D)

        outs["kp"].append(k_p.reshape(B, S // page, page, KV_HEADS, HEAD_DIM))
        outs["vp"].append(v_p.reshape(B, S // page, page, KV_HEADS, HEAD_DIM))
        outs["kip"].append(ki_p.reshape(B, S // page, page, IDX_DIM))
        outs["ks"].append(k_s8[:, :DS].reshape(DB, DS, KV_HEADS, HEAD_DIM))
        outs["vs"].append(c_s3[:, :, C_AV:C_AV + KV_WIDTH].reshape(DB, DS, KV_HEADS, HEAD_DIM))
        outs["kis"].append(ki_s8[:, :DS])
        outs["wkvp"].append(_pair_unblock(wkv_p))
        outs["wkvs"].append(wkv_s)
        outs["shp"].append(_rw_cols_from_layout(c_p.reshape(B, S, NP_COLS)[:, -1]))
        outs["shs"].append(_rw_cols_from_layout(c_s3[:, -1]))
        outs["cvp"].append(tail_p)
        outs["cvs"].append(tail_s.reshape(CONV_W - 1, DB, d_ff).transpose(1, 0, 2))

    y_prompt = hp.reshape(B, S, D)
    y_sample = hs.reshape(DB, DS, D)
    st = lambda k: jnp.stack(outs[k])
    return (y_prompt, y_sample, st("kp"), st("vp"), st("kip"), st("ks"), st("vs"), st("kis"), st("wkvp"),
            st("wkvs"), st("shp"), st("shs"), st("cvp"), st("cvs"))
```

```python
import functools
import math

import numpy as np
import jax
import jax.numpy as jnp
from jax import lax
from jax.experimental import pallas as pl
from jax.experimental.pallas import tpu as pltpu

f32 = jnp.float32
bf16 = jnp.bfloat16
i32 = jnp.int32

RW_HEADS = 16
RW_HEAD = 64
RW_WIDTH = RW_HEADS * RW_HEAD
D_DECAY_LORA = 96
D_AAA_LORA = 96
D_GATE_LORA = 128
GN_EPS = 64e-5
ATT_HEADS = 8
KV_HEADS = 2
HEAD_DIM = 128
ATT_WIDTH = ATT_HEADS * HEAD_DIM
KV_WIDTH = KV_HEADS * HEAD_DIM
IDX_HEADS = 16
IDX_DIM = 64
IDX_W_SCALE = (IDX_HEADS * IDX_DIM) ** -0.5
TOPK_MAX = 256
Q_BLOCK = 128
ROPE_THETA = 10000.0
CONV_W = 3
RMS_EPS = 1e-6

LANES = 128
NEG_BIG = -1e30
INT_MIN = -(2 ** 31)

C_R, C_K, C_V, C_Q, C_GRW, C_GATT, C_QI = 0, 1024, 2048, 3072, 4096, 6144, 8192
C_LORA = 9216
C_KI, C_AK, C_AV, C_WI = 9600, 9728, 9984, 10240
NP_COLS = 10368
LORA_W = 384
RKV_W = 3 * RW_WIDTH

VMEM_LIMIT = 56 * 1024 * 1024
CONVGLU_VMEM_LIMIT = 61 * 1024 * 1024


def _cparams(sem, vmem_limit=VMEM_LIMIT):
    return pltpu.CompilerParams(dimension_semantics=sem, vmem_limit_bytes=vmem_limit)


def _bdot(a, b):
    return jnp.dot(a.astype(bf16), b.astype(bf16), preferred_element_type=f32)


def _bdot_nt(a, b):
    return lax.dot_general(a.astype(bf16), b.astype(bf16), (((1,), (1,)), ((), ())), preferred_element_type=f32)


def _bdot_tn(a, b):
    return lax.dot_general(a.astype(bf16), b.astype(bf16), (((0,), (0,)), ((), ())), preferred_element_type=f32)


def _split3(x):
    x1 = x.astype(bf16)
    r1 = x - x1.astype(f32)
    x2 = r1.astype(bf16)
    x3 = (r1 - x2.astype(f32)).astype(bf16)
    return x1, x2, x3


def _split2(x):
    x1 = x.astype(bf16)
    return x1, (x - x1.astype(f32)).astype(bf16)


def _headsum(x):
    ri = lax.broadcasted_iota(i32, (LANES, LANES), 0) // RW_HEAD
    ci = lax.broadcasted_iota(i32, (LANES, LANES), 1) // RW_HEAD
    bd = (ri == ci).astype(bf16)
    outs = []
    for i in range(x.shape[1] // LANES):
        hi, lo = _split2(x[:, i * LANES:(i + 1) * LANES])
        outs.append(jnp.dot(hi, bd, preferred_element_type=f32) + jnp.dot(lo, bd, preferred_element_type=f32))
    return jnp.concatenate(outs, axis=1)


def _softplus(x):
    return jnp.maximum(x, 0.0) + jnp.log(1.0 + jnp.exp(-jnp.abs(x)))


def _rope(x, cos, sin, half):
    w = x.shape[1]
    reps = w // LANES
    if reps > 1:
        cos = jnp.concatenate([cos] * reps, axis=1)
        sin = jnp.concatenate([sin] * reps, axis=1)
    if 2 * half == LANES and w == LANES:
        partner = pltpu.roll(x, half, axis=1)
    else:
        lane = lax.broadcasted_iota(i32, (1, w), 1)
        first = (lane % (2 * half)) < half
        partner = jnp.where(first, pltpu.roll(x, w - half, axis=1), pltpu.roll(x, half, axis=1))
    return x * cos + partner * sin


def _inproj_body(x_ref, g_ref, w_ref, o_ref, n_ref):
    @pl.when(pl.program_id(1) == 0)
    def _():
        x = x_ref[...]
        ms = jnp.mean(x * x, axis=-1, keepdims=True)
        n_ref[...] = (x * lax.rsqrt(ms + RMS_EPS) * g_ref[...]).astype(bf16)

    o_ref[...] = lax.dot_general(n_ref[...], w_ref[...], (((1,), (1,)), ((), ())), preferred_element_type=f32)


def _in_proj(x2d, gain, w_t):
    m, d = x2d.shape
    tm = 1024 if m % 1024 == 0 else min(512, m)
    tn = NP_COLS // 9
    return pl.pallas_call(
        _inproj_body,
        grid=(m // tm, NP_COLS // tn),
        in_specs=[
            pl.BlockSpec((tm, d), lambda i, j: (i, 0)),
            pl.BlockSpec((1, d), lambda i, j: (0, 0)),
            pl.BlockSpec((tn, d), lambda i, j: (j, 0)),
        ],
        out_specs=pl.BlockSpec((tm, tn), lambda i, j: (i, j)),
        out_shape=jax.ShapeDtypeStruct((m, NP_COLS), f32),
        scratch_shapes=[pltpu.VMEM((tm, d), bf16)],
        compiler_params=_cparams(("parallel", "arbitrary")),
        name="in_proj",
    )(x2d, gain, w_t)


RW_PARAM_NAMES = ("mu_rkv", "mu_lora", "w0", "w2", "a0", "a2", "g2", "k_k", "k_a", "r_k", "lnx_w", "lnx_b")


def _rw_prep(c_rkv, c_lora, p_rkv, p_lora, P):
    m = c_rkv + (p_rkv - c_rkv) * P["mu_rkv"]
    ml = c_lora + (p_lora - c_lora) * P["mu_lora"]
    r, k, v = m[:, :RW_WIDTH], m[:, RW_WIDTH:2 * RW_WIDTH], m[:, 2 * RW_WIDTH:]
    wl, al, gl = ml[:, :LANES], ml[:, LANES:2 * LANES], ml[:, 2 * LANES:]
    w_log = -_softplus(-(P["w0"] + _bdot(jnp.tanh(wl), P["w2"]))) - 0.5
    logw = -jnp.exp(w_log)
    asig = jax.nn.sigmoid(P["a0"] + _bdot(al, P["a2"]))
    g = _bdot(jax.nn.sigmoid(gl), P["g2"])
    kk = k * P["k_k"]
    kkn = kk * lax.rsqrt(jnp.maximum(_headsum(kk * kk), 1e-24))
    k2 = k * (1.0 + (asig - 1.0) * P["k_a"])
    return r, logw, k2, v, -kkn, kkn * asig, g


def _rw_post(y, r, k2, v, g, P):
    inv_n = 1.0 / RW_HEAD
    mean = _headsum(y) * inv_n
    d = y - mean
    var = _headsum(d * d) * inv_n
    yn = d * lax.rsqrt(var + GN_EPS) * P["lnx_w"] + P["lnx_b"]
    bonus = _headsum(r * k2 * P["r_k"]) * v
    return (yn + bonus) * g


def _chunk_pre(at, rt, bt, kt, v, masks, nsteps):
    keep, eye2, colh, mA, _ = masks
    R = range(len(at))
    C = at[0].shape[0]
    X = [jnp.concatenate([at[p], rt[p]], axis=0) for p in R]
    scA = [jnp.where(keep, _bdot_nt(jnp.where(mA, X[p], 0.0), jnp.concatenate([bt[p], kt[p]], axis=0)), 0.0)
           for p in R]
    scB = [jnp.where(keep, _bdot_nt(jnp.where(mA, 0.0, X[p]), jnp.concatenate([kt[p], bt[p]], axis=0)), 0.0)
           for p in R]

    def bdiag(x):
        return jnp.concatenate([jnp.where(colh, x, 0.0), jnp.where(colh, 0.0, x)], axis=0)

    L = [jnp.where(colh, scA[p][:C], scB[p][:C]) for p in R]
    T = [eye2 + L[p] for p in R]
    if nsteps > 0:
        Pw = [_bdot(L[p], bdiag(L[p])) for p in R]
        for i in range(nsteps):
            if i < nsteps - 1:
                Z = [_bdot(Pw[p], jnp.concatenate([bdiag(T[p]), bdiag(Pw[p])], axis=1)) for p in R]
                T = [T[p] + Z[p][:, :2 * C] for p in R]
                Pw = [Z[p][:, 2 * C:] for p in R]
            else:
                T = [T[p] + _bdot(Pw[p], bdiag(T[p])) for p in R]
    lkv = [_bdot(jnp.where(colh, scB[p][:C], scA[p][:C]),
                 jnp.concatenate([jnp.where(mA, 0.0, v[p]), jnp.where(mA, v[p], 0.0)], axis=0)) for p in R]
    readout = [jnp.concatenate([scA[p][C:], scB[p][C:]], axis=1) for p in R]
    return T, lkv, readout


def _chunk_seq(at, rt, bt, kt, v, wc, T, lkv, readout, S, masks):
    _, _, _, mA, bdmask = masks
    R = range(len(at))
    C = at[0].shape[0]
    XS = [_bdot_nt(jnp.concatenate([at[p], rt[p]], axis=0), S[p]) for p in R]
    G = [XS[p][:C] + lkv[p] for p in R]
    U = [_bdot(T[p], jnp.concatenate([jnp.where(mA, G[p], 0.0), jnp.where(mA, 0.0, G[p])], axis=0)) for p in R]
    Y = [XS[p][C:] + _bdot(readout[p], jnp.concatenate([jnp.where(mA, U[p], 0.0), jnp.where(mA, v[p], 0.0),
                                                        jnp.where(mA, 0.0, v[p]), jnp.where(mA, 0.0, U[p])], axis=0))
         for p in R]
    dS = [_bdot_tn(jnp.concatenate([U[p], v[p]], axis=0), jnp.concatenate([bt[p] * wc[p], kt[p] * wc[p]], axis=0))
          for p in R]
    S_new = [S[p] * wc[p] + jnp.where(bdmask, dS[p], 0.0) for p in R]
    return Y, S_new


def _chunk_masks(C):
    row = lax.broadcasted_iota(i32, (2 * C, 2 * C), 0)
    col = lax.broadcasted_iota(i32, (2 * C, 2 * C), 1)
    t = jnp.where(row >= C, row - C, row)
    s = jnp.where(col >= C, col - C, col)
    keep = (s < t) | ((row >= C) & (s == t))
    hrow = lax.broadcasted_iota(i32, (C, 2 * C), 0)
    hcol = lax.broadcasted_iota(i32, (C, 2 * C), 1)
    eye2 = (hrow == jnp.where(hcol >= C, hcol - C, hcol)).astype(f32)
    colh = hcol < C
    mA = lax.broadcasted_iota(i32, (1, LANES), 1) < RW_HEAD
    r2 = lax.broadcasted_iota(i32, (LANES, LANES), 0) // RW_HEAD
    c2 = lax.broadcasted_iota(i32, (LANES, LANES), 1) // RW_HEAD
    return keep, eye2, colh, mA, r2 == c2


def _rwkv_prompt_body(chunk, rkv_ref, lora_ref, sp_rkv_ref, sp_lora_ref, s0_ref, *rest):
    np_ = len(RW_PARAM_NAMES)
    P = {n: rest[i][...] for i, n in enumerate(RW_PARAM_NAMES)}
    o_ref, sout_ref = rest[np_], rest[np_ + 1]
    (S_ref, car_rkv, car_lora, at_s, rt_s, bt_s, kt_s, v_s, cum_s, y_s, t_s, lkv_s, ro_s) = rest[np_ + 2:]
    t = pl.program_id(1)
    tt = rkv_ref.shape[0]
    C = chunk

    @pl.when(t == 0)
    def _():
        S_ref[...] = s0_ref[0]
        car_rkv[...] = sp_rkv_ref[0]
        car_lora[...] = sp_lora_ref[0]

    c_rkv = rkv_ref[...]
    c_lora = lora_ref[...]
    first = lax.broadcasted_iota(i32, (tt, 1), 0) == 0
    p_rkv = jnp.where(first, car_rkv[...], pltpu.roll(c_rkv, 1, axis=0))
    p_lora = jnp.where(first, car_lora[...], pltpu.roll(c_lora, 1, axis=0))
    car_rkv[...] = c_rkv[tt - 1:tt, :]
    car_lora[...] = c_lora[tt - 1:tt, :]
    r, logw, k2, v, a, b, g = _rw_prep(c_rkv, c_lora, p_rkv, p_lora, P)

    ri = lax.broadcasted_iota(i32, (tt, tt), 0)
    ci = lax.broadcasted_iota(i32, (tt, tt), 1)
    tril = ((ri // C == ci // C) & (ci <= ri)).astype(bf16)
    cum = sum(jnp.dot(tril, piece, preferred_element_type=f32) for piece in _split3(logw))
    iw = jnp.exp(-cum)
    at_s[...] = a * jnp.exp(cum - logw)
    rt_s[...] = r * jnp.exp(cum)
    bt_s[...] = b * iw
    kt_s[...] = k2 * iw
    v_s[...] = v
    cum_s[...] = cum
    masks = _chunk_masks(C)
    nsteps = max(int(math.ceil(math.log2(C))) - 1, 0)
    pairs = range(RW_WIDTH // LANES)

    lanes = [slice(p * LANES, (p + 1) * LANES) for p in pairs]
    wide = [slice(p * 4 * C, (p + 1) * 4 * C) for p in pairs]

    def ld(ref, r0, cols=lanes):
        return [ref[pl.ds(r0, C), cols[p]] for p in pairs]

    npre = next(u for u in (4, 2, 1) if (tt // C) % u == 0)

    def pre_body(gi, carry):
        r0s = [pl.multiple_of((gi * npre + u) * C, C) for u in range(npre)]
        cat = lambda ref: sum((ld(ref, r0) for r0 in r0s), [])
        T, lkv, readout = _chunk_pre(cat(at_s), cat(rt_s), cat(bt_s), cat(kt_s), cat(v_s), masks, nsteps)
        for u, r0 in enumerate(r0s):
            for p in pairs:
                q = u * len(pairs) + p
                t_s[pl.ds(r0, C), lanes[p]] = T[q]
                lkv_s[pl.ds(r0, C), lanes[p]] = lkv[q]
                ro_s[pl.ds(r0, C), wide[p]] = readout[q]
        return carry

    def seq_body(ci_, carry):
        r0 = pl.multiple_of(ci_ * C, C)
        last8 = pl.multiple_of(r0 + C - 8, 8)
        wc = [jnp.exp(cum_s[pl.ds(last8, 8), lanes[p]][7:8]) for p in pairs]
        Y, S_new = _chunk_seq(ld(at_s, r0), ld(rt_s, r0), ld(bt_s, r0), ld(kt_s, r0), ld(v_s, r0), wc, ld(t_s, r0),
                              ld(lkv_s, r0), ld(ro_s, r0, wide), [S_ref[p] for p in pairs], masks)
        for p in pairs:
            y_s[pl.ds(r0, C), lanes[p]] = Y[p]
            S_ref[p] = S_new[p]
        return carry

    lax.fori_loop(0, tt // C // npre, pre_body, 0)
    lax.fori_loop(0, tt // C, seq_body, 0)
    o_ref[...] = _rw_post(y_s[...], r, k2, v, g, P)

    @pl.when(t == pl.num_programs(1) - 1)
    def _():
        sout_ref[0] = S_ref[...]


def _const_spec(shape):
    nd = len(shape)
    return pl.BlockSpec(shape, lambda *_: (0,) * nd)


def _rwkv_prompt(c, sp_rkv, sp_lora, s0, params, nb, seq):
    tt = min(256, seq)
    chunk = min(64, tt)
    nt = seq // tt
    npairs = RW_WIDTH // LANES
    in_specs = [
        pl.BlockSpec((tt, RKV_W), lambda b, t: (b * nt + t, 0)),
        pl.BlockSpec((tt, LORA_W), lambda b, t: (b * nt + t, C_LORA // LORA_W)),
        pl.BlockSpec((1, 1, RKV_W), lambda b, t: (b, 0, 0)),
        pl.BlockSpec((1, 1, LORA_W), lambda b, t: (b, 0, 0)),
        pl.BlockSpec((1, npairs, LANES, LANES), lambda b, t: (b, 0, 0, 0)),
    ] + [_const_spec(params[n].shape) for n in RW_PARAM_NAMES]
    out_specs = [
        pl.BlockSpec((tt, RW_WIDTH), lambda b, t: (b * nt + t, 0)),
        pl.BlockSpec((1, npairs, LANES, LANES), lambda b, t: (b, 0, 0, 0)),
    ]
    scratch = [pltpu.VMEM((npairs, LANES, LANES), f32), pltpu.VMEM((1, RKV_W), f32), pltpu.VMEM((1, LORA_W), f32)]
    assert 2 * chunk == LANES, "a head pair's chunk matrices fill one 128-lane tile"
    scratch += [pltpu.VMEM((tt, RW_WIDTH), f32) for _ in range(9)] + [pltpu.VMEM((tt, 2 * RW_WIDTH), f32)]
    return pl.pallas_call(
        functools.partial(_rwkv_prompt_body, chunk),
        grid=(nb, nt),
        in_specs=in_specs,
        out_specs=out_specs,
        out_shape=[jax.ShapeDtypeStruct((nb * seq, RW_WIDTH), f32),
                   jax.ShapeDtypeStruct((nb, npairs, LANES, LANES), f32)],
        scratch_shapes=scratch,
        compiler_params=_cparams(("parallel", "arbitrary")),
        name="rwkv_prompt",
    )(c, c, sp_rkv, sp_lora, s0, *[params[n] for n in RW_PARAM_NAMES])


def _rwkv_prep_body(rkv_ref, lora_ref, prkv_ref, plora_ref, *rest):
    np_ = len(RW_PARAM_NAMES)
    P = {n: rest[i][...] for i, n in enumerate(RW_PARAM_NAMES)}
    outs = rest[np_:]
    vals = _rw_prep(rkv_ref[...], lora_ref[...], prkv_ref[...], plora_ref[...], P)
    for o, v in zip(outs, vals):
        o[...] = v


def _rwkv_prep(c, prev_rkv, prev_lora, params):
    m = c.shape[0]
    tm = min(512, m)
    in_specs = [
        pl.BlockSpec((tm, RKV_W), lambda i: (i, 0)),
        pl.BlockSpec((tm, LORA_W), lambda i: (i, C_LORA // LORA_W)),
        pl.BlockSpec((tm, RKV_W), lambda i: (i, 0)),
        pl.BlockSpec((tm, LORA_W), lambda i: (i, 0)),
    ] + [_const_spec(params[n].shape) for n in RW_PARAM_NAMES]
    return pl.pallas_call(
        _rwkv_prep_body,
        grid=(m // tm,),
        in_specs=in_specs,
        out_specs=[pl.BlockSpec((tm, RW_WIDTH), lambda i: (i, 0)) for _ in range(7)],
        out_shape=[jax.ShapeDtypeStruct((m, RW_WIDTH), f32) for _ in range(7)],
        compiler_params=_cparams(("parallel",)),
        name="rwkv_prep",
    )(c, c, prev_rkv, prev_lora, *[params[n] for n in RW_PARAM_NAMES])


def _rwkv_seq_body(r_ref, lw_ref, k_ref, v_ref, a_ref, b_ref, s_ref, y_ref, sout_ref):
    steps = r_ref.shape[0]
    w = [jnp.exp(lw_ref[t]) for t in range(steps)]
    for i in range(RW_HEAD):
        Si = s_ref[0, i]
        for t in range(steps):
            sa = jnp.sum(Si * a_ref[t], axis=0, keepdims=True)
            Si = Si * w[t] + sa * b_ref[t] + v_ref[t, i:i + 1, :] * k_ref[t]
            y_ref[t, i:i + 1, :] = jnp.sum(Si * r_ref[t], axis=0, keepdims=True)
        sout_ref[0, i] = Si


def _rwkv_seq(ops, s0, nreq, steps):
    ops_t = [o.reshape(nreq, steps, RW_WIDTH).transpose(1, 2, 0) for o in ops]
    vec = pl.BlockSpec((steps, RW_HEAD, nreq), lambda h: (0, h, 0))
    st = pl.BlockSpec((1, RW_HEAD, RW_HEAD, nreq), lambda h: (h, 0, 0, 0))
    y_t, s_t = pl.pallas_call(
        _rwkv_seq_body,
        grid=(RW_HEADS,),
        in_specs=[vec] * 6 + [st],
        out_specs=[vec, st],
        out_shape=[jax.ShapeDtypeStruct((steps, RW_WIDTH, nreq), f32),
                   jax.ShapeDtypeStruct((RW_HEADS, RW_HEAD, RW_HEAD, nreq), f32)],
        compiler_params=_cparams(("parallel",)),
        name="rwkv_seq",
    )(*ops_t, s0.transpose(1, 2, 3, 0))
    return y_t.transpose(2, 0, 1).reshape(nreq * steps, RW_WIDTH), s_t.transpose(3, 0, 1, 2)


def _rwkv_post_body(y_ref, r_ref, k_ref, v_ref, g_ref, *rest):
    np_ = len(RW_PARAM_NAMES)
    P = {n: rest[i][...] for i, n in enumerate(RW_PARAM_NAMES)}
    rest[np_][...] = _rw_post(y_ref[...], r_ref[...], k_ref[...], v_ref[...], g_ref[...], P)


def _rwkv_post(y, r, k2, v, g, params):
    m = y.shape[0]
    tm = min(512, m)
    return pl.pallas_call(
        _rwkv_post_body,
        grid=(m // tm,),
        in_specs=[pl.BlockSpec((tm, RW_WIDTH), lambda i: (i, 0)) for _ in range(5)]
        + [_const_spec(params[n].shape) for n in RW_PARAM_NAMES],
        out_specs=pl.BlockSpec((tm, RW_WIDTH), lambda i: (i, 0)),
        out_shape=jax.ShapeDtypeStruct((m, RW_WIDTH), f32),
        compiler_params=_cparams(("parallel",)),
        name="rwkv_post",
    )(y, r, k2, v, g, *[params[n] for n in RW_PARAM_NAMES])


def _sort_key(score):
    bits = pltpu.bitcast(score, i32)
    key = jnp.where(bits < 0, bits ^ jnp.int32(0x7FFFFFFF), bits)
    return jnp.where(score == 0.0, jnp.int32(0), key)


def _kth_largest(keys, k):
    def body(i, ts):
        bit = lax.shift_left(jnp.int32(1), jnp.int32(31) - i)
        cands = [t + bit for t in ts]
        cnts = [jnp.sum((key >= c).astype(f32), axis=1, keepdims=True) for key, c in zip(keys, cands)]
        return tuple(jnp.where(n >= k, c, t) for n, c, t in zip(cnts, cands, ts))

    init = tuple(jnp.full((key.shape[0], 1), INT_MIN, i32) for key in keys)
    return lax.fori_loop(0, 32, body, init, unroll=4)


def _row_groups(x, n):
    step = x.shape[0] // n
    return [x[g * step:(g + 1) * step] for g in range(n)]


def _dsa_prompt_body(topk, q_ref, qi_ref, wi_ref, k_ref, v_ref, ki_ref, cosk_ref, sink_ref, cosi_ref, sini_ref,
                     o_ref, kout_ref, vout_ref, kiout_ref, kb_ref, vb_ref, kib_ref, bias_ref):
    qb = pl.program_id(1)
    seq = k_ref.shape[0]
    nq = q_ref.shape[0]

    @pl.when(qb == 0)
    def _():
        cos, sin = cosk_ref[...], sink_ref[...]
        v = v_ref[...]
        kr = jnp.concatenate([_rope(k_ref[:, h * HEAD_DIM:(h + 1) * HEAD_DIM], cos, sin, HEAD_DIM // 2)
                              for h in range(KV_HEADS)], axis=1)
        for h in range(KV_HEADS):
            kout_ref[pl.ds(h, seq, stride=KV_HEADS), :] = kr[:, h * HEAD_DIM:(h + 1) * HEAD_DIM]
            vout_ref[pl.ds(h, seq, stride=KV_HEADS), :] = v[:, h * HEAD_DIM:(h + 1) * HEAD_DIM]
        kb_ref[...] = kr.astype(bf16)
        vb_ref[...] = v.astype(bf16)
        kir = _rope(ki_ref[...], cosi_ref[...], sini_ref[...], IDX_DIM // 2)
        kiout_ref[...] = kir[:, :IDX_DIM]
        kib_ref[...] = (kir + pltpu.roll(kir, IDX_DIM, axis=1)).astype(bf16)

    r0 = pl.multiple_of(qb * nq, nq)
    cosq, sinq = cosk_ref[pl.ds(r0, nq), :], sink_ref[pl.ds(r0, nq), :]
    cosqi, sinqi = cosi_ref[pl.ds(r0, nq), :], sini_ref[pl.ds(r0, nq), :]
    q = _rope(q_ref[...], cosq, sinq, HEAD_DIM // 2).astype(bf16)
    qi = _rope(qi_ref[...], cosqi, sinqi, IDX_DIM // 2)
    wi = wi_ref[...] * IDX_W_SCALE
    lane = lax.broadcasted_iota(i32, (1, LANES), 1)
    scale = HEAD_DIM ** -0.5 * math.log2(math.e)
    hpg = ATT_HEADS // KV_HEADS

    def process(ext):
        kib = kib_ref[:ext, :]
        sc = None
        for h in range(IDX_HEADS):
            pair = qi[:, (h // 2) * LANES:(h // 2 + 1) * LANES]
            mine = (lane < IDX_DIM) if h % 2 == 0 else (lane >= IDX_DIM)
            term = jnp.maximum(_bdot_nt(jnp.where(mine, pair, 0.0), kib), 0.0) * wi[:, h:h + 1]
            sc = term if sc is None else sc + term

        qpos = r0 + lax.broadcasted_iota(i32, (nq, 1), 0)
        kpos = lax.broadcasted_iota(i32, (1, ext), 1)
        causal = kpos <= qpos
        key = jnp.where(causal, _sort_key(sc), INT_MIN)
        thr = jnp.concatenate(_kth_largest(_row_groups(key, 4), float(topk)), axis=0)
        bias_ref[:, :ext] = jnp.where(causal & (key >= thr), 0.0, NEG_BIG)

        def qk(h):
            kg = kb_ref[:ext, (h // hpg) * HEAD_DIM:(h // hpg + 1) * HEAD_DIM]
            return _bdot_nt(q[:, h * HEAD_DIM:(h + 1) * HEAD_DIM], kg)

        s_next = qk(0)
        for h in range(ATT_HEADS):
            s = s_next * scale + bias_ref[:, :ext]
            if h + 1 < ATT_HEADS:
                s_next = qk(h + 1)
            m = jnp.max(s, axis=1, keepdims=True)
            p = jnp.exp2(s - m)
            l = jnp.sum(p, axis=1, keepdims=True)
            vg = vb_ref[:ext, (h // hpg) * HEAD_DIM:(h // hpg + 1) * HEAD_DIM]
            o = jnp.dot(p.astype(bf16), vg, preferred_element_type=f32)
            o_ref[:, h * HEAD_DIM:(h + 1) * HEAD_DIM] = o / l

    nvar = next(v for v in (8, 4, 2, 1) if (seq // nq) % v == 0)
    per = seq // nq // nvar
    for var in range(nvar):
        pl.when(qb // per == var)(functools.partial(process, (var + 1) * per * nq))


def _dsa_prompt(c, tabs, nb, seq):
    nq = Q_BLOCK
    nblk = seq // nq
    topk = min(TOPK_MAX, seq // 4)
    cosk, sink, cosi, sini = tabs
    row = lambda b, j: b * nblk + j
    in_specs = [
        pl.BlockSpec((nq, ATT_WIDTH), lambda b, j: (row(b, j), C_Q // ATT_WIDTH)),
        pl.BlockSpec((nq, IDX_HEADS * IDX_DIM), lambda b, j: (row(b, j), C_QI // (IDX_HEADS * IDX_DIM))),
        pl.BlockSpec((nq, LANES), lambda b, j: (row(b, j), C_WI // LANES)),
        pl.BlockSpec((seq, KV_WIDTH), lambda b, j: (b, C_AK // KV_WIDTH)),
        pl.BlockSpec((seq, KV_WIDTH), lambda b, j: (b, C_AV // KV_WIDTH)),
        pl.BlockSpec((seq, LANES), lambda b, j: (b, C_KI // LANES)),
    ] + [_const_spec((seq, LANES)) for _ in range(4)]
    out_specs = [
        pl.BlockSpec((nq, ATT_WIDTH), lambda b, j: (row(b, j), 0)),
        pl.BlockSpec((seq * KV_HEADS, HEAD_DIM), lambda b, j: (b, 0)),
        pl.BlockSpec((seq * KV_HEADS, HEAD_DIM), lambda b, j: (b, 0)),
        pl.BlockSpec((seq, IDX_DIM), lambda b, j: (b, 0)),
    ]
    scratch = [pltpu.VMEM((seq, KV_WIDTH), bf16), pltpu.VMEM((seq, KV_WIDTH), bf16), pltpu.VMEM((seq, LANES), bf16),
               pltpu.VMEM((nq, seq), f32)]
    return pl.pallas_call(
        functools.partial(_dsa_prompt_body, topk),
        grid=(nb, nblk),
        in_specs=in_specs,
        out_specs=out_specs,
        out_shape=[jax.ShapeDtypeStruct((nb * seq, ATT_WIDTH), f32),
                   jax.ShapeDtypeStruct((nb * seq * KV_HEADS, HEAD_DIM), f32),
                   jax.ShapeDtypeStruct((nb * seq * KV_HEADS, HEAD_DIM), f32),
                   jax.ShapeDtypeStruct((nb * seq, IDX_DIM), f32)],
        scratch_shapes=scratch,
        compiler_params=_cparams(("parallel", "arbitrary")),
        name="dsa_prompt",
    )(c, c, c, c, c, c, cosk, sink, cosi, sini)


ROWS = 8


def _page_copies(pt_ref, req, n_pages, srcs_dsts_sems):
    out = []
    for p in range(n_pages):
        pg = pt_ref[req, p]
        for hbm, dst, sem in srcs_dsts_sems:
            out.append(pltpu.make_async_copy(hbm.at[pg], dst(p), sem))
    return out


def _dsa_select_body(topk, n_new, group, pt_ref, qi_ref, wrow_ref, kin_ref, cosi_ref, sini_ref, cki_hbm,
                     sel_ref, kiout_ref, kibuf, sems):
    i = pl.program_id(0)
    nsteps = pl.num_programs(0)
    n_pages = pt_ref.shape[1]
    page = cki_hbm.shape[2]
    past = n_pages * page
    G = group

    def copies(step, slot):
        out = []
        for j in range(G):
            out += _page_copies(pt_ref, step * G + j, n_pages,
                                [(cki_hbm, lambda p, j=j: kibuf.at[slot, j, :, pl.ds(p * page, page)],
                                  sems.at[slot])])
        return out

    slot = i % 2

    @pl.when(i == 0)
    def _():
        for cp in copies(0, 0):
            cp.start()

    @pl.when(i + 1 < nsteps)
    def _():
        for cp in copies(i + 1, 1 - slot):
            cp.start()

    cosi, sini = cosi_ref[...], sini_ref[...]
    li = lax.broadcasted_iota(i32, (LANES, LANES), 0)
    lo = lax.broadcasted_iota(i32, (LANES, LANES), 1)
    eye = (li == lo).astype(f32)
    qis, kins, wcols = [], [], []
    for j in range(G):
        qi = _rope(qi_ref[j], cosi, sini, IDX_DIM // 2)
        kin = _rope(kin_ref[j], cosi, sini, IDX_DIM // 2)[:, :IDX_DIM]
        kiout_ref[j] = kin
        kins.append(jnp.concatenate([kin, jnp.zeros((LANES - ROWS, IDX_DIM), f32)], axis=0))
        qis.append(jnp.concatenate([qi[:, h * IDX_DIM:(h + 1) * IDX_DIM] for h in range(IDX_HEADS)], axis=0))
        wrow = wrow_ref[0, :, j * LANES:(j + 1) * LANES] * IDX_W_SCALE
        wcols.append(jnp.sum(eye * wrow, axis=1, keepdims=True))

    for cp in copies(i, slot):
        cp.wait()

    rows = []
    for j in range(G):
        d = jnp.concatenate([_bdot(qis[j], kibuf[slot, j]), _bdot_nt(qis[j], kins[j])], axis=1)
        term = jnp.maximum(d, 0.0) * wcols[j]
        rows.append(sum(term[h * ROWS:(h + 1) * ROWS] for h in range(IDX_HEADS)))
    sc = jnp.concatenate(rows, axis=0)
    nrow = ROWS * G
    width = past + LANES
    t_row = lax.broadcasted_iota(i32, (nrow, width), 0) % ROWS
    col = lax.broadcasted_iota(i32, (nrow, width), 1)
    valid = (col < past) | ((col - past <= t_row) & (col - past < n_new))
    key = jnp.where(valid, _sort_key(sc), INT_MIN)
    thr = jnp.concatenate(_kth_largest(_row_groups(key, G), float(topk)), axis=0)
    chosen = (valid & (key >= thr)).astype(f32)
    if nrow < LANES:
        chosen = jnp.concatenate([chosen, jnp.zeros((LANES - nrow, width), f32)], axis=0)
    sel_ref[0] = chosen.T


def _dsa_sample_body(n_new, group, pt_ref, q_ref, kn_ref, vn_ref, sel_ref, cosk_ref, sink_ref, ck_hbm, cv_hbm,
                     o_ref, kout_ref, kbuf, vbuf, sems):
    b = pl.program_id(0)
    nreq = pl.num_programs(0)
    n_pages = pt_ref.shape[1]
    prow = ck_hbm.shape[1]
    past = n_pages * prow // KV_HEADS
    G = group

    def copies(req, slot):
        return _page_copies(pt_ref, req, n_pages,
                            [(ck_hbm, lambda p: kbuf.at[slot, pl.ds(p * prow, prow)], sems.at[0, slot]),
                             (cv_hbm, lambda p: vbuf.at[slot, pl.ds(p * prow, prow)], sems.at[1, slot])])

    slot = b % 2

    @pl.when(b == 0)
    def _():
        for cp in copies(0, 0):
            cp.start()

    @pl.when(b + 1 < nreq)
    def _():
        for cp in copies(b + 1, 1 - slot):
            cp.start()

    cosk, sink = cosk_ref[...], sink_ref[...]
    q = _rope(q_ref[0], cosk, sink, HEAD_DIM // 2)
    kn = jnp.concatenate([_rope(kn_ref[0][:, h * HEAD_DIM:(h + 1) * HEAD_DIM], cosk, sink, HEAD_DIM // 2)
                          for h in range(KV_HEADS)], axis=1)
    vn = vn_ref[0]
    kout_ref[0] = kn

    li = lax.broadcasted_iota(i32, (LANES, LANES), 0)
    lo = lax.broadcasted_iota(i32, (LANES, LANES), 1)
    route = ((b % G) * ROWS + lo % ROWS == li) & (lo % ROWS < n_new)
    chosen = jnp.dot(sel_ref[0].astype(bf16), route.astype(bf16), preferred_element_type=f32)
    bias = (chosen - 1.0) * (-NEG_BIG)
    bias_p, bias_n = bias[:past], bias[past:past + ROWS]

    eye = (li == lo).astype(f32)
    scale = HEAD_DIM ** -0.5 * math.log2(math.e)
    hpg = ATT_HEADS // KV_HEADS
    zero_rows = lambda n: [jnp.zeros((n, HEAD_DIM), f32)] if n else []

    for cp in copies(b, slot):
        cp.wait()

    s_p = s_n = None
    for g in range(KV_HEADS):
        heads = [q[:, (g * hpg + r) * HEAD_DIM:(g * hpg + r + 1) * HEAD_DIM] for r in range(hpg)]
        qg = jnp.concatenate(zero_rows(g * hpg * ROWS) + heads + zero_rows(LANES - (g + 1) * hpg * ROWS), axis=0)
        kp = kbuf[slot, pl.ds(g, past, stride=KV_HEADS), :]
        part_p = _bdot_nt(kp, qg)
        part_n = _bdot_nt(kn[:, g * HEAD_DIM:(g + 1) * HEAD_DIM], qg)
        s_p = part_p if s_p is None else s_p + part_p
        s_n = part_n if s_n is None else s_n + part_n
    s_p = s_p * scale + bias_p
    s_n = s_n * scale + bias_n
    m = jnp.maximum(jnp.max(s_p, axis=0, keepdims=True), jnp.max(s_n, axis=0, keepdims=True))
    p_p = jnp.exp2(s_p - m)
    p_n = jnp.exp2(s_n - m)
    l = jnp.sum(p_p, axis=0, keepdims=True) + jnp.sum(p_n, axis=0, keepdims=True)
    l_col = jnp.sum(eye * l, axis=1, keepdims=True)
    for g in range(KV_HEADS):
        vp = vbuf[slot, pl.ds(g, past, stride=KV_HEADS), :]
        o = (_bdot_tn(p_p, vp) + _bdot_tn(p_n, vn[:, g * HEAD_DIM:(g + 1) * HEAD_DIM])) / l_col
        for r in range(hpg):
            h = g * hpg + r
            o_ref[0, :, h * HEAD_DIM:(h + 1) * HEAD_DIM] = o[h * ROWS:(h + 1) * ROWS]


def _dsa_sample(csel, wrow, tabs, page_table, cache_k, cache_v, cache_kidx, n_new):
    q8, qi8, kn8, vn8, kin8 = csel
    nreq = q8.shape[0]
    n_pages = page_table.shape[1]
    page = cache_kidx.shape[2]
    past = n_pages * page
    topk = min(TOPK_MAX, (past + n_new) // 4)
    group = min(LANES // ROWS, nreq)
    cosk, sink, cosi, sini = tabs
    anyspec = pl.BlockSpec(memory_space=pl.ANY)
    tab = pl.BlockSpec((ROWS, LANES), lambda b, pt: (0, 0))

    grp3 = lambda w: pl.BlockSpec((group, ROWS, w), lambda i, pt: (i, 0, 0))
    sel, ki_new = pl.pallas_call(
        functools.partial(_dsa_select_body, topk, n_new, group),
        grid_spec=pltpu.PrefetchScalarGridSpec(
            num_scalar_prefetch=1,
            grid=(nreq // group,),
            in_specs=[grp3(IDX_HEADS * IDX_DIM), pl.BlockSpec((1, 1, group * LANES), lambda i, pt: (i, 0, 0)),
                      grp3(LANES), tab, tab, anyspec],
            out_specs=[pl.BlockSpec((1, past + LANES, LANES), lambda i, pt: (i, 0, 0)), grp3(IDX_DIM)],
            scratch_shapes=[pltpu.VMEM((2, group, IDX_DIM, past), f32), pltpu.SemaphoreType.DMA((2,))],
        ),
        out_shape=[jax.ShapeDtypeStruct((nreq // group, past + LANES, LANES), f32),
                   jax.ShapeDtypeStruct((nreq, ROWS, IDX_DIM), f32)],
        compiler_params=_cparams(("arbitrary",)),
        name="dsa_select",
    )(page_table, qi8, wrow.reshape(nreq // group, 1, group * LANES), kin8, cosi, sini, cache_kidx)

    req3 = lambda w: pl.BlockSpec((1, ROWS, w), lambda b, pt: (b, 0, 0))
    prow = cache_k.shape[1]
    o, k_new = pl.pallas_call(
        functools.partial(_dsa_sample_body, n_new, group),
        grid_spec=pltpu.PrefetchScalarGridSpec(
            num_scalar_prefetch=1,
            grid=(nreq,),
            in_specs=[req3(ATT_WIDTH), req3(KV_WIDTH), req3(KV_WIDTH),
                      pl.BlockSpec((1, past + LANES, LANES), lambda b, pt: (b // group, 0, 0)), tab, tab,
                      anyspec, anyspec],
            out_specs=[req3(ATT_WIDTH), req3(KV_WIDTH)],
            scratch_shapes=[pltpu.VMEM((2, n_pages * prow, HEAD_DIM), f32),
                            pltpu.VMEM((2, n_pages * prow, HEAD_DIM), f32), pltpu.SemaphoreType.DMA((2, 2))],
        ),
        out_shape=[jax.ShapeDtypeStruct((nreq, ROWS, ATT_WIDTH), f32),
                   jax.ShapeDtypeStruct((nreq, ROWS, KV_WIDTH), f32)],
        compiler_params=_cparams(("arbitrary",)),
        name="dsa_sample",
    )(page_table, q8, kn8, vn8, sel, cosk, sink, cache_k, cache_v)
    return o, k_new, ki_new


def _merge_body(orw_ref, oatt_ref, grw_ref, gatt_ref, prw_ref, patt_ref, o_ref):
    a = jnp.dot(orw_ref[...].astype(bf16), prw_ref[...], preferred_element_type=f32)
    b = jnp.dot(oatt_ref[...].astype(bf16), patt_ref[...], preferred_element_type=f32)
    o_ref[...] = (jax.nn.sigmoid(grw_ref[...]) * a + jax.nn.sigmoid(gatt_ref[...]) * b).astype(bf16)


def _merge(o_rw, o_att, c, p_rw, p_att):
    m = o_rw.shape[0]
    d = p_rw.shape[1]
    tm = 1024 if m % 1024 == 0 else min(512, m)
    tn = 1024
    nj = d // tn
    return pl.pallas_call(
        _merge_body,
        grid=(m // tm, nj),
        in_specs=[
            pl.BlockSpec((tm, RW_WIDTH), lambda i, j: (i, 0)),
            pl.BlockSpec((tm, ATT_WIDTH), lambda i, j: (i, 0)),
            pl.BlockSpec((tm, tn), lambda i, j: (i, C_GRW // tn + j)),
            pl.BlockSpec((tm, tn), lambda i, j: (i, C_GATT // tn + j)),
            pl.BlockSpec((RW_WIDTH, tn), lambda i, j: (0, j)),
            pl.BlockSpec((ATT_WIDTH, tn), lambda i, j: (0, j)),
        ],
        out_specs=pl.BlockSpec((tm, tn), lambda i, j: (i, j)),
        out_shape=jax.ShapeDtypeStruct((m, d), bf16),
        compiler_params=_cparams(("parallel", "arbitrary")),
        name="merge",
    )(o_rw, o_att, c, c, p_rw, p_att)


def _outproj_body(mg_ref, x_ref, w_ref, o_ref):
    o_ref[...] = x_ref[...] + jnp.dot(mg_ref[...], w_ref[...], preferred_element_type=f32)


def _outproj(merged, x2d, w_o):
    m, d = x2d.shape
    tm = 1024 if m % 1024 == 0 else min(512, m)
    tn = 1024
    return pl.pallas_call(
        _outproj_body,
        grid=(m // tm, d // tn),
        in_specs=[
            pl.BlockSpec((tm, d), lambda i, j: (i, 0)),
            pl.BlockSpec((tm, tn), lambda i, j: (i, j)),
            pl.BlockSpec((d, tn), lambda i, j: (0, j)),
        ],
        out_specs=pl.BlockSpec((tm, tn), lambda i, j: (i, j)),
        out_shape=jax.ShapeDtypeStruct((m, d), f32),
        compiler_params=_cparams(("parallel", "arbitrary")),
        name="out_proj",
    )(merged, x2d, w_o)


def _convglu_body(shift, tiles_per_seq, final_norm, h_ref, g2_ref, gf_ref, cprev_ref, wg_ref, wu_ref, cw_ref,
                  cb_ref, wd_ref, o_ref, tail_ref, n_ref, ext_ref, carry_ref, act_ref):
    i = pl.program_id(0)
    j = pl.program_id(1)
    nj = pl.num_programs(1) - 1
    tm = h_ref.shape[0]
    hist = 2 * shift
    base = ext_ref.shape[0] - tm

    def up_stage():
        n = n_ref[...]
        gate = jnp.dot(n, wg_ref[...], preferred_element_type=f32)
        up = jnp.dot(n, wu_ref[...], preferred_element_type=f32)
        ext_ref[base - hist:base, :] = jnp.where(i % tiles_per_seq == 0, cprev_ref[0], carry_ref[j])
        ext_ref[base:, :] = gate
        cw = cw_ref[...]
        c = (cb_ref[...] + ext_ref[base - hist:base - hist + tm, :] * cw[0:1, :]
             + ext_ref[base - shift:base - shift + tm, :] * cw[1:2, :] + gate * cw[2:3, :])
        tail = ext_ref[base + tm - hist:base + tm, :]
        tail_ref[0] = tail
        carry_ref[j] = tail
        act_ref[j % 2] = ((c * jax.nn.sigmoid(c)) * up).astype(bf16)

    def down_stage():
        o_ref[...] += jnp.dot(act_ref[(j + 1) % 2], wd_ref[...], preferred_element_type=f32)

    @pl.when(j == 0)
    def _():
        @pl.when(i == 0)
        def _():
            carry_ref[...] = jnp.zeros_like(carry_ref)

        h = h_ref[...]
        ms = jnp.mean(h * h, axis=-1, keepdims=True)
        n_ref[...] = (h * lax.rsqrt(ms + RMS_EPS) * g2_ref[...]).astype(bf16)
        o_ref[...] = jnp.zeros_like(o_ref)
        up_stage()

    @pl.when((j > 0) & (j < nj))
    def _():
        up_stage()
        down_stage()

    @pl.when(j == nj)
    def _():
        down_stage()
        out = h_ref[...] + o_ref[...]
        if final_norm:
            ms = jnp.mean(out * out, axis=-1, keepdims=True)
            out = out * lax.rsqrt(ms + RMS_EPS) * gf_ref[...]
        o_ref[...] = out


def _convglu(h2d, conv_prev, norm2, norm_f, w_up_b, conv_w, conv_b, w_down_b, nseq_groups, shift, final_norm):
    m, d = h2d.shape
    d_ff = w_down_b.shape[0]
    rows_per_group = m // nseq_groups
    tm = 1024 if rows_per_group % 1024 == 0 else min(512, rows_per_group)
    tf = 512
    nj = d_ff // tf
    tiles_per_seq = rows_per_group // tm
    hist = 2 * shift
    base = ((hist + 7) // 8) * 8
    up_j = lambda j: jnp.minimum(j, nj - 1)
    down_j = lambda j: jnp.maximum(j - 1, 0)
    out, tails = pl.pallas_call(
        functools.partial(_convglu_body, shift, tiles_per_seq, final_norm),
        grid=(m // tm, nj + 1),
        in_specs=[
            pl.BlockSpec((tm, d), lambda i, j: (i, 0)),
            pl.BlockSpec((1, d), lambda i, j: (0, 0)),
            pl.BlockSpec((1, d), lambda i, j: (0, 0)),
            pl.BlockSpec((1, hist, tf), lambda i, j: (i // tiles_per_seq, 0, up_j(j))),
            pl.BlockSpec((d, tf), lambda i, j: (0, up_j(j))),
            pl.BlockSpec((d, tf), lambda i, j: (0, nj + up_j(j))),
            pl.BlockSpec((CONV_W, tf), lambda i, j: (0, up_j(j))),
            pl.BlockSpec((1, tf), lambda i, j: (0, up_j(j))),
            pl.BlockSpec((tf, d), lambda i, j: (down_j(j), 0)),
        ],
        out_specs=[
            pl.BlockSpec((tm, d), lambda i, j: (i, 0)),
            pl.BlockSpec((1, hist, tf), lambda i, j: (i, 0, up_j(j))),
        ],
        out_shape=[jax.ShapeDtypeStruct((m, d), f32), jax.ShapeDtypeStruct((m // tm, hist, d_ff), f32)],
        scratch_shapes=[pltpu.VMEM((tm, d), bf16), pltpu.VMEM((base + tm, tf), f32),
                        pltpu.VMEM((nj, hist, tf), f32), pltpu.VMEM((2, tm, tf), bf16)],
        compiler_params=_cparams(("arbitrary", "arbitrary"), CONVGLU_VMEM_LIMIT),
        name="convglu",
    )(h2d, norm2, norm_f, conv_prev, w_up_b, w_up_b, conv_w, conv_b, w_down_b)
    return out, tails[tiles_per_seq - 1::tiles_per_seq]


def _pad_cols(a, width):
    return jnp.pad(a, [(0, 0)] * (a.ndim - 1) + [(0, width - a.shape[-1])])


def _to_layout_rows(a):
    def pad_rows(x, n):
        return jnp.pad(x, ((0, n - x.shape[0]), (0, 0)))

    rw = 3 * RW_WIDTH
    o = {}
    o["r"], o["k"], o["v"] = a[0:RW_WIDTH], a[RW_WIDTH:2 * RW_WIDTH], a[2 * RW_WIDTH:rw]
    p = rw
    o["wl"] = a[p:p + D_DECAY_LORA]; p += D_DECAY_LORA
    o["al"] = a[p:p + D_AAA_LORA]; p += D_AAA_LORA
    o["gl"] = a[p:p + D_GATE_LORA]; p += D_GATE_LORA
    o["q"] = a[p:p + ATT_WIDTH]; p += ATT_WIDTH
    o["ak"] = a[p:p + KV_WIDTH]; p += KV_WIDTH
    o["av"] = a[p:p + KV_WIDTH]; p += KV_WIDTH
    o["qi"] = a[p:p + IDX_HEADS * IDX_DIM]; p += IDX_HEADS * IDX_DIM
    o["ki"] = a[p:p + IDX_DIM]; p += IDX_DIM
    o["wi"] = a[p:p + IDX_HEADS]; p += IDX_HEADS
    d = (a.shape[0] - p) // 2
    o["grw"], o["gatt"] = a[p:p + d], a[p + d:p + 2 * d]
    return jnp.concatenate([
        o["r"], o["k"], o["v"], o["q"], o["grw"], o["gatt"], o["qi"],
        pad_rows(o["wl"], LANES), pad_rows(o["al"], LANES), o["gl"],
        pad_rows(o["ki"], LANES), o["ak"], o["av"], pad_rows(o["wi"], LANES)], axis=0)


def _rw_cols_layout(a):
    rw = 3 * RW_WIDTH
    wl = a[..., rw:rw + D_DECAY_LORA]
    al = a[..., rw + D_DECAY_LORA:rw + D_DECAY_LORA + D_AAA_LORA]
    gl = a[..., rw + D_DECAY_LORA + D_AAA_LORA:]
    return a[..., :rw], jnp.concatenate([_pad_cols(wl, LANES), _pad_cols(al, LANES), gl], axis=-1)


def _rw_cols_from_layout(c):
    return jnp.concatenate([c[..., :3 * RW_WIDTH], c[..., C_LORA:C_LORA + D_DECAY_LORA],
                            c[..., C_LORA + LANES:C_LORA + LANES + D_AAA_LORA],
                            c[..., C_LORA + 2 * LANES:C_LORA + 3 * LANES]], axis=-1)


def _rope_tables(pos, rows):
    pos = jnp.pad(pos.astype(f32), (0, rows - pos.shape[0]))
    out = []
    for dim in (HEAD_DIM, IDX_DIM):
        half = dim // 2
        inv_freq = 1.0 / (ROPE_THETA ** (jnp.arange(half, dtype=f32) / half))
        ang = pos[:, None] * inv_freq[None, :]
        cos, sin = jnp.cos(ang), jnp.sin(ang)
        reps = LANES // dim
        out.append(jnp.tile(jnp.concatenate([cos, cos], axis=1), (1, reps)))
        out.append(jnp.tile(jnp.concatenate([-sin, sin], axis=1), (1, reps)))
    return tuple(out)


def _pair_unblock(s):
    n = s.shape[0]
    a = s[:, :, :RW_HEAD, :RW_HEAD]
    b = s[:, :, RW_HEAD:, RW_HEAD:]
    return jnp.stack([a, b], axis=2).reshape(n, RW_HEADS, RW_HEAD, RW_HEAD)


def kernel(x_prompt, x_sample, cache_k, cache_v, cache_kidx, state_wkv, state_shift, state_conv, page_table, norm1, w_in, rw_mu, rw_w0, rw_w2, rw_a0, rw_a2, rw_g2, rw_k_k, rw_k_a, rw_r_k, rw_lnx_w, rw_lnx_b, p_rw, p_att, w_o, norm2, w_up, conv_w, conv_b, w_down, norm_f):
    B, S, D = x_prompt.shape
    DB, DS, _ = x_sample.shape
    depth = w_in.shape[0]
    page = cache_k.shape[2]
    n_pages = page_table.shape[1]
    past_len = n_pages * page
    d_ff = w_down.shape[1]
    dt = x_prompt.dtype

    tabs_p = _rope_tables(jnp.arange(S), S)
    tabs_s = _rope_tables(past_len + jnp.arange(DS), ROWS)
    row2 = lambda a: a.reshape(1, -1)

    hp = x_prompt.reshape(B * S, D)
    hs = x_sample.reshape(DB * DS, D)
    outs = {k: [] for k in ("kp", "vp", "kip", "ks", "vs", "kis", "wkvp", "wkvs", "shp", "shs", "cvp", "cvs")}
    for l in range(depth):
        w_in_t = _to_layout_rows(w_in[l].T).astype(bf16)
        mu_rkv, mu_lora = _rw_cols_layout(rw_mu[l][None, :])
        params = dict(
            mu_rkv=mu_rkv, mu_lora=mu_lora, w0=row2(rw_w0[l]),
            w2=jnp.pad(rw_w2[l], ((0, LANES - D_DECAY_LORA), (0, 0))).astype(bf16), a0=row2(rw_a0[l]),
            a2=jnp.pad(rw_a2[l], ((0, LANES - D_AAA_LORA), (0, 0))).astype(bf16), g2=rw_g2[l].astype(bf16),
            k_k=row2(rw_k_k[l]), k_a=row2(rw_k_a[l]), r_k=row2(rw_r_k[l]), lnx_w=row2(rw_lnx_w[l]),
            lnx_b=row2(rw_lnx_b[l]))
        last = l == depth - 1

        c_p = _in_proj(hp, row2(norm1[l]), w_in_t)
        c_s = _in_proj(hs, row2(norm1[l]), w_in_t)

        sp_rkv = jnp.zeros((B, 1, RKV_W), dt)
        sp_lora = jnp.zeros((B, 1, LORA_W), dt)
        s0_p = jnp.zeros((B, RW_HEADS // 2, LANES, LANES), dt)
        o_rw_p, wkv_p = _rwkv_prompt(c_p, sp_rkv, sp_lora, s0_p, params, B, S)

        def req3(lo, w):
            return c_s[:, lo:lo + w].reshape(DB, DS, w)

        ss_rkv, ss_lora = _rw_cols_layout(state_shift[l])
        prev_rkv = jnp.concatenate([ss_rkv[:, None, :], req3(0, RKV_W)[:, :-1]], axis=1).reshape(DB * DS, RKV_W)
        prev_lora = jnp.concatenate([ss_lora[:, None, :], req3(C_LORA, LORA_W)[:, :-1]], axis=1)
        prev_lora = prev_lora.reshape(DB * DS, LORA_W)
        r_s, lw_s, k_s, v_s, a_s, b_s, g_s = _rwkv_prep(c_s, prev_rkv, prev_lora, params)
        y_s, wkv_s = _rwkv_seq((r_s, lw_s, k_s, v_s, a_s, b_s), state_wkv[l], DB, DS)
        o_rw_s = _rwkv_post(y_s, r_s, k_s, v_s, g_s, params)

        o_att_p, k_p, v_p, ki_p = _dsa_prompt(c_p, tabs_p, B, S)

        def rows8(lo, w):
            return jnp.pad(req3(lo, w), ((0, 0), (0, ROWS - DS), (0, 0)))

        csel = (rows8(C_Q, ATT_WIDTH), rows8(C_QI, IDX_HEADS * IDX_DIM), rows8(C_AK, KV_WIDTH),
                rows8(C_AV, KV_WIDTH), rows8(C_KI, LANES))
        wi_s = rows8(C_WI, IDX_HEADS)
        wrow = jnp.transpose(wi_s, (0, 2, 1)).reshape(DB, 1, IDX_HEADS * ROWS)
        o_att_s8, k_s8, ki_s8 = _dsa_sample(
            csel, wrow, tabs_s, page_table, cache_k[l].reshape(-1, page * KV_HEADS, HEAD_DIM),
            cache_v[l].reshape(-1, page * KV_HEADS, HEAD_DIM), jnp.swapaxes(cache_kidx[l], 1, 2), DS)
        o_att_s = o_att_s8[:, :DS].reshape(DB * DS, ATT_WIDTH)

        p_rw_b, p_att_b, w_o_b = p_rw[l].astype(bf16), p_att[l].astype(bf16), w_o[l].astype(bf16)
        h_p = _outproj(_merge(o_rw_p, o_att_p, c_p, p_rw_b, p_att_b), hp, w_o_b)
        h_s = _outproj(_merge(o_rw_s, o_att_s, c_s, p_rw_b, p_att_b), hs, w_o_b)

        w_up_b, w_down_b = w_up[l].astype(bf16), w_down[l].astype(bf16)
        cv_args = (row2(norm2[l]), row2(norm_f), w_up_b, conv_w[l], row2(conv_b[l]), w_down_b)
        hp, tail_p = _convglu(h_p, jnp.zeros((B, CONV_W - 1, d_ff), dt), *cv_args, B, 1, last)
        h_s_tm = h_s.reshape(DB, DS, D).transpose(1, 0, 2).reshape(DS * DB, D)
        cprev_tm = state_conv[l].transpose(1, 0, 2).reshape(1, (CONV_W - 1) * DB, d_ff)
        hs_tm, tail_s = _convglu(h_s_tm, cprev_tm, *cv_args, 1, DB, last)
        hs = hs_tm.reshape(DS, DB, D).transpose(1, 0, 2).reshape(DB * DS, D)

        outs["kp"].append(k_p.reshape(B, S // page, page, KV_HEADS, HEAD_DIM))
        outs["vp"].append(v_p.reshape(B, S // page, page, KV_HEADS, HEAD_DIM))
        outs["kip"].append(ki_p.reshape(B, S // page, page, IDX_DIM))
        outs["ks"].append(k_s8[:, :DS].reshape(DB, DS, KV_HEADS, HEAD_DIM))
        outs["vs"].append(c_s[:, C_AV:C_AV + KV_WIDTH].reshape(DB, DS, KV_HEADS, HEAD_DIM))
        outs["kis"].append(ki_s8[:, :DS])
        outs["wkvp"].append(_pair_unblock(wkv_p))
        outs["wkvs"].append(wkv_s)
        outs["shp"].append(_rw_cols_from_layout(c_p.reshape(B, S, NP_COLS)[:, -1]))
        outs["shs"].append(_rw_cols_from_layout(c_s[DS - 1::DS]))
        outs["cvp"].append(tail_p)
        outs["cvs"].append(tail_s.reshape(CONV_W - 1, DB, d_ff).transpose(1, 0, 2))

    y_prompt = hp.reshape(B, S, D)
    y_sample = hs.reshape(DB, DS, D)
    st = lambda k: jnp.stack(outs[k])
    return (y_prompt, y_sample, st("kp"), st("vp"), st("kip"), st("ks"), st("vs"), st("kis"), st("wkvp"),
            st("wkvs"), st("shp"), st("shs"), st("cvp"), st("cvs"))
```

```python
import functools
import math

import numpy as np
import jax
import jax.numpy as jnp
from jax import lax
from jax.experimental import pallas as pl
from jax.experimental.pallas import tpu as pltpu

f32 = jnp.float32
bf16 = jnp.bfloat16
i32 = jnp.int32

RW_HEADS = 16
RW_HEAD = 64
RW_WIDTH = RW_HEADS * RW_HEAD
D_DECAY_LORA = 96
D_AAA_LORA = 96
D_GATE_LORA = 128
GN_EPS = 64e-5
ATT_HEADS = 8
KV_HEADS = 2
HEAD_DIM = 128
ATT_WIDTH = ATT_HEADS * HEAD_DIM
KV_WIDTH = KV_HEADS * HEAD_DIM
IDX_HEADS = 16
IDX_DIM = 64
IDX_W_SCALE = (IDX_HEADS * IDX_DIM) ** -0.5
TOPK_MAX = 256
Q_BLOCK = 128
ROPE_THETA = 10000.0
CONV_W = 3
RMS_EPS = 1e-6

LANES = 128
NEG_BIG = -1e30
INT_MIN = -(2 ** 31)

C_R, C_K, C_V, C_Q, C_GRW, C_GATT, C_QI = 0, 1024, 2048, 3072, 4096, 6144, 8192
C_LORA = 9216
C_KI, C_AK, C_AV, C_WI = 9600, 9728, 9984, 10240
NP_COLS = 10368
LORA_W = 384
RKV_W = 3 * RW_WIDTH

VMEM_LIMIT = 56 * 1024 * 1024
CONVGLU_VMEM_LIMIT = 61 * 1024 * 1024


def _cparams(sem, vmem_limit=VMEM_LIMIT):
    return pltpu.CompilerParams(dimension_semantics=sem, vmem_limit_bytes=vmem_limit)


def _bdot(a, b):
    return jnp.dot(a.astype(bf16), b.astype(bf16), preferred_element_type=f32)


def _bdot_nt(a, b):
    return lax.dot_general(a.astype(bf16), b.astype(bf16), (((1,), (1,)), ((), ())), preferred_element_type=f32)


def _bdot_tn(a, b):
    return lax.dot_general(a.astype(bf16), b.astype(bf16), (((0,), (0,)), ((), ())), preferred_element_type=f32)


def _split3(x):
    x1 = x.astype(bf16)
    r1 = x - x1.astype(f32)
    x2 = r1.astype(bf16)
    x3 = (r1 - x2.astype(f32)).astype(bf16)
    return x1, x2, x3


def _split2(x):
    x1 = x.astype(bf16)
    return x1, (x - x1.astype(f32)).astype(bf16)


def _headsum(x):
    ri = lax.broadcasted_iota(i32, (LANES, LANES), 0) // RW_HEAD
    ci = lax.broadcasted_iota(i32, (LANES, LANES), 1) // RW_HEAD
    bd = (ri == ci).astype(bf16)
    outs = []
    for i in range(x.shape[1] // LANES):
        hi, lo = _split2(x[:, i * LANES:(i + 1) * LANES])
        outs.append(jnp.dot(hi, bd, preferred_element_type=f32) + jnp.dot(lo, bd, preferred_element_type=f32))
    return jnp.concatenate(outs, axis=1)


def _softplus(x):
    return jnp.maximum(x, 0.0) + jnp.log(1.0 + jnp.exp(-jnp.abs(x)))


def _rope(x, cos, sin, half):
    w = x.shape[1]
    reps = w // LANES
    if reps > 1:
        cos = jnp.concatenate([cos] * reps, axis=1)
        sin = jnp.concatenate([sin] * reps, axis=1)
    if 2 * half == LANES and w == LANES:
        partner = pltpu.roll(x, half, axis=1)
    else:
        lane = lax.broadcasted_iota(i32, (1, w), 1)
        first = (lane % (2 * half)) < half
        partner = jnp.where(first, pltpu.roll(x, w - half, axis=1), pltpu.roll(x, half, axis=1))
    return x * cos + partner * sin


def _inproj_body(x_ref, g_ref, w_ref, o_ref, n_ref):
    @pl.when(pl.program_id(1) == 0)
    def _():
        x = x_ref[...]
        ms = jnp.mean(x * x, axis=-1, keepdims=True)
        n_ref[...] = (x * lax.rsqrt(ms + RMS_EPS) * g_ref[...]).astype(bf16)

    o_ref[...] = lax.dot_general(n_ref[...], w_ref[...], (((1,), (1,)), ((), ())), preferred_element_type=f32)


def _in_proj(x2d, gain, w_t):
    m, d = x2d.shape
    tm = 1024 if m % 1024 == 0 else min(512, m)
    tn = NP_COLS // 9
    return pl.pallas_call(
        _inproj_body,
        grid=(m // tm, NP_COLS // tn),
        in_specs=[
            pl.BlockSpec((tm, d), lambda i, j: (i, 0)),
            pl.BlockSpec((1, d), lambda i, j: (0, 0)),
            pl.BlockSpec((tn, d), lambda i, j: (j, 0)),
        ],
        out_specs=pl.BlockSpec((tm, tn), lambda i, j: (i, j)),
        out_shape=jax.ShapeDtypeStruct((m, NP_COLS), f32),
        scratch_shapes=[pltpu.VMEM((tm, d), bf16)],
        compiler_params=_cparams(("parallel", "arbitrary")),
        name="in_proj",
    )(x2d, gain, w_t)


RW_PARAM_NAMES = ("mu_rkv", "mu_lora", "w0", "w2", "a0", "a2", "g2", "k_k", "k_a", "r_k", "lnx_w", "lnx_b")


def _rw_prep(c_rkv, c_lora, p_rkv, p_lora, P):
    m = c_rkv + (p_rkv - c_rkv) * P["mu_rkv"]
    ml = c_lora + (p_lora - c_lora) * P["mu_lora"]
    r, k, v = m[:, :RW_WIDTH], m[:, RW_WIDTH:2 * RW_WIDTH], m[:, 2 * RW_WIDTH:]
    wl, al, gl = ml[:, :LANES], ml[:, LANES:2 * LANES], ml[:, 2 * LANES:]
    w_log = -_softplus(-(P["w0"] + _bdot(jnp.tanh(wl), P["w2"]))) - 0.5
    logw = -jnp.exp(w_log)
    asig = jax.nn.sigmoid(P["a0"] + _bdot(al, P["a2"]))
    g = _bdot(jax.nn.sigmoid(gl), P["g2"])
    kk = k * P["k_k"]
    kkn = kk * lax.rsqrt(jnp.maximum(_headsum(kk * kk), 1e-24))
    k2 = k * (1.0 + (asig - 1.0) * P["k_a"])
    return r, logw, k2, v, -kkn, kkn * asig, g


def _rw_post(y, r, k2, v, g, P):
    inv_n = 1.0 / RW_HEAD
    mean = _headsum(y) * inv_n
    d = y - mean
    var = _headsum(d * d) * inv_n
    yn = d * lax.rsqrt(var + GN_EPS) * P["lnx_w"] + P["lnx_b"]
    bonus = _headsum(r * k2 * P["r_k"]) * v
    return (yn + bonus) * g


def _chunk_pre(at, rt, bt, kt, v, masks, nsteps):
    keep, eye2, colh, mA, _ = masks
    R = range(len(at))
    C = at[0].shape[0]
    X = [jnp.concatenate([at[p], rt[p]], axis=0) for p in R]
    scA = [jnp.where(keep, _bdot_nt(jnp.where(mA, X[p], 0.0), jnp.concatenate([bt[p], kt[p]], axis=0)), 0.0)
           for p in R]
    scB = [jnp.where(keep, _bdot_nt(jnp.where(mA, 0.0, X[p]), jnp.concatenate([kt[p], bt[p]], axis=0)), 0.0)
           for p in R]

    def bdiag(x):
        return jnp.concatenate([jnp.where(colh, x, 0.0), jnp.where(colh, 0.0, x)], axis=0)

    L = [jnp.where(colh, scA[p][:C], scB[p][:C]) for p in R]
    T = [eye2 + L[p] for p in R]
    if nsteps > 0:
        Pw = [_bdot(L[p], bdiag(L[p])) for p in R]
        for i in range(nsteps):
            if i < nsteps - 1:
                Z = [_bdot(Pw[p], jnp.concatenate([bdiag(T[p]), bdiag(Pw[p])], axis=1)) for p in R]
                T = [T[p] + Z[p][:, :2 * C] for p in R]
                Pw = [Z[p][:, 2 * C:] for p in R]
            else:
                T = [T[p] + _bdot(Pw[p], bdiag(T[p])) for p in R]
    lkv = [_bdot(jnp.where(colh, scB[p][:C], scA[p][:C]),
                 jnp.concatenate([jnp.where(mA, 0.0, v[p]), jnp.where(mA, v[p], 0.0)], axis=0)) for p in R]
    readout = [jnp.concatenate([scA[p][C:], scB[p][C:]], axis=1) for p in R]
    return T, lkv, readout


def _chunk_seq(at, rt, be, ke, v, wc, T, lkv, readout, S, masks):
    _, _, _, mA, bdmask = masks
    R = range(len(at))
    C = at[0].shape[0]
    XS = [_bdot_nt(jnp.concatenate([at[p], rt[p]], axis=0), S[p]) for p in R]
    G = [XS[p][:C] + lkv[p] for p in R]
    U = [_bdot(T[p], jnp.concatenate([jnp.where(mA, G[p], 0.0), jnp.where(mA, 0.0, G[p])], axis=0)) for p in R]
    Y = [XS[p][C:] + _bdot(readout[p], jnp.concatenate([jnp.where(mA, U[p], 0.0), jnp.where(mA, v[p], 0.0),
                                                        jnp.where(mA, 0.0, v[p]), jnp.where(mA, 0.0, U[p])], axis=0))
         for p in R]
    dS = [_bdot_tn(jnp.concatenate([U[p], v[p]], axis=0), jnp.concatenate([be[p], ke[p]], axis=0)) for p in R]
    S_new = [S[p] * wc[p] + jnp.where(bdmask, dS[p], 0.0) for p in R]
    return Y, S_new


def _chunk_masks(C):
    row = lax.broadcasted_iota(i32, (2 * C, 2 * C), 0)
    col = lax.broadcasted_iota(i32, (2 * C, 2 * C), 1)
    t = jnp.where(row >= C, row - C, row)
    s = jnp.where(col >= C, col - C, col)
    keep = (s < t) | ((row >= C) & (s == t))
    hrow = lax.broadcasted_iota(i32, (C, 2 * C), 0)
    hcol = lax.broadcasted_iota(i32, (C, 2 * C), 1)
    eye2 = (hrow == jnp.where(hcol >= C, hcol - C, hcol)).astype(f32)
    colh = hcol < C
    mA = lax.broadcasted_iota(i32, (1, LANES), 1) < RW_HEAD
    r2 = lax.broadcasted_iota(i32, (LANES, LANES), 0) // RW_HEAD
    c2 = lax.broadcasted_iota(i32, (LANES, LANES), 1) // RW_HEAD
    return keep, eye2, colh, mA, r2 == c2


def _rwkv_prompt_body(chunk, rkv_ref, lora_ref, sp_rkv_ref, sp_lora_ref, s0_ref, *rest):
    np_ = len(RW_PARAM_NAMES)
    P = {n: rest[i][...] for i, n in enumerate(RW_PARAM_NAMES)}
    o_ref, sout_ref = rest[np_], rest[np_ + 1]
    (S_ref, car_rkv, car_lora, at_s, rt_s, bt_s, kt_s, be_s, ke_s, v_s, cum_s, y_s, t_s, lkv_s, ro_s) = rest[np_ + 2:]
    t = pl.program_id(1)
    tt = rkv_ref.shape[0]
    C = chunk

    @pl.when(t == 0)
    def _():
        S_ref[...] = s0_ref[0]
        car_rkv[...] = sp_rkv_ref[0]
        car_lora[...] = sp_lora_ref[0]

    c_rkv = rkv_ref[...]
    c_lora = lora_ref[...]
    first = lax.broadcasted_iota(i32, (tt, 1), 0) == 0
    p_rkv = jnp.where(first, car_rkv[...], pltpu.roll(c_rkv, 1, axis=0))
    p_lora = jnp.where(first, car_lora[...], pltpu.roll(c_lora, 1, axis=0))
    car_rkv[...] = c_rkv[tt - 1:tt, :]
    car_lora[...] = c_lora[tt - 1:tt, :]
    r, logw, k2, v, a, b, g = _rw_prep(c_rkv, c_lora, p_rkv, p_lora, P)

    ri = lax.broadcasted_iota(i32, (tt, tt), 0)
    ci = lax.broadcasted_iota(i32, (tt, tt), 1)
    same = ri // C == ci // C
    tril = (same & (ci <= ri)).astype(bf16)
    triu = (same & (ci > ri)).astype(bf16)
    pieces = _split3(logw)
    cum = sum(jnp.dot(tril, piece, preferred_element_type=f32) for piece in pieces)
    rev = sum(jnp.dot(triu, piece, preferred_element_type=f32) for piece in pieces)
    iw = jnp.exp(-cum)
    ew = jnp.exp(rev)
    at_s[...] = a * jnp.exp(cum - logw)
    rt_s[...] = r * jnp.exp(cum)
    bt_s[...] = b * iw
    kt_s[...] = k2 * iw
    be_s[...] = b * ew
    ke_s[...] = k2 * ew
    v_s[...] = v
    cum_s[...] = cum
    masks = _chunk_masks(C)
    nsteps = max(int(math.ceil(math.log2(C))) - 1, 0)
    pairs = range(RW_WIDTH // LANES)

    lanes = [slice(p * LANES, (p + 1) * LANES) for p in pairs]
    wide = [slice(p * 4 * C, (p + 1) * 4 * C) for p in pairs]

    def ld(ref, r0, cols=lanes):
        return [ref[pl.ds(r0, C), cols[p]] for p in pairs]

    npre = next(u for u in (4, 2, 1) if (tt // C) % u == 0)

    def pre_body(gi, carry):
        r0s = [pl.multiple_of((gi * npre + u) * C, C) for u in range(npre)]
        cat = lambda ref: sum((ld(ref, r0) for r0 in r0s), [])
        T, lkv, readout = _chunk_pre(cat(at_s), cat(rt_s), cat(bt_s), cat(kt_s), cat(v_s), masks, nsteps)
        for u, r0 in enumerate(r0s):
            for p in pairs:
                q = u * len(pairs) + p
                t_s[pl.ds(r0, C), lanes[p]] = T[q]
                lkv_s[pl.ds(r0, C), lanes[p]] = lkv[q]
                ro_s[pl.ds(r0, C), wide[p]] = readout[q]
        return carry

    def seq_body(ci_, carry):
        r0 = pl.multiple_of(ci_ * C, C)
        last8 = pl.multiple_of(r0 + C - 8, 8)
        wc = [jnp.exp(cum_s[pl.ds(last8, 8), lanes[p]][7:8]) for p in pairs]
        Y, S_new = _chunk_seq(ld(at_s, r0), ld(rt_s, r0), ld(be_s, r0), ld(ke_s, r0), ld(v_s, r0), wc, ld(t_s, r0),
                              ld(lkv_s, r0), ld(ro_s, r0, wide), [S_ref[p] for p in pairs], masks)
        for p in pairs:
            y_s[pl.ds(r0, C), lanes[p]] = Y[p]
            S_ref[p] = S_new[p]
        return carry

    lax.fori_loop(0, tt // C // npre, pre_body, 0)
    lax.fori_loop(0, tt // C, seq_body, 0)
    o_ref[...] = _rw_post(y_s[...], r, k2, v, g, P)

    @pl.when(t == pl.num_programs(1) - 1)
    def _():
        sout_ref[0] = S_ref[...]


def _const_spec(shape):
    nd = len(shape)
    return pl.BlockSpec(shape, lambda *_: (0,) * nd)


def _rwkv_prompt(c, sp_rkv, sp_lora, s0, params, nb, seq):
    tt = min(256, seq)
    chunk = min(64, tt)
    nt = seq // tt
    npairs = RW_WIDTH // LANES
    in_specs = [
        pl.BlockSpec((tt, RKV_W), lambda b, t: (b * nt + t, 0)),
        pl.BlockSpec((tt, LORA_W), lambda b, t: (b * nt + t, C_LORA // LORA_W)),
        pl.BlockSpec((1, 1, RKV_W), lambda b, t: (b, 0, 0)),
        pl.BlockSpec((1, 1, LORA_W), lambda b, t: (b, 0, 0)),
        pl.BlockSpec((1, npairs, LANES, LANES), lambda b, t: (b, 0, 0, 0)),
    ] + [_const_spec(params[n].shape) for n in RW_PARAM_NAMES]
    out_specs = [
        pl.BlockSpec((tt, RW_WIDTH), lambda b, t: (b * nt + t, 0)),
        pl.BlockSpec((1, npairs, LANES, LANES), lambda b, t: (b, 0, 0, 0)),
    ]
    scratch = [pltpu.VMEM((npairs, LANES, LANES), f32), pltpu.VMEM((1, RKV_W), f32), pltpu.VMEM((1, LORA_W), f32)]
    assert 2 * chunk == LANES, "a head pair's chunk matrices fill one 128-lane tile"
    scratch += [pltpu.VMEM((tt, RW_WIDTH), f32) for _ in range(11)] + [pltpu.VMEM((tt, 2 * RW_WIDTH), f32)]
    return pl.pallas_call(
        functools.partial(_rwkv_prompt_body, chunk),
        grid=(nb, nt),
        in_specs=in_specs,
        out_specs=out_specs,
        out_shape=[jax.ShapeDtypeStruct((nb * seq, RW_WIDTH), f32),
                   jax.ShapeDtypeStruct((nb, npairs, LANES, LANES), f32)],
        scratch_shapes=scratch,
        compiler_params=_cparams(("parallel", "arbitrary")),
        name="rwkv_prompt",
    )(c, c, sp_rkv, sp_lora, s0, *[params[n] for n in RW_PARAM_NAMES])


def _rwkv_prep_body(rkv_ref, lora_ref, prkv_ref, plora_ref, *rest):
    np_ = len(RW_PARAM_NAMES)
    P = {n: rest[i][...] for i, n in enumerate(RW_PARAM_NAMES)}
    outs = rest[np_:]
    vals = _rw_prep(rkv_ref[...], lora_ref[...], prkv_ref[...], plora_ref[...], P)
    for o, v in zip(outs, vals):
        o[...] = v


def _rwkv_prep(c, prev_rkv, prev_lora, params):
    m = c.shape[0]
    tm = min(512, m)
    in_specs = [
        pl.BlockSpec((tm, RKV_W), lambda i: (i, 0)),
        pl.BlockSpec((tm, LORA_W), lambda i: (i, C_LORA // LORA_W)),
        pl.BlockSpec((tm, RKV_W), lambda i: (i, 0)),
        pl.BlockSpec((tm, LORA_W), lambda i: (i, 0)),
    ] + [_const_spec(params[n].shape) for n in RW_PARAM_NAMES]
    return pl.pallas_call(
        _rwkv_prep_body,
        grid=(m // tm,),
        in_specs=in_specs,
        out_specs=[pl.BlockSpec((tm, RW_WIDTH), lambda i: (i, 0)) for _ in range(7)],
        out_shape=[jax.ShapeDtypeStruct((m, RW_WIDTH), f32) for _ in range(7)],
        compiler_params=_cparams(("parallel",)),
        name="rwkv_prep",
    )(c, c, prev_rkv, prev_lora, *[params[n] for n in RW_PARAM_NAMES])


def _rwkv_seq_body(r_ref, lw_ref, k_ref, v_ref, a_ref, b_ref, s_ref, y_ref, sout_ref):
    steps = r_ref.shape[0]
    w = [jnp.exp(lw_ref[t]) for t in range(steps)]
    for i in range(RW_HEAD):
        Si = s_ref[0, i]
        for t in range(steps):
            sa = jnp.sum(Si * a_ref[t], axis=0, keepdims=True)
            Si = Si * w[t] + sa * b_ref[t] + v_ref[t, i:i + 1, :] * k_ref[t]
            y_ref[t, i:i + 1, :] = jnp.sum(Si * r_ref[t], axis=0, keepdims=True)
        sout_ref[0, i] = Si


def _rwkv_seq(ops, s0, nreq, steps):
    ops_t = [o.reshape(nreq, steps, RW_WIDTH).transpose(1, 2, 0) for o in ops]
    vec = pl.BlockSpec((steps, RW_HEAD, nreq), lambda h: (0, h, 0))
    st = pl.BlockSpec((1, RW_HEAD, RW_HEAD, nreq), lambda h: (h, 0, 0, 0))
    y_t, s_t = pl.pallas_call(
        _rwkv_seq_body,
        grid=(RW_HEADS,),
        in_specs=[vec] * 6 + [st],
        out_specs=[vec, st],
        out_shape=[jax.ShapeDtypeStruct((steps, RW_WIDTH, nreq), f32),
                   jax.ShapeDtypeStruct((RW_HEADS, RW_HEAD, RW_HEAD, nreq), f32)],
        compiler_params=_cparams(("parallel",)),
        name="rwkv_seq",
    )(*ops_t, s0.transpose(1, 2, 3, 0))
    return y_t.transpose(2, 0, 1).reshape(nreq * steps, RW_WIDTH), s_t.transpose(3, 0, 1, 2)


def _rwkv_post_body(y_ref, r_ref, k_ref, v_ref, g_ref, *rest):
    np_ = len(RW_PARAM_NAMES)
    P = {n: rest[i][...] for i, n in enumerate(RW_PARAM_NAMES)}
    rest[np_][...] = _rw_post(y_ref[...], r_ref[...], k_ref[...], v_ref[...], g_ref[...], P)


def _rwkv_post(y, r, k2, v, g, params):
    m = y.shape[0]
    tm = min(512, m)
    return pl.pallas_call(
        _rwkv_post_body,
        grid=(m // tm,),
        in_specs=[pl.BlockSpec((tm, RW_WIDTH), lambda i: (i, 0)) for _ in range(5)]
        + [_const_spec(params[n].shape) for n in RW_PARAM_NAMES],
        out_specs=pl.BlockSpec((tm, RW_WIDTH), lambda i: (i, 0)),
        out_shape=jax.ShapeDtypeStruct((m, RW_WIDTH), f32),
        compiler_params=_cparams(("parallel",)),
        name="rwkv_post",
    )(y, r, k2, v, g, *[params[n] for n in RW_PARAM_NAMES])


def _sort_key(score):
    bits = pltpu.bitcast(score, i32)
    key = jnp.where(bits < 0, bits ^ jnp.int32(0x7FFFFFFF), bits)
    return jnp.where(score == 0.0, jnp.int32(0), key)


def _kth_largest(keys, k):
    def body(i, ts):
        bit = lax.shift_left(jnp.int32(1), jnp.int32(31) - i)
        cands = [t + bit for t in ts]
        cnts = [jnp.sum((key >= c).astype(f32), axis=1, keepdims=True) for key, c in zip(keys, cands)]
        return tuple(jnp.where(n >= k, c, t) for n, c, t in zip(cnts, cands, ts))

    init = tuple(jnp.full((key.shape[0], 1), INT_MIN, i32) for key in keys)
    return lax.fori_loop(0, 32, body, init, unroll=4)


def _row_groups(x, n):
    step = x.shape[0] // n
    return [x[g * step:(g + 1) * step] for g in range(n)]


def _dsa_prompt_body(topk, q_ref, qi_ref, wi_ref, k_ref, v_ref, ki_ref, cosk_ref, sink_ref, cosi_ref, sini_ref,
                     o_ref, kout_ref, vout_ref, kiout_ref, kb_ref, vb_ref, kib_ref, bias_ref):
    qb = pl.program_id(1)
    seq = k_ref.shape[0]
    nq = q_ref.shape[0]

    @pl.when(qb == 0)
    def _():
        cos, sin = cosk_ref[...], sink_ref[...]
        v = v_ref[...]
        kr = jnp.concatenate([_rope(k_ref[:, h * HEAD_DIM:(h + 1) * HEAD_DIM], cos, sin, HEAD_DIM // 2)
                              for h in range(KV_HEADS)], axis=1)
        for h in range(KV_HEADS):
            kout_ref[pl.ds(h, seq, stride=KV_HEADS), :] = kr[:, h * HEAD_DIM:(h + 1) * HEAD_DIM]
            vout_ref[pl.ds(h, seq, stride=KV_HEADS), :] = v[:, h * HEAD_DIM:(h + 1) * HEAD_DIM]
        kb_ref[...] = kr.astype(bf16)
        vb_ref[...] = v.astype(bf16)
        kir = _rope(ki_ref[...], cosi_ref[...], sini_ref[...], IDX_DIM // 2)
        kiout_ref[...] = kir[:, :IDX_DIM]
        kib_ref[...] = (kir + pltpu.roll(kir, IDX_DIM, axis=1)).astype(bf16)

    r0 = pl.multiple_of(qb * nq, nq)
    cosq, sinq = cosk_ref[pl.ds(r0, nq), :], sink_ref[pl.ds(r0, nq), :]
    cosqi, sinqi = cosi_ref[pl.ds(r0, nq), :], sini_ref[pl.ds(r0, nq), :]
    q = _rope(q_ref[...], cosq, sinq, HEAD_DIM // 2).astype(bf16)
    qi = _rope(qi_ref[...], cosqi, sinqi, IDX_DIM // 2)
    wi = wi_ref[...] * IDX_W_SCALE
    lane = lax.broadcasted_iota(i32, (1, LANES), 1)
    scale = HEAD_DIM ** -0.5 * math.log2(math.e)
    hpg = ATT_HEADS // KV_HEADS

    def process(ext):
        kib = kib_ref[:ext, :]
        sc = None
        for h in range(IDX_HEADS):
            pair = qi[:, (h // 2) * LANES:(h // 2 + 1) * LANES]
            mine = (lane < IDX_DIM) if h % 2 == 0 else (lane >= IDX_DIM)
            term = jnp.maximum(_bdot_nt(jnp.where(mine, pair, 0.0), kib), 0.0) * wi[:, h:h + 1]
            sc = term if sc is None else sc + term

        qpos = r0 + lax.broadcasted_iota(i32, (nq, 1), 0)
        kpos = lax.broadcasted_iota(i32, (1, ext), 1)
        causal = kpos <= qpos
        key = jnp.where(causal, _sort_key(sc), INT_MIN)
        thr = jnp.concatenate(_kth_largest(_row_groups(key, 4), float(topk)), axis=0)
        bias_ref[:, :ext] = jnp.where(causal & (key >= thr), 0.0, NEG_BIG)

        def qk(h):
            kg = kb_ref[:ext, (h // hpg) * HEAD_DIM:(h // hpg + 1) * HEAD_DIM]
            return _bdot_nt(q[:, h * HEAD_DIM:(h + 1) * HEAD_DIM], kg)

        s_next = qk(0)
        for h in range(ATT_HEADS):
            s = s_next * scale + bias_ref[:, :ext]
            if h + 1 < ATT_HEADS:
                s_next = qk(h + 1)
            m = jnp.max(s, axis=1, keepdims=True)
            p = jnp.exp2(s - m)
            l = jnp.sum(p, axis=1, keepdims=True)
            vg = vb_ref[:ext, (h // hpg) * HEAD_DIM:(h // hpg + 1) * HEAD_DIM]
            o = jnp.dot(p.astype(bf16), vg, preferred_element_type=f32)
            o_ref[:, h * HEAD_DIM:(h + 1) * HEAD_DIM] = o / l

    nvar = next(v for v in (8, 4, 2, 1) if (seq // nq) % v == 0)
    per = seq // nq // nvar
    for var in range(nvar):
        pl.when(qb // per == var)(functools.partial(process, (var + 1) * per * nq))


def _dsa_prompt(c, tabs, nb, seq):
    nq = Q_BLOCK
    nblk = seq // nq
    topk = min(TOPK_MAX, seq // 4)
    cosk, sink, cosi, sini = tabs
    row = lambda b, j: b * nblk + j
    in_specs = [
        pl.BlockSpec((nq, ATT_WIDTH), lambda b, j: (row(b, j), C_Q // ATT_WIDTH)),
        pl.BlockSpec((nq, IDX_HEADS * IDX_DIM), lambda b, j: (row(b, j), C_QI // (IDX_HEADS * IDX_DIM))),
        pl.BlockSpec((nq, LANES), lambda b, j: (row(b, j), C_WI // LANES)),
        pl.BlockSpec((seq, KV_WIDTH), lambda b, j: (b, C_AK // KV_WIDTH)),
        pl.BlockSpec((seq, KV_WIDTH), lambda b, j: (b, C_AV // KV_WIDTH)),
        pl.BlockSpec((seq, LANES), lambda b, j: (b, C_KI // LANES)),
    ] + [_const_spec((seq, LANES)) for _ in range(4)]
    out_specs = [
        pl.BlockSpec((nq, ATT_WIDTH), lambda b, j: (row(b, j), 0)),
        pl.BlockSpec((seq * KV_HEADS, HEAD_DIM), lambda b, j: (b, 0)),
        pl.BlockSpec((seq * KV_HEADS, HEAD_DIM), lambda b, j: (b, 0)),
        pl.BlockSpec((seq, IDX_DIM), lambda b, j: (b, 0)),
    ]
    scratch = [pltpu.VMEM((seq, KV_WIDTH), bf16), pltpu.VMEM((seq, KV_WIDTH), bf16), pltpu.VMEM((seq, LANES), bf16),
               pltpu.VMEM((nq, seq), f32)]
    return pl.pallas_call(
        functools.partial(_dsa_prompt_body, topk),
        grid=(nb, nblk),
        in_specs=in_specs,
        out_specs=out_specs,
        out_shape=[jax.ShapeDtypeStruct((nb * seq, ATT_WIDTH), f32),
                   jax.ShapeDtypeStruct((nb * seq * KV_HEADS, HEAD_DIM), f32),
                   jax.ShapeDtypeStruct((nb * seq * KV_HEADS, HEAD_DIM), f32),
                   jax.ShapeDtypeStruct((nb * seq, IDX_DIM), f32)],
        scratch_shapes=scratch,
        compiler_params=_cparams(("parallel", "arbitrary")),
        name="dsa_prompt",
    )(c, c, c, c, c, c, cosk, sink, cosi, sini)


ROWS = 8


def _page_copies(pt_ref, req, n_pages, srcs_dsts_sems):
    out = []
    for p in range(n_pages):
        pg = pt_ref[req, p]
        for hbm, dst, sem in srcs_dsts_sems:
            out.append(pltpu.make_async_copy(hbm.at[pg], dst(p), sem))
    return out


def _dsa_select_body(topk, n_new, group, pt_ref, qi_ref, wrow_ref, kin_ref, cosi_ref, sini_ref, cki_hbm,
                     sel_ref, kiout_ref, kibuf, sems):
    i = pl.program_id(0)
    nsteps = pl.num_programs(0)
    n_pages = pt_ref.shape[1]
    page = cki_hbm.shape[2]
    past = n_pages * page
    G = group

    def copies(step, slot):
        out = []
        for j in range(G):
            out += _page_copies(pt_ref, step * G + j, n_pages,
                                [(cki_hbm, lambda p, j=j: kibuf.at[slot, j, :, pl.ds(p * page, page)],
                                  sems.at[slot])])
        return out

    slot = i % 2

    @pl.when(i == 0)
    def _():
        for cp in copies(0, 0):
            cp.start()

    @pl.when(i + 1 < nsteps)
    def _():
        for cp in copies(i + 1, 1 - slot):
            cp.start()

    cosi, sini = cosi_ref[...], sini_ref[...]
    li = lax.broadcasted_iota(i32, (LANES, LANES), 0)
    lo = lax.broadcasted_iota(i32, (LANES, LANES), 1)
    eye = (li == lo).astype(f32)
    qis, kins, wcols = [], [], []
    for j in range(G):
        qi = _rope(qi_ref[j], cosi, sini, IDX_DIM // 2)
        kin = _rope(kin_ref[j], cosi, sini, IDX_DIM // 2)[:, :IDX_DIM]
        kiout_ref[j] = kin
        kins.append(jnp.concatenate([kin, jnp.zeros((LANES - ROWS, IDX_DIM), f32)], axis=0))
        qis.append(jnp.concatenate([qi[:, h * IDX_DIM:(h + 1) * IDX_DIM] for h in range(IDX_HEADS)], axis=0))
        wrow = wrow_ref[0, :, j * LANES:(j + 1) * LANES] * IDX_W_SCALE
        wcols.append(jnp.sum(eye * wrow, axis=1, keepdims=True))

    for cp in copies(i, slot):
        cp.wait()

    rows = []
    for j in range(G):
        d = jnp.concatenate([_bdot(qis[j], kibuf[slot, j]), _bdot_nt(qis[j], kins[j])], axis=1)
        term = jnp.maximum(d, 0.0) * wcols[j]
        rows.append(sum(term[h * ROWS:(h + 1) * ROWS] for h in range(IDX_HEADS)))
    sc = jnp.concatenate(rows, axis=0)
    nrow = ROWS * G
    width = past + LANES
    t_row = lax.broadcasted_iota(i32, (nrow, width), 0) % ROWS
    col = lax.broadcasted_iota(i32, (nrow, width), 1)
    valid = (col < past) | ((col - past <= t_row) & (col - past < n_new))
    key = jnp.where(valid, _sort_key(sc), INT_MIN)
    thr = jnp.concatenate(_kth_largest(_row_groups(key, G), float(topk)), axis=0)
    chosen = (valid & (key >= thr)).astype(f32)
    if nrow < LANES:
        chosen = jnp.concatenate([chosen, jnp.zeros((LANES - nrow, width), f32)], axis=0)
    sel_ref[0] = chosen.T


def _dsa_sample_body(n_new, group, pt_ref, q_ref, kn_ref, vn_ref, sel_ref, cosk_ref, sink_ref, ck_hbm, cv_hbm,
                     o_ref, kout_ref, kbuf, vbuf, sems):
    b = pl.program_id(0)
    nreq = pl.num_programs(0)
    n_pages = pt_ref.shape[1]
    prow = ck_hbm.shape[1]
    past = n_pages * prow // KV_HEADS
    G = group

    def copies(req, slot):
        return _page_copies(pt_ref, req, n_pages,
                            [(ck_hbm, lambda p: kbuf.at[slot, pl.ds(p * prow, prow)], sems.at[0, slot]),
                             (cv_hbm, lambda p: vbuf.at[slot, pl.ds(p * prow, prow)], sems.at[1, slot])])

    slot = b % 2

    @pl.when(b == 0)
    def _():
        for cp in copies(0, 0):
            cp.start()

    @pl.when(b + 1 < nreq)
    def _():
        for cp in copies(b + 1, 1 - slot):
            cp.start()

    cosk, sink = cosk_ref[...], sink_ref[...]
    q = _rope(q_ref[0], cosk, sink, HEAD_DIM // 2)
    kn = jnp.concatenate([_rope(kn_ref[0][:, h * HEAD_DIM:(h + 1) * HEAD_DIM], cosk, sink, HEAD_DIM // 2)
                          for h in range(KV_HEADS)], axis=1)
    vn = vn_ref[0]
    kout_ref[0] = kn

    li = lax.broadcasted_iota(i32, (LANES, LANES), 0)
    lo = lax.broadcasted_iota(i32, (LANES, LANES), 1)
    route = ((b % G) * ROWS + lo % ROWS == li) & (lo % ROWS < n_new)
    chosen = jnp.dot(sel_ref[0].astype(bf16), route.astype(bf16), preferred_element_type=f32)
    bias = (chosen - 1.0) * (-NEG_BIG)
    bias_p, bias_n = bias[:past], bias[past:past + ROWS]

    eye = (li == lo).astype(f32)
    scale = HEAD_DIM ** -0.5 * math.log2(math.e)
    hpg = ATT_HEADS // KV_HEADS
    zero_rows = lambda n: [jnp.zeros((n, HEAD_DIM), f32)] if n else []

    for cp in copies(b, slot):
        cp.wait()

    s_p = s_n = None
    for g in range(KV_HEADS):
        heads = [q[:, (g * hpg + r) * HEAD_DIM:(g * hpg + r + 1) * HEAD_DIM] for r in range(hpg)]
        qg = jnp.concatenate(zero_rows(g * hpg * ROWS) + heads + zero_rows(LANES - (g + 1) * hpg * ROWS), axis=0)
        kp = kbuf[slot, pl.ds(g, past, stride=KV_HEADS), :]
        part_p = _bdot_nt(kp, qg)
        part_n = _bdot_nt(kn[:, g * HEAD_DIM:(g + 1) * HEAD_DIM], qg)
        s_p = part_p if s_p is None else s_p + part_p
        s_n = part_n if s_n is None else s_n + part_n
    s_p = s_p * scale + bias_p
    s_n = s_n * scale + bias_n
    m = jnp.maximum(jnp.max(s_p, axis=0, keepdims=True), jnp.max(s_n, axis=0, keepdims=True))
    p_p = jnp.exp2(s_p - m)
    p_n = jnp.exp2(s_n - m)
    l = jnp.sum(p_p, axis=0, keepdims=True) + jnp.sum(p_n, axis=0, keepdims=True)
    l_col = jnp.sum(eye * l, axis=1, keepdims=True)
    for g in range(KV_HEADS):
        vp = vbuf[slot, pl.ds(g, past, stride=KV_HEADS), :]
        o = (_bdot_tn(p_p, vp) + _bdot_tn(p_n, vn[:, g * HEAD_DIM:(g + 1) * HEAD_DIM])) / l_col
        for r in range(hpg):
            h = g * hpg + r
            o_ref[0, :, h * HEAD_DIM:(h + 1) * HEAD_DIM] = o[h * ROWS:(h + 1) * ROWS]


def _dsa_sample(csel, wrow, tabs, page_table, cache_k, cache_v, cache_kidx, n_new):
    q8, qi8, kn8, vn8, kin8 = csel
    nreq = q8.shape[0]
    n_pages = page_table.shape[1]
    page = cache_kidx.shape[2]
    past = n_pages * page
    topk = min(TOPK_MAX, (past + n_new) // 4)
    group = min(LANES // ROWS, nreq)
    cosk, sink, cosi, sini = tabs
    anyspec = pl.BlockSpec(memory_space=pl.ANY)
    tab = pl.BlockSpec((ROWS, LANES), lambda b, pt: (0, 0))

    grp3 = lambda w: pl.BlockSpec((group, ROWS, w), lambda i, pt: (i, 0, 0))
    sel, ki_new = pl.pallas_call(
        functools.partial(_dsa_select_body, topk, n_new, group),
        grid_spec=pltpu.PrefetchScalarGridSpec(
            num_scalar_prefetch=1,
            grid=(nreq // group,),
            in_specs=[grp3(IDX_HEADS * IDX_DIM), pl.BlockSpec((1, 1, group * LANES), lambda i, pt: (i, 0, 0)),
                      grp3(LANES), tab, tab, anyspec],
            out_specs=[pl.BlockSpec((1, past + LANES, LANES), lambda i, pt: (i, 0, 0)), grp3(IDX_DIM)],
            scratch_shapes=[pltpu.VMEM((2, group, IDX_DIM, past), f32), pltpu.SemaphoreType.DMA((2,))],
        ),
        out_shape=[jax.ShapeDtypeStruct((nreq // group, past + LANES, LANES), f32),
                   jax.ShapeDtypeStruct((nreq, ROWS, IDX_DIM), f32)],
        compiler_params=_cparams(("arbitrary",)),
        name="dsa_select",
    )(page_table, qi8, wrow.reshape(nreq // group, 1, group * LANES), kin8, cosi, sini, cache_kidx)

    req3 = lambda w: pl.BlockSpec((1, ROWS, w), lambda b, pt: (b, 0, 0))
    prow = cache_k.shape[1]
    o, k_new = pl.pallas_call(
        functools.partial(_dsa_sample_body, n_new, group),
        grid_spec=pltpu.PrefetchScalarGridSpec(
            num_scalar_prefetch=1,
            grid=(nreq,),
            in_specs=[req3(ATT_WIDTH), req3(KV_WIDTH), req3(KV_WIDTH),
                      pl.BlockSpec((1, past + LANES, LANES), lambda b, pt: (b // group, 0, 0)), tab, tab,
                      anyspec, anyspec],
            out_specs=[req3(ATT_WIDTH), req3(KV_WIDTH)],
            scratch_shapes=[pltpu.VMEM((2, n_pages * prow, HEAD_DIM), f32),
                            pltpu.VMEM((2, n_pages * prow, HEAD_DIM), f32), pltpu.SemaphoreType.DMA((2, 2))],
        ),
        out_shape=[jax.ShapeDtypeStruct((nreq, ROWS, ATT_WIDTH), f32),
                   jax.ShapeDtypeStruct((nreq, ROWS, KV_WIDTH), f32)],
        compiler_params=_cparams(("arbitrary",)),
        name="dsa_sample",
    )(page_table, q8, kn8, vn8, sel, cosk, sink, cache_k, cache_v)
    return o, k_new, ki_new


def _merge_out_body(orw_ref, oatt_ref, grw_ref, gatt_ref, x_ref, prw_ref, patt_ref, wo_ref, o_ref):
    a = jnp.dot(orw_ref[...].astype(bf16), prw_ref[...], preferred_element_type=f32)
    b = jnp.dot(oatt_ref[...].astype(bf16), patt_ref[...], preferred_element_type=f32)
    merged = (jax.nn.sigmoid(grw_ref[...]) * a + jax.nn.sigmoid(gatt_ref[...]) * b).astype(bf16)
    o_ref[...] = x_ref[...] + jnp.dot(merged, wo_ref[...], preferred_element_type=f32)


def _merge_out(o_rw, o_att, c, x2d, p_rw, p_att, w_o):
    m, d = x2d.shape
    tm = min(256, m)
    once = pl.Buffered(1)
    return pl.pallas_call(
        _merge_out_body,
        grid=(m // tm,),
        in_specs=[
            pl.BlockSpec((tm, RW_WIDTH), lambda i: (i, 0)),
            pl.BlockSpec((tm, ATT_WIDTH), lambda i: (i, 0)),
            pl.BlockSpec((tm, d), lambda i: (i, C_GRW // d)),
            pl.BlockSpec((tm, d), lambda i: (i, C_GATT // d)),
            pl.BlockSpec((tm, d), lambda i: (i, 0)),
            pl.BlockSpec((RW_WIDTH, d), lambda i: (0, 0), pipeline_mode=once),
            pl.BlockSpec((ATT_WIDTH, d), lambda i: (0, 0), pipeline_mode=once),
            pl.BlockSpec((d, d), lambda i: (0, 0), pipeline_mode=once),
        ],
        out_specs=pl.BlockSpec((tm, d), lambda i: (i, 0)),
        out_shape=jax.ShapeDtypeStruct((m, d), f32),
        compiler_params=_cparams(("parallel",)),
        name="merge_out",
    )(o_rw, o_att, c, c, x2d, p_rw, p_att, w_o)


def _convglu_body(shift, tiles_per_seq, final_norm, h_ref, g2_ref, gf_ref, cprev_ref, wg_ref, wu_ref, cw_ref,
                  cb_ref, wd_ref, o_ref, tail_ref, n_ref, ext_ref, carry_ref, act_ref):
    i = pl.program_id(0)
    j = pl.program_id(1)
    nj = pl.num_programs(1) - 1
    tm = h_ref.shape[0]
    hist = 2 * shift
    base = ext_ref.shape[0] - tm

    def up_stage():
        n = n_ref[...]
        gate = jnp.dot(n, wg_ref[...], preferred_element_type=f32)
        up = jnp.dot(n, wu_ref[...], preferred_element_type=f32)
        ext_ref[base - hist:base, :] = jnp.where(i % tiles_per_seq == 0, cprev_ref[0], carry_ref[j])
        ext_ref[base:, :] = gate
        cw = cw_ref[...]
        c = (cb_ref[...] + ext_ref[base - hist:base - hist + tm, :] * cw[0:1, :]
             + ext_ref[base - shift:base - shift + tm, :] * cw[1:2, :] + gate * cw[2:3, :])
        tail = ext_ref[base + tm - hist:base + tm, :]
        tail_ref[0] = tail
        carry_ref[j] = tail
        act_ref[j % 2] = ((c * jax.nn.sigmoid(c)) * up).astype(bf16)

    def down_stage():
        o_ref[...] += jnp.dot(act_ref[(j + 1) % 2], wd_ref[...], preferred_element_type=f32)

    @pl.when(j == 0)
    def _():
        @pl.when(i == 0)
        def _():
            carry_ref[...] = jnp.zeros_like(carry_ref)

        h = h_ref[...]
        ms = jnp.mean(h * h, axis=-1, keepdims=True)
        n_ref[...] = (h * lax.rsqrt(ms + RMS_EPS) * g2_ref[...]).astype(bf16)
        o_ref[...] = jnp.zeros_like(o_ref)
        up_stage()

    @pl.when((j > 0) & (j < nj))
    def _():
        up_stage()
        down_stage()

    @pl.when(j == nj)
    def _():
        down_stage()
        out = h_ref[...] + o_ref[...]
        if final_norm:
            ms = jnp.mean(out * out, axis=-1, keepdims=True)
            out = out * lax.rsqrt(ms + RMS_EPS) * gf_ref[...]
        o_ref[...] = out


def _convglu(h2d, conv_prev, norm2, norm_f, w_up_b, conv_w, conv_b, w_down_b, nseq_groups, shift, final_norm):
    m, d = h2d.shape
    d_ff = w_down_b.shape[0]
    rows_per_group = m // nseq_groups
    tm = 1024 if rows_per_group % 1024 == 0 else min(512, rows_per_group)
    tf = 512
    nj = d_ff // tf
    tiles_per_seq = rows_per_group // tm
    hist = 2 * shift
    base = ((hist + 7) // 8) * 8
    up_j = lambda j: jnp.minimum(j, nj - 1)
    down_j = lambda j: jnp.maximum(j - 1, 0)
    out, tails = pl.pallas_call(
        functools.partial(_convglu_body, shift, tiles_per_seq, final_norm),
        grid=(m // tm, nj + 1),
        in_specs=[
            pl.BlockSpec((tm, d), lambda i, j: (i, 0)),
            pl.BlockSpec((1, d), lambda i, j: (0, 0)),
            pl.BlockSpec((1, d), lambda i, j: (0, 0)),
            pl.BlockSpec((1, hist, tf), lambda i, j: (i // tiles_per_seq, 0, up_j(j))),
            pl.BlockSpec((d, tf), lambda i, j: (0, up_j(j))),
            pl.BlockSpec((d, tf), lambda i, j: (0, nj + up_j(j))),
            pl.BlockSpec((CONV_W, tf), lambda i, j: (0, up_j(j))),
            pl.BlockSpec((1, tf), lambda i, j: (0, up_j(j))),
            pl.BlockSpec((tf, d), lambda i, j: (down_j(j), 0)),
        ],
        out_specs=[
            pl.BlockSpec((tm, d), lambda i, j: (i, 0)),
            pl.BlockSpec((1, hist, tf), lambda i, j: (i, 0, up_j(j))),
        ],
        out_shape=[jax.ShapeDtypeStruct((m, d), f32), jax.ShapeDtypeStruct((m // tm, hist, d_ff), f32)],
        scratch_shapes=[pltpu.VMEM((tm, d), bf16), pltpu.VMEM((base + tm, tf), f32),
                        pltpu.VMEM((nj, hist, tf), f32), pltpu.VMEM((2, tm, tf), bf16)],
        compiler_params=_cparams(("arbitrary", "arbitrary"), CONVGLU_VMEM_LIMIT),
        name="convglu",
    )(h2d, norm2, norm_f, conv_prev, w_up_b, w_up_b, conv_w, conv_b, w_down_b)
    return out, tails[tiles_per_seq - 1::tiles_per_seq]


def _pad_cols(a, width):
    return jnp.pad(a, [(0, 0)] * (a.ndim - 1) + [(0, width - a.shape[-1])])


def _to_layout_rows(a):
    def pad_rows(x, n):
        return jnp.pad(x, ((0, n - x.shape[0]), (0, 0)))

    rw = 3 * RW_WIDTH
    o = {}
    o["r"], o["k"], o["v"] = a[0:RW_WIDTH], a[RW_WIDTH:2 * RW_WIDTH], a[2 * RW_WIDTH:rw]
    p = rw
    o["wl"] = a[p:p + D_DECAY_LORA]; p += D_DECAY_LORA
    o["al"] = a[p:p + D_AAA_LORA]; p += D_AAA_LORA
    o["gl"] = a[p:p + D_GATE_LORA]; p += D_GATE_LORA
    o["q"] = a[p:p + ATT_WIDTH]; p += ATT_WIDTH
    o["ak"] = a[p:p + KV_WIDTH]; p += KV_WIDTH
    o["av"] = a[p:p + KV_WIDTH]; p += KV_WIDTH
    o["qi"] = a[p:p + IDX_HEADS * IDX_DIM]; p += IDX_HEADS * IDX_DIM
    o["ki"] = a[p:p + IDX_DIM]; p += IDX_DIM
    o["wi"] = a[p:p + IDX_HEADS]; p += IDX_HEADS
    d = (a.shape[0] - p) // 2
    o["grw"], o["gatt"] = a[p:p + d], a[p + d:p + 2 * d]
    return jnp.concatenate([
        o["r"], o["k"], o["v"], o["q"], o["grw"], o["gatt"], o["qi"],
        pad_rows(o["wl"], LANES), pad_rows(o["al"], LANES), o["gl"],
        pad_rows(o["ki"], LANES), o["ak"], o["av"], pad_rows(o["wi"], LANES)], axis=0)


def _rw_cols_layout(a):
    rw = 3 * RW_WIDTH
    wl = a[..., rw:rw + D_DECAY_LORA]
    al = a[..., rw + D_DECAY_LORA:rw + D_DECAY_LORA + D_AAA_LORA]
    gl = a[..., rw + D_DECAY_LORA + D_AAA_LORA:]
    return a[..., :rw], jnp.concatenate([_pad_cols(wl, LANES), _pad_cols(al, LANES), gl], axis=-1)


def _rw_cols_from_layout(c):
    return jnp.concatenate([c[..., :3 * RW_WIDTH], c[..., C_LORA:C_LORA + D_DECAY_LORA],
                            c[..., C_LORA + LANES:C_LORA + LANES + D_AAA_LORA],
                            c[..., C_LORA + 2 * LANES:C_LORA + 3 * LANES]], axis=-1)


def _rope_tables(pos, rows):
    pos = jnp.pad(pos.astype(f32), (0, rows - pos.shape[0]))
    out = []
    for dim in (HEAD_DIM, IDX_DIM):
        half = dim // 2
        inv_freq = 1.0 / (ROPE_THETA ** (jnp.arange(half, dtype=f32) / half))
        ang = pos[:, None] * inv_freq[None, :]
        cos, sin = jnp.cos(ang), jnp.sin(ang)
        reps = LANES // dim
        out.append(jnp.tile(jnp.concatenate([cos, cos], axis=1), (1, reps)))
        out.append(jnp.tile(jnp.concatenate([-sin, sin], axis=1), (1, reps)))
    return tuple(out)


def _pair_unblock(s):
    n = s.shape[0]
    a = s[:, :, :RW_HEAD, :RW_HEAD]
    b = s[:, :, RW_HEAD:, RW_HEAD:]
    return jnp.stack([a, b], axis=2).reshape(n, RW_HEADS, RW_HEAD, RW_HEAD)


def kernel(x_prompt, x_sample, cache_k, cache_v, cache_kidx, state_wkv, state_shift, state_conv, page_table, norm1, w_in, rw_mu, rw_w0, rw_w2, rw_a0, rw_a2, rw_g2, rw_k_k, rw_k_a, rw_r_k, rw_lnx_w, rw_lnx_b, p_rw, p_att, w_o, norm2, w_up, conv_w, conv_b, w_down, norm_f):
    B, S, D = x_prompt.shape
    DB, DS, _ = x_sample.shape
    depth = w_in.shape[0]
    page = cache_k.shape[2]
    n_pages = page_table.shape[1]
    past_len = n_pages * page
    d_ff = w_down.shape[1]
    dt = x_prompt.dtype

    tabs_p = _rope_tables(jnp.arange(S), S)
    tabs_s = _rope_tables(past_len + jnp.arange(DS), ROWS)
    row2 = lambda a: a.reshape(1, -1)

    hp = x_prompt.reshape(B * S, D)
    hs = x_sample.reshape(DB * DS, D)
    outs = {k: [] for k in ("kp", "vp", "kip", "ks", "vs", "kis", "wkvp", "wkvs", "shp", "shs", "cvp", "cvs")}
    for l in range(depth):
        w_in_t = _to_layout_rows(w_in[l].T).astype(bf16)
        mu_rkv, mu_lora = _rw_cols_layout(rw_mu[l][None, :])
        params = dict(
            mu_rkv=mu_rkv, mu_lora=mu_lora, w0=row2(rw_w0[l]),
            w2=jnp.pad(rw_w2[l], ((0, LANES - D_DECAY_LORA), (0, 0))).astype(bf16), a0=row2(rw_a0[l]),
            a2=jnp.pad(rw_a2[l], ((0, LANES - D_AAA_LORA), (0, 0))).astype(bf16), g2=rw_g2[l].astype(bf16),
            k_k=row2(rw_k_k[l]), k_a=row2(rw_k_a[l]), r_k=row2(rw_r_k[l]), lnx_w=row2(rw_lnx_w[l]),
            lnx_b=row2(rw_lnx_b[l]))
        last = l == depth - 1

        c_p = _in_proj(hp, row2(norm1[l]), w_in_t)
        c_s = _in_proj(hs, row2(norm1[l]), w_in_t)

        sp_rkv = jnp.zeros((B, 1, RKV_W), dt)
        sp_lora = jnp.zeros((B, 1, LORA_W), dt)
        s0_p = jnp.zeros((B, RW_HEADS // 2, LANES, LANES), dt)
        o_rw_p, wkv_p = _rwkv_prompt(c_p, sp_rkv, sp_lora, s0_p, params, B, S)

        c_s3 = c_s.reshape(DB, DS, NP_COLS)
        ss_rkv, ss_lora = _rw_cols_layout(state_shift[l])
        prev_rkv = jnp.concatenate([ss_rkv[:, None, :], c_s3[:, :-1, :RKV_W]], axis=1).reshape(DB * DS, RKV_W)
        prev_lora = jnp.concatenate([ss_lora[:, None, :], c_s3[:, :-1, C_LORA:C_LORA + LORA_W]], axis=1)
        prev_lora = prev_lora.reshape(DB * DS, LORA_W)
        r_s, lw_s, k_s, v_s, a_s, b_s, g_s = _rwkv_prep(c_s, prev_rkv, prev_lora, params)
        y_s, wkv_s = _rwkv_seq((r_s, lw_s, k_s, v_s, a_s, b_s), state_wkv[l], DB, DS)
        o_rw_s = _rwkv_post(y_s, r_s, k_s, v_s, g_s, params)

        o_att_p, k_p, v_p, ki_p = _dsa_prompt(c_p, tabs_p, B, S)

        def rows8(lo, w):
            return jnp.pad(c_s3[:, :, lo:lo + w], ((0, 0), (0, ROWS - DS), (0, 0)))

        csel = (rows8(C_Q, ATT_WIDTH), rows8(C_QI, IDX_HEADS * IDX_DIM), rows8(C_AK, KV_WIDTH),
                rows8(C_AV, KV_WIDTH), rows8(C_KI, LANES))
        wi_s = jnp.pad(c_s3[:, :, C_WI:C_WI + IDX_HEADS], ((0, 0), (0, ROWS - DS), (0, 0)))
        wrow = jnp.transpose(wi_s, (0, 2, 1)).reshape(DB, 1, IDX_HEADS * ROWS)
        o_att_s8, k_s8, ki_s8 = _dsa_sample(
            csel, wrow, tabs_s, page_table, cache_k[l].reshape(-1, page * KV_HEADS, HEAD_DIM),
            cache_v[l].reshape(-1, page * KV_HEADS, HEAD_DIM), jnp.swapaxes(cache_kidx[l], 1, 2), DS)
        o_att_s = o_att_s8[:, :DS].reshape(DB * DS, ATT_WIDTH)

        p_rw_b, p_att_b, w_o_b = p_rw[l].astype(bf16), p_att[l].astype(bf16), w_o[l].astype(bf16)
        h_p = _merge_out(o_rw_p, o_att_p, c_p, hp, p_rw_b, p_att_b, w_o_b)
        h_s = _merge_out(o_rw_s, o_att_s, c_s, hs, p_rw_b, p_att_b, w_o_b)

        w_up_b, w_down_b = w_up[l].astype(bf16), w_down[l].astype(bf16)
        cv_args = (row2(norm2[l]), row2(norm_f), w_up_b, conv_w[l], row2(conv_b[l]), w_down_b)
        hp, tail_p = _convglu(h_p, jnp.zeros((B, CONV_W - 1, d_ff), dt), *cv_args, B, 1, last)
        h_s_tm = h_s.reshape(DB, DS, D).transpose(1, 0, 2).reshape(DS * DB, D)
        cprev_tm = state_conv[l].transpose(1, 0, 2).reshape(1, (CONV_W - 1) * DB, d_ff)
        hs_tm, tail_s = _convglu(h_s_tm, cprev_tm, *cv_args, 1, DB, last)
        hs = hs_tm.reshape(DS, DB, D).transpose(1, 0, 2).reshape(DB * DS, D)

        outs["kp"].append(k_p.reshape(B, S // page, page, KV_HEADS, HEAD_DIM))
        outs["vp"].append(v_p.reshape(B, S // page, page, KV_HEADS, HEAD_DIM))
        outs["kip"].append(ki_p.reshape(B, S // page, page, IDX_DIM))
        outs["ks"].append(k_s8[:, :DS].reshape(DB, DS, KV_HEADS, HEAD_DIM))
        outs["vs"].append(c_s3[:, :, C_AV:C_AV + KV_WIDTH].reshape(DB, DS, KV_HEADS, HEAD_DIM))
        outs["kis"].append(ki_s8[:, :DS])
        outs["wkvp"].append(_pair_unblock(wkv_p))
        outs["wkvs"].append(wkv_s)
        outs["shp"].append(_rw_cols_from_layout(c_p.reshape(B, S, NP_COLS)[:, -1]))
        outs["shs"].append(_rw_cols_from_layout(c_s3[:, -1]))
        outs["cvp"].append(tail_p)
        outs["cvs"].append(tail_s.reshape(CONV_W - 1, DB, d_ff).transpose(1, 0, 2))

    y_prompt = hp.reshape(B, S, D)
    y_sample = hs.reshape(DB, DS, D)
    st = lambda k: jnp.stack(outs[k])
    return (y_prompt, y_sample, st("kp"), st("vp"), st("kip"), st("ks"), st("vs"), st("kis"), st("wkvp"),
            st("wkvs"), st("shp"), st("shs"), st("cvp"), st("cvs"))
```
